```python
import math
import jax, jax.numpy as jnp
from jax import lax
import numpy as np

D_MODEL = 1024
BATCH = 2
SEQ = 8192
DEPTH = 2
DEC_BATCH = 32
DEC_SEQ = 4
PAST_LEN = 8192
PAGE_SIZE = 128

HEAD_DIM = 64
SB_WIDTH = D_MODEL // 2
SB_HEADS = SB_WIDTH // HEAD_DIM
ML_WIDTH = D_MODEL // 2
ML_HEADS = 4
ML_HEAD_DIM = ML_WIDTH // ML_HEADS
MOBA_WIDTH = D_MODEL // 2
MOBA_HEADS = MOBA_WIDTH // HEAD_DIM
LRU_WIDTH = D_MODEL // 2
LRU_BLOCKS = 8
LRU_BLOCK_DIM = LRU_WIDTH // LRU_BLOCKS
CONV_WIDTH = 4
ML_CHUNK = 64
SB_QBLOCK = 128
MOBA_BLOCK = 256
MOBA_TOPK = 3
MOBA_QBLOCK = 64
RG_C = 8.0
ROPE_THETA = 10000.0
NORM_EPS = 1e-6
IN_EVEN = 4 * SB_WIDTH + 5 * ML_WIDTH + 2 * ML_HEADS
IN_ODD = 4 * MOBA_WIDTH + 2 * LRU_WIDTH
F32 = jnp.float32

kernel_name = "hybrid_sb_mlstm_moba_rglru_step"


def _split_cols(a, widths):
    return jnp.split(a, [int(v) for v in np.cumsum(widths)[:-1]], axis=-1)


def _rmsnorm(x, g):
    xf = x.astype(F32)
    return xf * lax.rsqrt(jnp.mean(xf * xf, axis=-1, keepdims=True) + NORM_EPS) * g.astype(F32)


def _adaln(c, w, b):
    mod = jax.nn.silu(c.astype(F32)) @ w.astype(F32) + b.astype(F32)
    shift, scale, gate = jnp.split(mod, 3, axis=-1)
    return shift[:, None], scale[:, None], gate[:, None]


def _rope(x, pos):
    half = x.shape[-1] // 2
    freqs = ROPE_THETA ** (-jnp.arange(half, dtype=F32) / half)
    ang = pos.astype(F32)[:, None] * freqs[None, :]
    cos = jnp.cos(ang)[None, :, None, :]
    sin = jnp.sin(ang)[None, :, None, :]
    x1, x2 = x[..., :half], x[..., half:]
    return jnp.concatenate([x1 * cos - x2 * sin, x1 * sin + x2 * cos], axis=-1)


def _causal_conv(x, w, b, buf):
    t = x.shape[1]
    xp = jnp.concatenate([buf.astype(F32), x], axis=1)
    y = b.astype(F32)
    for j in range(CONV_WIDTH):
        y = y + w[j].astype(F32) * xp[:, j:j + t]
    return y, xp[:, xp.shape[1] - (CONV_WIDTH - 1):]


def _sweep(fn, q, q_start, block):
    bsz, tq = q.shape[:2]
    if tq > block and tq % block == 0:
        nq = tq // block
        qb = jnp.moveaxis(q.reshape((bsz, nq, block) + q.shape[2:]), 1, 0)
        starts = q_start + block * jnp.arange(nq)
        out = lax.map(lambda a: fn(a[0], a[1] + jnp.arange(block)), (qb, starts))
        return jnp.moveaxis(out, 0, 1).reshape((bsz, tq) + out.shape[3:])
    return fn(q, q_start + jnp.arange(tq))


def _sb_block(q, k, v, q_pos):
    k_pos = jnp.arange(k.shape[1])
    z = jnp.einsum('bqhd,bkhd->bhqk', q, k) * (HEAD_DIM ** -0.5)
    mask = k_pos[None, :] < q_pos[:, None]
    log_stay = jnp.where(mask, jax.nn.log_sigmoid(-z), 0.0)
    after = lax.cumsum(log_stay, axis=3, reverse=True) - log_stay
    w = jnp.where(mask, jnp.exp(jax.nn.log_sigmoid(z) + after), 0.0)
    return jnp.einsum('bhqk,bkhd->bqhd', w, v)


def _stick_breaking(q, k, v, q_start):
    return _sweep(lambda qb, qp: _sb_block(qb, k, v, qp), q, q_start, SB_QBLOCK)


def _moba_block(q, kb, vb, kmean, q_pos):
    bsz, tq, nh, hd = q.shape
    nb = kb.shape[2]
    qh = jnp.swapaxes(q, 1, 2)
    own = q_pos // MOBA_BLOCK
    past = jnp.arange(nb)[None, :] < own[:, None]
    gate = jnp.where(past, jnp.einsum('bhqd,bhnd->bhqn', qh, kmean), -jnp.inf)
    vals, idx = lax.top_k(gate, min(MOBA_TOPK, nb))
    blocks = jnp.concatenate([idx, jnp.broadcast_to(own[None, None, :, None], (bsz, nh, tq, 1))], axis=-1)
    valid = jnp.concatenate([vals > -jnp.inf, jnp.ones((bsz, nh, tq, 1), dtype=bool)], axis=-1)
    bi = jnp.arange(bsz)[:, None, None, None]
    hi = jnp.arange(nh)[None, :, None, None]
    kg = kb[bi, hi, blocks]
    vg = vb[bi, hi, blocks]
    key_pos = blocks[..., None] * MOBA_BLOCK + jnp.arange(MOBA_BLOCK)
    mask = valid[..., None] & (key_pos <= q_pos[None, None, :, None, None])
    s = jnp.einsum('bhqd,bhqknd->bhqkn', qh, kg) * (hd ** -0.5)
    s = jnp.where(mask, s, -jnp.inf)
    p = jax.nn.softmax(s.reshape(bsz, nh, tq, -1), axis=-1).reshape(s.shape)
    o = jnp.einsum('bhqkn,bhqknd->bhqd', p, vg)
    return jnp.swapaxes(o, 1, 2)


def _moba(q, k, v, q_start):
    bsz, tk, nh, hd = k.shape
    nb = -(-tk // MOBA_BLOCK)
    pad = nb * MOBA_BLOCK - tk

    def to_blocks(a):
        a = jnp.pad(a, ((0, 0), (0, pad), (0, 0), (0, 0)))
        return a.reshape(bsz, nb, MOBA_BLOCK, nh, hd).transpose(0, 3, 1, 2, 4)

    kb, vb = to_blocks(k), to_blocks(v)
    kmean = jnp.mean(kb, axis=3)
    return _sweep(lambda qb, qp: _moba_block(qb, kb, vb, kmean, qp), q, q_start, MOBA_QBLOCK)


def _mlstm(q, k, v, log_i, log_f, c0, n0, m0):
    bsz, seq, nh, dk = q.shape
    dv = v.shape[-1]
    L = math.gcd(seq, ML_CHUNK)
    nc = seq // L

    def chunks(a):
        return jnp.moveaxis(a.reshape((bsz, nc, L) + a.shape[2:]), 1, 0)

    causal = jnp.tril(jnp.ones((L, L), dtype=bool))

    def step(carry, blk):
        c, n, m = carry
        qc, kc, vc, li, lf = blk
        qc, kc, vc = jnp.swapaxes(qc, 1, 2), jnp.swapaxes(kc, 1, 2), jnp.swapaxes(vc, 1, 2)
        li, lf = jnp.swapaxes(li, 1, 2), jnp.swapaxes(lf, 1, 2)
        F = jnp.cumsum(lf, axis=-1)
        inter = F + m[..., None]
        intra = jnp.where(causal, F[..., :, None] - F[..., None, :] + li[..., None, :], -jnp.inf)
        mt = jnp.maximum(inter, jnp.max(intra, axis=-1))
        w = jnp.exp(intra - mt[..., None])
        g = jnp.exp(inter - mt)
        s = jnp.einsum('bhtd,bhsd->bhts', qc, kc) * w
        num = g[..., None] * jnp.einsum('bhtd,bhde->bhte', qc, c) + jnp.einsum('bhts,bhse->bhte', s, vc)
        den = g * jnp.einsum('bhtd,bhd->bht', qc, n) + jnp.sum(s, axis=-1)
        h = num / jnp.maximum(jnp.abs(den), jnp.exp(-mt))[..., None]
        m_new = mt[..., -1]
        decay = jnp.exp(F[..., -1] + m - m_new)
        ws = jnp.exp(F[..., -1:] - F + li - m_new[..., None])
        c = decay[..., None, None] * c + jnp.einsum('bhs,bhsd,bhse->bhde', ws, kc, vc)
        n = decay[..., None] * n + jnp.einsum('bhs,bhsd->bhd', ws, kc)
        return (c, n, m_new), jnp.swapaxes(h, 1, 2)

    xs = tuple(chunks(a) for a in (q, k, v, log_i, log_f))
    (c, n, m), hs = lax.scan(step, (c0, n0, m0), xs)
    return jnp.moveaxis(hs, 0, 1).reshape(bsz, seq, nh, dv), c, n, m


def _rglru(x, r, i, lam, h0):
    log_a = RG_C * r * jax.nn.log_sigmoid(lam.astype(F32))
    a = jnp.exp(log_a)
    b = jnp.sqrt(-jnp.expm1(2.0 * log_a)) * (i * x)
    b = b.at[:, 0].add(a[:, 0] * h0)

    def combine(e1, e2):
        return e1[0] * e2[0], e2[0] * e1[1] + e2[1]

    _, h = lax.associative_scan(combine, (a, b), axis=1)
    return h, h[:, -1]


def _even_layer(x, c, k_past, v_past, conv_buf, c0, n0, m0,
                norm_g, mod_w, mod_b, w_in, conv_w, conv_b, b_ig, b_fg, w_out):
    bsz, t, _ = x.shape
    p0 = 0 if k_past is None else k_past.shape[1]
    shift, scale, gate = _adaln(c, mod_w, mod_b)
    h = _rmsnorm(x, norm_g) * (1.0 + scale) + shift
    proj = h @ w_in.astype(F32)
    q_a, k_a, v_a, z_a, q_b, k_b, v_b, o_b, z_b, i_b, f_b = _split_cols(
        proj, [SB_WIDTH] * 4 + [ML_WIDTH] * 5 + [ML_HEADS] * 2)
    hs = (bsz, t, SB_HEADS, HEAD_DIM)
    q_a, k_a, v_a = q_a.reshape(hs), k_a.reshape(hs), v_a.reshape(hs)
    k_all = k_a if k_past is None else jnp.concatenate([k_past.astype(F32), k_a], axis=1)
    v_all = v_a if v_past is None else jnp.concatenate([v_past.astype(F32), v_a], axis=1)
    y_a = _stick_breaking(q_a, k_all, v_all, p0).reshape(bsz, t, SB_WIDTH) * jax.nn.silu(z_a)
    qk, conv_new = _causal_conv(jnp.concatenate([q_b, k_b], axis=-1), conv_w, conv_b, conv_buf)
    q_m, k_m = jnp.split(jax.nn.silu(qk), 2, axis=-1)
    ms = (bsz, t, ML_HEADS, ML_HEAD_DIM)
    q_m = q_m.reshape(ms)
    k_m = k_m.reshape(ms) * (ML_HEAD_DIM ** -0.5)
    log_i = i_b + b_ig.astype(F32)
    log_f = jax.nn.log_sigmoid(f_b + b_fg.astype(F32))
    hb, c_new, n_new, m_new = _mlstm(q_m, k_m, v_b.reshape(ms), log_i, log_f, c0, n0, m0)
    y_b = jax.nn.sigmoid(o_b) * hb.reshape(bsz, t, ML_WIDTH) * jax.nn.silu(z_b)
    out = jnp.concatenate([y_a, y_b], axis=-1) @ w_out.astype(F32)
    return x.astype(F32) + gate * out, (k_a, v_a, conv_new, c_new, n_new, m_new)


def _odd_layer(x, c, k_past, v_past, conv_buf, h0,
               norm_g, mod_w, mod_b, w_in, conv_w, conv_b, w_rg, b_rg, w_ig, b_ig, lam, w_out):
    bsz, t, _ = x.shape
    p0 = 0 if k_past is None else k_past.shape[1]
    shift, scale, gate = _adaln(c, mod_w, mod_b)
    h = _rmsnorm(x, norm_g) * (1.0 + scale) + shift
    proj = h @ w_in.astype(F32)
    q_c, k_c, v_c, z_c, x_d, z_d = _split_cols(proj, [MOBA_WIDTH] * 4 + [LRU_WIDTH] * 2)
    hs = (bsz, t, MOBA_HEADS, HEAD_DIM)
    pos = p0 + jnp.arange(t)
    q_c = _rope(q_c.reshape(hs), pos)
    k_c = _rope(k_c.reshape(hs), pos)
    v_c = v_c.reshape(hs)
    k_all = k_c if k_past is None else jnp.concatenate([k_past.astype(F32), k_c], axis=1)
    v_all = v_c if v_past is None else jnp.concatenate([v_past.astype(F32), v_c], axis=1)
    y_c = _moba(q_c, k_all, v_all, p0).reshape(bsz, t, MOBA_WIDTH) * jax.nn.silu(z_c)
    xc, conv_new = _causal_conv(x_d, conv_w, conv_b, conv_buf)
    xg = xc.reshape(bsz, t, LRU_BLOCKS, LRU_BLOCK_DIM)
    r = jax.nn.sigmoid(jnp.einsum('btgi,gij->btgj', xg, w_rg.astype(F32)).reshape(bsz, t, LRU_WIDTH) + b_rg.astype(F32))
    i = jax.nn.sigmoid(jnp.einsum('btgi,gij->btgj', xg, w_ig.astype(F32)).reshape(bsz, t, LRU_WIDTH) + b_ig.astype(F32))
    hd, h_last = _rglru(xc, r, i, lam, h0.astype(F32))
    y_d = hd * jax.nn.silu(z_d)
    out = jnp.concatenate([y_c, y_d], axis=-1) @ w_out.astype(F32)
    return x.astype(F32) + gate * out, (k_c, v_c, conv_new, h_last)


def setup_inputs(seed: int = 0) -> dict:
    key = jax.random.key(seed)
    ks = iter(jax.random.split(key, 48))
    ne, no = (DEPTH + 1) // 2, DEPTH // 2
    n_pages = PAST_LEN // PAGE_SIZE
    n_used = DEC_BATCH * n_pages
    n_pool = n_used + n_used // 4
    D = D_MODEL

    def nrm(shape, scale):
        return jax.random.normal(next(ks), shape, F32) * scale

    inp = {}
    inp["x_prompt"] = nrm((BATCH, SEQ, D), 1.0)
    inp["x_sample"] = nrm((DEC_BATCH, DEC_SEQ, D), 1.0)
    inp["c_prompt"] = nrm((BATCH, D), 1.0)
    inp["c_sample"] = nrm((DEC_BATCH, D), 1.0)
    inp["page_table"] = jax.random.permutation(next(ks), n_pool)[:n_used].reshape(DEC_BATCH, n_pages).astype(jnp.int32)
    inp["cache_k_sb"] = nrm((ne, n_pool, PAGE_SIZE, SB_HEADS, HEAD_DIM), 1.0)
    inp["cache_v_sb"] = nrm((ne, n_pool, PAGE_SIZE, SB_HEADS, HEAD_DIM), 1.0)
    inp["state_conv_mlstm"] = nrm((ne, DEC_BATCH, CONV_WIDTH - 1, 2 * ML_WIDTH), 1.0)
    inp["state_c_mlstm"] = nrm((ne, DEC_BATCH, ML_HEADS, ML_HEAD_DIM, ML_HEAD_DIM), 0.1)
    inp["state_n_mlstm"] = nrm((ne, DEC_BATCH, ML_HEADS, ML_HEAD_DIM), 0.1)
    inp["state_m_mlstm"] = nrm((ne, DEC_BATCH, ML_HEADS), 1.0)
    inp["cache_k_moba"] = nrm((no, n_pool, PAGE_SIZE, MOBA_HEADS, HEAD_DIM), 1.0)
    inp["cache_v_moba"] = nrm((no, n_pool, PAGE_SIZE, MOBA_HEADS, HEAD_DIM), 1.0)
    inp["state_conv_rglru"] = nrm((no, DEC_BATCH, CONV_WIDTH - 1, LRU_WIDTH), 1.0)
    inp["state_h_rglru"] = nrm((no, DEC_BATCH, LRU_WIDTH), 0.5)
    inp["norm_g_even"] = 1.0 + nrm((ne, D), 0.02)
    inp["mod_w_even"] = nrm((ne, D, 3 * D), 0.5 * D ** -0.5)
    inp["mod_b_even"] = nrm((ne, 3 * D), 0.1)
    inp["w_in_even"] = nrm((ne, D, IN_EVEN), D ** -0.5)
    inp["conv_w_even"] = nrm((ne, CONV_WIDTH, 2 * ML_WIDTH), CONV_WIDTH ** -0.5)
    inp["conv_b_even"] = nrm((ne, 2 * ML_WIDTH), 0.02)
    inp["b_igate_even"] = nrm((ne, ML_HEADS), 0.1)
    inp["b_fgate_even"] = jnp.linspace(3.0, 6.0, ML_HEADS, dtype=F32)[None, :] + nrm((ne, ML_HEADS), 0.1)
    inp["w_out_even"] = nrm((ne, SB_WIDTH + ML_WIDTH, D), (SB_WIDTH + ML_WIDTH) ** -0.5)
    inp["norm_g_odd"] = 1.0 + nrm((no, D), 0.02)
    inp["mod_w_odd"] = nrm((no, D, 3 * D), 0.5 * D ** -0.5)
    inp["mod_b_odd"] = nrm((no, 3 * D), 0.1)
    inp["w_in_odd"] = nrm((no, D, IN_ODD), D ** -0.5)
    inp["conv_w_odd"] = nrm((no, CONV_WIDTH, LRU_WIDTH), CONV_WIDTH ** -0.5)
    inp["conv_b_odd"] = nrm((no, LRU_WIDTH), 0.02)
    inp["w_rgate_odd"] = nrm((no, LRU_BLOCKS, LRU_BLOCK_DIM, LRU_BLOCK_DIM), LRU_BLOCK_DIM ** -0.5)
    inp["b_rgate_odd"] = nrm((no, LRU_WIDTH), 0.1)
    inp["w_igate_odd"] = nrm((no, LRU_BLOCKS, LRU_BLOCK_DIM, LRU_BLOCK_DIM), LRU_BLOCK_DIM ** -0.5)
    inp["b_igate_odd"] = nrm((no, LRU_WIDTH), 0.1)
    u = jax.random.uniform(next(ks), (no, LRU_WIDTH), F32, 0.9, 0.999)
    s = u ** (1.0 / RG_C)
    inp["lru_lambda_odd"] = jnp.log(s) - jnp.log1p(-s)
    inp["w_out_odd"] = nrm((no, MOBA_WIDTH + LRU_WIDTH, D), (MOBA_WIDTH + LRU_WIDTH) ** -0.5)
    inp["final_g"] = 1.0 + nrm((D,), 0.02)
    return inp


def reference(x_prompt, x_sample, c_prompt, c_sample, page_table,
              cache_k_sb, cache_v_sb, state_conv_mlstm, state_c_mlstm, state_n_mlstm, state_m_mlstm,
              cache_k_moba, cache_v_moba, state_conv_rglru, state_h_rglru,
              norm_g_even, mod_w_even, mod_b_even, w_in_even, conv_w_even, conv_b_even,
              b_igate_even, b_fgate_even, w_out_even,
              norm_g_odd, mod_w_odd, mod_b_odd, w_in_odd, conv_w_odd, conv_b_odd,
              w_rgate_odd, b_rgate_odd, w_igate_odd, b_igate_odd, lru_lambda_odd, w_out_odd,
              final_g):
    def gather_pages(pool):
        g = pool[page_table]
        return g.reshape((g.shape[0], g.shape[1] * g.shape[2]) + g.shape[3:])

    bp = x_prompt.shape[0]
    xp, xs = x_prompt, x_sample
    p_even, s_even, p_odd, s_odd = [], [], [], []
    for l in range(DEPTH):
        j = l // 2
        if l % 2 == 0:
            w = (norm_g_even[j], mod_w_even[j], mod_b_even[j], w_in_even[j], conv_w_even[j], conv_b_even[j],
                 b_igate_even[j], b_fgate_even[j], w_out_even[j])
            xp, st = _even_layer(xp, c_prompt, None, None,
                                 jnp.zeros((bp, CONV_WIDTH - 1, 2 * ML_WIDTH), F32),
                                 jnp.zeros((bp, ML_HEADS, ML_HEAD_DIM, ML_HEAD_DIM), F32),
                                 jnp.zeros((bp, ML_HEADS, ML_HEAD_DIM), F32),
                                 jnp.zeros((bp, ML_HEADS), F32), *w)
            p_even.append(st)
            xs, st = _even_layer(xs, c_sample, gather_pages(cache_k_sb[j]), gather_pages(cache_v_sb[j]),
                                 state_conv_mlstm[j], state_c_mlstm[j].astype(F32),
                                 state_n_mlstm[j].astype(F32), state_m_mlstm[j].astype(F32), *w)
            s_even.append(st)
        else:
            w = (norm_g_odd[j], mod_w_odd[j], mod_b_odd[j], w_in_odd[j], conv_w_odd[j], conv_b_odd[j],
                 w_rgate_odd[j], b_rgate_odd[j], w_igate_odd[j], b_igate_odd[j], lru_lambda_odd[j], w_out_odd[j])
            xp, st = _odd_layer(xp, c_prompt, None, None,
                                jnp.zeros((bp, CONV_WIDTH - 1, LRU_WIDTH), F32),
                                jnp.zeros((bp, LRU_WIDTH), F32), *w)
            p_odd.append(st)
            xs, st = _odd_layer(xs, c_sample, gather_pages(cache_k_moba[j]), gather_pages(cache_v_moba[j]),
                                state_conv_rglru[j], state_h_rglru[j], *w)
            s_odd.append(st)
    y_prompt = _rmsnorm(xp, final_g).astype(x_prompt.dtype)
    y_sample = _rmsnorm(xs, final_g).astype(x_sample.dtype)
    pk_sb, pv_sb, pconv_m, pc_m, pn_m, pm_m = [jnp.stack(a) for a in zip(*p_even)]
    sk_sb, sv_sb, sconv_m, sc_m, sn_m, sm_m = [jnp.stack(a) for a in zip(*s_even)]
    pk_mb, pv_mb, pconv_d, ph_d = [jnp.stack(a) for a in zip(*p_odd)]
    sk_mb, sv_mb, sconv_d, sh_d = [jnp.stack(a) for a in zip(*s_odd)]
    return (y_prompt, y_sample,
            pk_sb, pv_sb, pconv_m, pc_m, pn_m, pm_m, pk_mb, pv_mb, pconv_d, ph_d,
            sk_sb, sv_sb, sconv_m, sc_m, sn_m, sm_m, sk_mb, sv_mb, sconv_d, sh_d)
```

```python
import functools

import jax
import jax.numpy as jnp
from jax import lax
from jax.experimental import pallas as pl
from jax.experimental.pallas import tpu as pltpu

F32 = jnp.float32
BF16 = jnp.bfloat16
HIGHEST = lax.Precision.HIGHEST

HEAD_DIM = 64
HALF_WIDTH = 512
ML_HEADS = 4
ML_HEAD_DIM = 128
CONV_WIDTH = 4
MOBA_BLOCK = 256
MOBA_TOPK = 3
RG_C = 8.0
ROPE_THETA = 10000.0
NORM_EPS = 1e-6
LANES = 128
SUBLANES = 8
VMEM_LIMIT = 56 * 1024 * 1024
NEG_INF = float("-inf")


def _params(sem):
    return pltpu.CompilerParams(dimension_semantics=sem, vmem_limit_bytes=VMEM_LIMIT)


def _sigmoid(x):
    return 1.0 / (1.0 + jnp.exp(-x))


def _silu(x):
    return x * _sigmoid(x)


def _softplus(x):
    return jnp.maximum(x, 0.0) + jnp.log1p(jnp.exp(-jnp.abs(x)))


def _log_sigmoid(x):
    return -_softplus(-x)


def _dot(a, b, precision=None):
    return jnp.dot(a, b, preferred_element_type=F32, precision=precision)


def _dot_nt(a, b, precision=None):
    return lax.dot_general(a, b, (((1,), (1,)), ((), ())), preferred_element_type=F32, precision=precision)


def _dot_tn(a, b, precision=None):
    return lax.dot_general(a, b, (((0,), (0,)), ((), ())), preferred_element_type=F32, precision=precision)


def _split_bf16(x):
    hi = x.astype(BF16)
    lo = (x - hi.astype(F32)).astype(BF16)
    return hi, lo


def _iota(shape, dim):
    return lax.broadcasted_iota(jnp.int32, shape, dim)


def _mod_kernel(c_ref, w_ref, b_ref, o_ref):
    o_ref[...] = _dot(_silu(c_ref[...]), w_ref[...], HIGHEST) + b_ref[...]


def _modulation(c, w, b):
    rows, d = c.shape
    n = w.shape[1]
    tn = 512
    return pl.pallas_call(
        _mod_kernel,
        grid=(n // tn,),
        in_specs=[pl.BlockSpec((rows, d), lambda j: (0, 0)),
                  pl.BlockSpec((d, tn), lambda j: (0, j)),
                  pl.BlockSpec((1, tn), lambda j: (0, j))],
        out_specs=pl.BlockSpec((rows, tn), lambda j: (0, j)),
        out_shape=jax.ShapeDtypeStruct((rows, n), F32),
        compiler_params=_params(("arbitrary",)),
        name="modulation",
    )(c, w, b.reshape(1, n))


def _normed(x_ref, shift_ref, scale_ref, g_ref):
    x = x_ref[...]
    ms = jnp.mean(x * x, axis=-1, keepdims=True)
    h = x * lax.rsqrt(ms + NORM_EPS) * g_ref[...]
    return h * (1.0 + scale_ref[0]) + shift_ref[0]


def _in_even_kernel(x_ref, shift_ref, scale_ref, g_ref, w_ref, wg_ref, wgt_ref,
                    qa_ref, ka_ref, kab_ref, va_ref, vab_ref, za_ref, qk_ref, vb_ref, gb_ref,
                    gt_ref, gtt_ref):
    h = _normed(x_ref, shift_ref, scale_ref, g_ref)
    hb = h.astype(BF16)
    W = HALF_WIDTH

    def proj(c):
        return _dot(hb, w_ref[:, c * W:(c + 1) * W])

    qa_ref[...] = (proj(0) * (HEAD_DIM ** -0.5)).astype(BF16)
    ka = proj(1)
    ka_ref[...] = ka
    kab_ref[...] = ka.astype(BF16)
    va = proj(2)
    va_ref[...] = va
    vab_ref[...] = va.astype(BF16)
    za_ref[...] = _silu(proj(3))
    qk_ref[:, 0:W] = proj(4)
    qk_ref[:, W:2 * W] = proj(5)
    vb_ref[...] = proj(6)
    gb_ref[...] = _sigmoid(proj(7)) * _silu(proj(8))
    gt_ref[0] = _dot(h, wg_ref[...], HIGHEST)
    gtt_ref[0] = _dot_nt(wgt_ref[...], h, HIGHEST)


def _in_proj_even(x, shift, scale, g, w_in, tm):
    rows, d = x.shape
    W = HALF_WIDTH
    nt = rows // tm
    groups = shift.shape[0]
    tiles_per_group = nt // groups
    w_main = w_in[:, :9 * W].astype(BF16)
    w_gate = jnp.pad(w_in[:, 9 * W:], ((0, 0), (0, LANES - 2 * ML_HEADS)))
    w_gate_t = w_in[:, 9 * W:].T
    row_blk = lambda n, dt: (pl.BlockSpec((tm, n), lambda i: (i, 0)), jax.ShapeDtypeStruct((rows, n), dt))
    outs = [row_blk(W, BF16), row_blk(W, F32), row_blk(W, BF16), row_blk(W, F32), row_blk(W, BF16),
            row_blk(W, F32), row_blk(2 * W, F32), row_blk(W, F32), row_blk(W, F32),
            (pl.BlockSpec((1, tm, LANES), lambda i: (i, 0, 0)), jax.ShapeDtypeStruct((nt, tm, LANES), F32)),
            (pl.BlockSpec((1, 2 * ML_HEADS, tm), lambda i: (i, 0, 0)),
             jax.ShapeDtypeStruct((nt, 2 * ML_HEADS, tm), F32))]
    rg = shift.shape[1]
    return pl.pallas_call(
        _in_even_kernel,
        grid=(nt,),
        in_specs=[pl.BlockSpec((tm, d), lambda i: (i, 0)),
                  pl.BlockSpec((1, rg, d), lambda i: (i // tiles_per_group, 0, 0)),
                  pl.BlockSpec((1, rg, d), lambda i: (i // tiles_per_group, 0, 0)),
                  pl.BlockSpec((1, d), lambda i: (0, 0)),
                  pl.BlockSpec((d, 9 * W), lambda i: (0, 0)),
                  pl.BlockSpec((d, LANES), lambda i: (0, 0)),
                  pl.BlockSpec((2 * ML_HEADS, d), lambda i: (0, 0))],
        out_specs=[o[0] for o in outs],
        out_shape=[o[1] for o in outs],
        compiler_params=_params(("arbitrary",)),
        name="in_proj_even",
    )(x, shift, scale, g.reshape(1, d), w_main, w_gate, w_gate_t)


def _in_odd_kernel(x_ref, shift_ref, scale_ref, g_ref, w_ref, cos_ref, sin_ref,
                   qb_ref, qf_ref, kc_ref, kcb_ref, vc_ref, vcb_ref, zc_ref, xd_ref, zd_ref, km_ref):
    h = _normed(x_ref, shift_ref, scale_ref, g_ref)
    hb = h.astype(BF16)
    W = HALF_WIDTH

    def proj(c):
        return _dot(hb, w_ref[:, c * W:(c + 1) * W])

    cos = cos_ref[...]
    sin = sin_ref[...]
    first_half = (_iota(cos.shape, 1) % HEAD_DIM) < (HEAD_DIM // 2)

    def rope(x):
        partner = jnp.where(first_half, pltpu.roll(x, W - HEAD_DIM // 2, 1), pltpu.roll(x, HEAD_DIM // 2, 1))
        return x * cos + partner * sin

    q = rope(proj(0))
    qf_ref[...] = q
    qb_ref[...] = (q * (HEAD_DIM ** -0.5)).astype(BF16)
    k = rope(proj(1))
    kc_ref[...] = k
    kcb_ref[...] = k.astype(BF16)
    km_ref[0] = jnp.mean(k, axis=0, keepdims=True)
    v = proj(2)
    vc_ref[...] = v
    vcb_ref[...] = v.astype(BF16)
    zc_ref[...] = _silu(proj(3))
    xd_ref[...] = proj(4)
    zd_ref[...] = _silu(proj(5))


def _in_proj_odd(x, shift, scale, g, w_in, cos, sin, tm):
    rows, d = x.shape
    W = HALF_WIDTH
    nt = rows // tm
    groups = shift.shape[0]
    tiles_per_group = nt // groups
    rg = shift.shape[1]
    row_blk = lambda n, dt: (pl.BlockSpec((tm, n), lambda i: (i, 0)), jax.ShapeDtypeStruct((rows, n), dt))
    outs = [row_blk(W, BF16), row_blk(W, F32), row_blk(W, F32), row_blk(W, BF16), row_blk(W, F32),
            row_blk(W, BF16), row_blk(W, F32), row_blk(W, F32), row_blk(W, F32),
            (pl.BlockSpec((1, 1, W), lambda i: (i, 0, 0)), jax.ShapeDtypeStruct((nt, 1, W), F32))]
    return pl.pallas_call(
        _in_odd_kernel,
        grid=(nt,),
        in_specs=[pl.BlockSpec((tm, d), lambda i: (i, 0)),
                  pl.BlockSpec((1, rg, d), lambda i: (i // tiles_per_group, 0, 0)),
                  pl.BlockSpec((1, rg, d), lambda i: (i // tiles_per_group, 0, 0)),
                  pl.BlockSpec((1, d), lambda i: (0, 0)),
                  pl.BlockSpec((d, 6 * W), lambda i: (0, 0)),
                  pl.BlockSpec((tm, W), lambda i: (i % tiles_per_group, 0)),
                  pl.BlockSpec((tm, W), lambda i: (i % tiles_per_group, 0))],
        out_specs=[o[0] for o in outs],
        out_shape=[o[1] for o in outs],
        compiler_params=_params(("arbitrary",)),
        name="in_proj_odd",
    )(x, shift, scale, g.reshape(1, d), w_in.astype(BF16), cos, sin)


def _rope_tables(positions):
    half = HEAD_DIM // 2
    freqs = ROPE_THETA ** (-jnp.arange(half, dtype=F32) / half)
    ang = positions.astype(F32)[:, None] * freqs[None, :]
    cos = jnp.cos(ang)
    sin = jnp.sin(ang)
    heads = HALF_WIDTH // HEAD_DIM
    cos_t = jnp.tile(jnp.concatenate([cos, cos], axis=-1), (1, heads))
    sin_t = jnp.tile(jnp.concatenate([-sin, sin], axis=-1), (1, heads))
    return cos_t, sin_t


def _out_proj_kernel(ya_ref, yb_ref, x_ref, gate_ref, w_ref, fg_ref, o_ref, *, final_norm):
    W = HALF_WIDTH
    out = _dot(ya_ref[...].astype(BF16), w_ref[0:W, :]) + _dot(yb_ref[...].astype(BF16), w_ref[W:2 * W, :])
    y = x_ref[...] + gate_ref[0] * out
    if final_norm:
        ms = jnp.mean(y * y, axis=-1, keepdims=True)
        y = y * lax.rsqrt(ms + NORM_EPS) * fg_ref[...]
    o_ref[...] = y


def _out_proj(ya, yb, x, gate, w_out, final_g, tm, final_norm):
    rows, d = x.shape
    W = HALF_WIDTH
    nt = rows // tm
    groups = gate.shape[0]
    tiles_per_group = nt // groups
    rg = gate.shape[1]
    return pl.pallas_call(
        functools.partial(_out_proj_kernel, final_norm=final_norm),
        grid=(nt,),
        in_specs=[pl.BlockSpec((tm, W), lambda i: (i, 0)),
                  pl.BlockSpec((tm, W), lambda i: (i, 0)),
                  pl.BlockSpec((tm, d), lambda i: (i, 0)),
                  pl.BlockSpec((1, rg, d), lambda i: (i // tiles_per_group, 0, 0)),
                  pl.BlockSpec((2 * W, d), lambda i: (0, 0)),
                  pl.BlockSpec((1, d), lambda i: (0, 0))],
        out_specs=pl.BlockSpec((tm, d), lambda i: (i, 0)),
        out_shape=jax.ShapeDtypeStruct((rows, d), F32),
        compiler_params=_params(("arbitrary",)),
        name="out_proj",
    )(ya, yb, x, gate, w_out.astype(BF16), final_g.reshape(1, d))


def _sb_prompt_kernel(q_ref, k_ref, v_ref, zs_ref, o_ref, *, tq):
    qi = pl.program_id(2)
    q = q_ref[...]
    lane = _iota((tq, LANES), 1)
    head0 = lane < HEAD_DIM
    qs = (jnp.where(head0, q, jnp.zeros_like(q)), jnp.where(head0, jnp.zeros_like(q), q))
    row = _iota((tq, tq), 0)
    col = _iota((tq, tq), 1)
    suffix = jnp.where(row >= col, 1.0, 0.0).astype(BF16)
    causal = col < row

    def block(j, carry, diag):
        acc, c0, c1 = carry
        start = pl.multiple_of(j * tq, tq)
        kb = k_ref[pl.ds(start, tq), :]
        vb = v_ref[pl.ds(start, tq), :]
        pvs, cs = [], []
        for qh, ch in zip(qs, (c0, c1)):
            z = _dot_nt(qh, kb)
            ls = -_softplus(z)
            if diag:
                ls = jnp.where(causal, ls, 0.0)
            hi, lo = _split_bf16(ls)
            cum = _dot(hi, suffix) + _dot(lo, suffix) + ch
            w = jnp.exp(z + cum)
            if diag:
                w = jnp.where(causal, w, 0.0)
            pvs.append(_dot(w.astype(BF16), vb))
            cs.append(cum[:, 0:1])
        return acc + jnp.where(head0, pvs[0], pvs[1]), cs[0], cs[1]

    zero_c = jnp.zeros((tq, 1), F32)
    carry = block(qi, (jnp.zeros((tq, LANES), F32), zero_c, zero_c), True)
    carry = lax.fori_loop(0, qi, lambda jj, c: block(qi - 1 - jj, c, False), carry)
    o_ref[...] = carry[0] * zs_ref[...]


def _sb_prompt(q, k, v, zs, batch, seq, tq):
    rows, W = q.shape
    nq = seq // tq
    pairs = W // LANES
    return pl.pallas_call(
        functools.partial(_sb_prompt_kernel, tq=tq),
        grid=(batch, pairs, nq),
        in_specs=[pl.BlockSpec((tq, LANES), lambda b, p, i: (b * nq + i, p)),
                  pl.BlockSpec((seq, LANES), lambda b, p, i: (b, p)),
                  pl.BlockSpec((seq, LANES), lambda b, p, i: (b, p)),
                  pl.BlockSpec((tq, LANES), lambda b, p, i: (b * nq + i, p))],
        out_specs=pl.BlockSpec((tq, LANES), lambda b, p, i: (b * nq + i, p)),
        out_shape=jax.ShapeDtypeStruct((rows, W), F32),
        compiler_params=_params(("arbitrary", "arbitrary", "arbitrary")),
        name="sb_prompt",
    )(q, k, v, zs)


def _moba_prompt_kernel(q_ref, qf_ref, k_ref, v_ref, km_ref, zs_ref, o_ref, kpad_ref, *, tq, nb):
    qi = pl.program_id(2)

    @pl.when(qi == 0)
    def _():
        kpad_ref[...] = jnp.zeros_like(kpad_ref)
        kpad_ref[0:nb, :] = km_ref[0]

    q = q_ref[...]
    qf = qf_ref[...]
    lane = _iota((tq, LANES), 1)
    lane_f = lane.astype(F32)
    head0 = lane < HEAD_DIM
    row = _iota((tq, tq), 0)
    col = _iota((tq, tq), 1)
    kmean = kpad_ref[...]

    qs, sels = [], []
    for h in range(2):
        hm = head0 if h == 0 else jnp.logical_not(head0)
        qs.append(jnp.where(hm, q, jnp.zeros_like(q)))
        g = _dot_nt(jnp.where(hm, qf, 0.0), kmean, HIGHEST)
        g = jnp.where(lane < qi, g, NEG_INF)
        sel = jnp.zeros((tq, LANES), F32)
        for _ in range(MOBA_TOPK):
            mx = jnp.max(g, axis=1, keepdims=True)
            is_max = jnp.logical_and(g == mx, mx > NEG_INF)
            idx = jnp.min(jnp.where(is_max, lane_f, float(LANES)), axis=1, keepdims=True)
            pick = lane_f == idx
            sel = jnp.where(pick, 1.0, sel)
            g = jnp.where(pick, NEG_INF, g)
        sels.append(sel)

    start = pl.multiple_of(qi * tq, tq)
    kb = k_ref[pl.ds(start, tq), :]
    vb = v_ref[pl.ds(start, tq), :]
    state = []
    for h in range(2):
        s = jnp.where(col <= row, _dot_nt(qs[h], kb), NEG_INF)
        m = jnp.max(s, axis=1, keepdims=True)
        p = jnp.exp(s - m)
        state += [m, jnp.sum(p, axis=1, keepdims=True), _dot(p.astype(BF16), vb)]

    def body(n, st):
        start_n = pl.multiple_of(n * tq, tq)
        kn = k_ref[pl.ds(start_n, tq), :]
        vn = v_ref[pl.ds(start_n, tq), :]
        new = []
        for h in range(2):
            m, l, acc = st[3 * h:3 * h + 3]
            chosen = jnp.max(jnp.where(lane == n, sels[h], 0.0), axis=1, keepdims=True)
            s = jnp.where(chosen > 0.0, _dot_nt(qs[h], kn), NEG_INF)
            m_new = jnp.maximum(m, jnp.max(s, axis=1, keepdims=True))
            alpha = jnp.exp(m - m_new)
            p = jnp.exp(s - m_new)
            new += [m_new, alpha * l + jnp.sum(p, axis=1, keepdims=True),
                    alpha * acc + _dot(p.astype(BF16), vn)]
        return tuple(new)

    st = lax.fori_loop(0, qi, body, tuple(state))
    o = jnp.where(head0, st[2] / st[1], st[5] / st[4])
    o_ref[...] = o * zs_ref[...]


def _moba_prompt(q, qf, k, v, kmean, zs, batch, seq):
    rows, W = q.shape
    tq = MOBA_BLOCK
    nq = seq // tq
    pairs = W // LANES
    return pl.pallas_call(
        functools.partial(_moba_prompt_kernel, tq=tq, nb=nq),
        grid=(batch, pairs, nq),
        in_specs=[pl.BlockSpec((tq, LANES), lambda b, p, i: (b * nq + i, p)),
                  pl.BlockSpec((tq, LANES), lambda b, p, i: (b * nq + i, p)),
                  pl.BlockSpec((seq, LANES), lambda b, p, i: (b, p)),
                  pl.BlockSpec((seq, LANES), lambda b, p, i: (b, p)),
                  pl.BlockSpec((1, nq, LANES), lambda b, p, i: (b, 0, p)),
                  pl.BlockSpec((tq, LANES), lambda b, p, i: (b * nq + i, p))],
        out_specs=pl.BlockSpec((tq, LANES), lambda b, p, i: (b * nq + i, p)),
        out_shape=jax.ShapeDtypeStruct((rows, W), F32),
        scratch_shapes=[pltpu.VMEM((LANES, LANES), F32)],
        compiler_params=_params(("arbitrary", "arbitrary", "arbitrary")),
        name="moba_prompt",
    )(q, qf, k, v, kmean.reshape(batch, nq, W), zs)


def _mlstm_kernel(qk_ref, v_ref, gb_ref, gt_ref, gtt_ref, conv0_ref, c0_ref, m0_ref,
                  cw_ref, cb_ref, brow_ref, bcol_ref,
                  y_ref, conv_out_ref, c_out_ref, m_out_ref,
                  xpad_ref, caug_ref, m_ref, *, L):
    c = pl.program_id(1)
    nc = pl.num_programs(1)
    H, DK = ML_HEADS, ML_HEAD_DIM
    W = HALF_WIDTH
    PADR = SUBLANES

    @pl.when(c == 0)
    def _():
        xpad_ref[0:PADR, :] = conv0_ref[0]
        caug_ref[...] = c0_ref[0]
        m_ref[...] = m0_ref[0]

    xpad_ref[PADR:PADR + L, :] = qk_ref[...]
    y = cb_ref[...]
    for j in range(CONV_WIDTH):
        off = PADR - (CONV_WIDTH - 1) + j
        y = y + cw_ref[j:j + 1, :] * xpad_ref[off:off + L, :]
    tail = xpad_ref[L:L + PADR, :]
    xpad_ref[0:PADR, :] = tail
    conv_out_ref[0] = tail
    qk = _silu(y)

    lane = _iota((L, LANES), 1)
    is_f_col = jnp.logical_and(lane >= H, lane < 2 * H)
    gcol = gt_ref[0] + brow_ref[...]
    lf_col = jnp.where(is_f_col, _log_sigmoid(gcol), 0.0)
    grow = gtt_ref[0] + bcol_ref[...]
    sub = _iota((2 * H, L), 0)
    lf_row = jnp.where(sub >= H, _log_sigmoid(grow), 0.0)
    row = _iota((L, L), 0)
    col = _iota((L, L), 1)
    causal = col <= row
    tri = jnp.where(causal, 1.0, 0.0)
    f_col = _dot(tri, lf_col, HIGHEST)
    f_row = _dot_nt(lf_row, tri, HIGHEST)
    ones_col = jnp.where(_iota((L, DK), 1) == 0, 1.0, 0.0).astype(BF16)
    m_all = m_ref[...]
    m_next = m_all
    lane1 = _iota((1, LANES), 1)

    for h in range(H):
        fc = f_col[:, H + h:H + h + 1]
        fr = f_row[H + h:H + h + 1, :]
        li_c = gcol[:, h:h + 1]
        li_r = grow[h:h + 1, :]
        m_prev = m_all[:, h:h + 1]
        inter = fc + m_prev
        intra = jnp.where(causal, fc - fr + li_r, NEG_INF)
        mt = jnp.maximum(inter, jnp.max(intra, axis=1, keepdims=True))
        w = jnp.exp(intra - mt)
        g = jnp.exp(inter - mt)
        qh = qk[:, h * DK:(h + 1) * DK].astype(BF16)
        kf = qk[:, W + h * DK:W + (h + 1) * DK] * (DK ** -0.5)
        kh = kf.astype(BF16)
        vaug = jnp.concatenate([v_ref[:, h * DK:(h + 1) * DK].astype(BF16), ones_col], axis=1)
        s = _dot_nt(qh, kh) * w
        nd = g * _dot(qh, caug_ref[h].astype(BF16)) + _dot(s.astype(BF16), vaug)
        den = nd[:, DK:DK + 1]
        hout = nd[:, 0:DK] / jnp.maximum(jnp.abs(den), jnp.exp(-mt))
        y_ref[:, h * DK:(h + 1) * DK] = gb_ref[:, h * DK:(h + 1) * DK] * hout
        m_new = mt[L - 1:L, :]
        f_last = fc[L - 1:L, :]
        decay = jnp.exp(f_last + m_prev - m_new)
        ws = jnp.exp(f_last - fc + li_c - m_new)
        caug_ref[h] = decay * caug_ref[h] + _dot_tn((kf * ws).astype(BF16), vaug)
        m_next = jnp.where(lane1 == h, m_new, m_next)

    m_ref[...] = m_next

    @pl.when(c == nc - 1)
    def _():
        c_out_ref[0] = caug_ref[...]
        m_out_ref[0] = m_next


def _mlstm(qk_pre, vb, gb, gates, gates_t, conv0, c0aug, m0, conv_w, conv_b, b_ig, b_fg, batch, L):
    rows, W2 = qk_pre.shape
    W = HALF_WIDTH
    H = ML_HEADS
    nc = rows // batch // L
    brow = jnp.pad(jnp.concatenate([b_ig, b_fg]).reshape(1, 2 * H), ((0, 0), (0, LANES - 2 * H)))
    bcol = jnp.concatenate([b_ig, b_fg]).reshape(2 * H, 1)
    out_shapes = [jax.ShapeDtypeStruct((rows, W), F32),
                  jax.ShapeDtypeStruct((batch, SUBLANES, W2), F32),
                  jax.ShapeDtypeStruct((batch, H, ML_HEAD_DIM, 2 * ML_HEAD_DIM), F32),
                  jax.ShapeDtypeStruct((batch, 1, LANES), F32)]
    return pl.pallas_call(
        functools.partial(_mlstm_kernel, L=L),
        grid=(batch, nc),
        in_specs=[pl.BlockSpec((L, W2), lambda b, c: (b * nc + c, 0)),
                  pl.BlockSpec((L, W), lambda b, c: (b * nc + c, 0)),
                  pl.BlockSpec((L, W), lambda b, c: (b * nc + c, 0)),
                  pl.BlockSpec((1, L, LANES), lambda b, c: (b * nc + c, 0, 0)),
                  pl.BlockSpec((1, 2 * H, L), lambda b, c: (b * nc + c, 0, 0)),
                  pl.BlockSpec((1, SUBLANES, W2), lambda b, c: (b, 0, 0)),
                  pl.BlockSpec((1, H, ML_HEAD_DIM, 2 * ML_HEAD_DIM), lambda b, c: (b, 0, 0, 0)),
                  pl.BlockSpec((1, 1, LANES), lambda b, c: (b, 0, 0)),
                  pl.BlockSpec((CONV_WIDTH, W2), lambda b, c: (0, 0)),
                  pl.BlockSpec((1, W2), lambda b, c: (0, 0)),
                  pl.BlockSpec((1, LANES), lambda b, c: (0, 0)),
                  pl.BlockSpec((2 * H, 1), lambda b, c: (0, 0))],
        out_specs=[pl.BlockSpec((L, W), lambda b, c: (b * nc + c, 0)),
                   pl.BlockSpec((1, SUBLANES, W2), lambda b, c: (b, 0, 0)),
                   pl.BlockSpec((1, H, ML_HEAD_DIM, 2 * ML_HEAD_DIM), lambda b, c: (b, 0, 0, 0)),
                   pl.BlockSpec((1, 1, LANES), lambda b, c: (b, 0, 0))],
        out_shape=out_shapes,
        scratch_shapes=[pltpu.VMEM((L + SUBLANES, W2), F32),
                        pltpu.VMEM((H, ML_HEAD_DIM, 2 * ML_HEAD_DIM), F32),
                        pltpu.VMEM((1, LANES), F32)],
        compiler_params=_params(("arbitrary", "arbitrary")),
        name="mlstm",
    )(qk_pre, vb, gb, gates, gates_t, conv0, c0aug, m0, conv_w, conv_b.reshape(1, W2), brow, bcol)


def _rglru_gates(xc, wr_ref, br_ref, wi_ref, bi_ref, lam_ref):
    xb = xc.astype(BF16)
    r = _sigmoid(_dot(xb, wr_ref[...]) + br_ref[...])
    i = _sigmoid(_dot(xb, wi_ref[...]) + bi_ref[...])
    log_a = RG_C * r * _log_sigmoid(lam_ref[...])
    a = jnp.exp(log_a)
    b = jnp.sqrt(-jnp.tanh(log_a) * (a * a + 1.0)) * (i * xc)
    return a, b


def _rglru_prompt_kernel(xd_ref, zs_ref, conv0_ref, h0_ref, cw_ref, cb_ref, wr_ref, br_ref, wi_ref, bi_ref,
                         lam_ref, y_ref, conv_out_ref, h_out_ref,
                         xpad_ref, a_ref, b_ref, hs_ref, hc_ref, *, L):
    c = pl.program_id(1)
    PADR = SUBLANES

    @pl.when(c == 0)
    def _():
        xpad_ref[0:PADR, :] = conv0_ref[0]
        hc_ref[...] = h0_ref[0]

    xpad_ref[PADR:PADR + L, :] = xd_ref[...]
    xc = cb_ref[...]
    for j in range(CONV_WIDTH):
        off = PADR - (CONV_WIDTH - 1) + j
        xc = xc + cw_ref[j:j + 1, :] * xpad_ref[off:off + L, :]
    tail = xpad_ref[L:L + PADR, :]
    xpad_ref[0:PADR, :] = tail
    conv_out_ref[0] = tail

    a, b = _rglru_gates(xc, wr_ref, br_ref, wi_ref, bi_ref, lam_ref)
    a_ref[...] = a
    b_ref[...] = b

    def step(t, h):
        h = a_ref[pl.ds(t, 1), :] * h + b_ref[pl.ds(t, 1), :]
        hs_ref[pl.ds(t, 1), :] = h
        return h

    h = lax.fori_loop(0, L, step, hc_ref[...], unroll=8)
    hc_ref[...] = h
    h_out_ref[0] = h
    y_ref[...] = hs_ref[...] * zs_ref[...]


def _block_diag(w):
    g, n, _ = w.shape
    eye = jnp.eye(g, dtype=w.dtype)
    return (eye[:, None, :, None] * w[:, :, None, :]).reshape(g * n, g * n)


def _rglru_prompt(xd, zs, conv0, h0, conv_w, conv_b, wr, br, wi, bi, lam, batch, L):
    rows, W = xd.shape
    nc = rows // batch // L
    vec = lambda a: a.reshape(1, W)
    const = lambda shape: pl.BlockSpec(shape, lambda b, c: tuple(0 for _ in shape))
    return pl.pallas_call(
        functools.partial(_rglru_prompt_kernel, L=L),
        grid=(batch, nc),
        in_specs=[pl.BlockSpec((L, W), lambda b, c: (b * nc + c, 0)),
                  pl.BlockSpec((L, W), lambda b, c: (b * nc + c, 0)),
                  pl.BlockSpec((1, SUBLANES, W), lambda b, c: (b, 0, 0)),
                  pl.BlockSpec((1, 1, W), lambda b, c: (b, 0, 0)),
                  const((CONV_WIDTH, W)), const((1, W)), const((W, W)), const((1, W)), const((W, W)),
                  const((1, W)), const((1, W))],
        out_specs=[pl.BlockSpec((L, W), lambda b, c: (b * nc + c, 0)),
                   pl.BlockSpec((1, SUBLANES, W), lambda b, c: (b, 0, 0)),
                   pl.BlockSpec((1, 1, W), lambda b, c: (b, 0, 0))],
        out_shape=[jax.ShapeDtypeStruct((rows, W), F32),
                   jax.ShapeDtypeStruct((batch, SUBLANES, W), F32),
                   jax.ShapeDtypeStruct((batch, 1, W), F32)],
        scratch_shapes=[pltpu.VMEM((L + SUBLANES, W), F32), pltpu.VMEM((L, W), F32), pltpu.VMEM((L, W), F32),
                        pltpu.VMEM((L, W), F32), pltpu.VMEM((1, W), F32)],
        compiler_params=_params(("arbitrary", "arbitrary")),
        name="rglru_prompt",
    )(xd, zs, conv0, h0, conv_w, vec(conv_b), _block_diag(wr).astype(BF16), vec(br),
      _block_diag(wi).astype(BF16), vec(bi), vec(lam))


def _rglru_sample_kernel(xd_ref, zs_ref, conv0_ref, h0_ref, cw_ref, cb_ref, wr_ref, br_ref, wi_ref, bi_ref,
                         lam_ref, y_ref, h_out_ref, *, T):
    xs = [conv0_ref[j] for j in range(CONV_WIDTH - 1)] + [xd_ref[t] for t in range(T)]
    h = h0_ref[...]
    for t in range(T):
        xc = cb_ref[...]
        for j in range(CONV_WIDTH):
            xc = xc + cw_ref[j:j + 1, :] * xs[t + j]
        a, b = _rglru_gates(xc, wr_ref, br_ref, wi_ref, bi_ref, lam_ref)
        h = a * h + b
        y_ref[t] = h * zs_ref[t]
    h_out_ref[...] = h


def _rglru_sample(xd, zs, conv0, h0, conv_w, conv_b, wr, br, wi, bi, lam):
    T, B, W = xd.shape
    vec = lambda a: a.reshape(1, W)
    return pl.pallas_call(
        functools.partial(_rglru_sample_kernel, T=T),
        out_shape=[jax.ShapeDtypeStruct((T, B, W), F32), jax.ShapeDtypeStruct((B, W), F32)],
        compiler_params=pltpu.CompilerParams(vmem_limit_bytes=VMEM_LIMIT),
        name="rglru_sample",
    )(xd, zs, conv0, h0, conv_w, vec(conv_b), _block_diag(wr).astype(BF16), vec(br),
      _block_diag(wi).astype(BF16), vec(bi), vec(lam))


def _query_columns(q, scale):
    b, t, w = q.shape
    heads = w // HEAD_DIM
    head_of_chan = jnp.arange(w) // HEAD_DIM
    onehot = (head_of_chan[:, None] == jnp.arange(heads)[None, :]).astype(q.dtype)
    cols = (q * scale)[:, :, :, None] * onehot[None, None, :, :]
    cols = jnp.transpose(cols, (0, 2, 1, 3)).reshape(b, w, t * heads)
    return jnp.pad(cols, ((0, 0), (0, 0), (0, LANES - t * heads)))


def _page_specs(pages_per_step, page_rows, width, page_of):
    def spec(i):
        return pl.BlockSpec((None, page_rows, width), lambda b, c, pt: (pt[b, page_of(c, i)], 0, 0))
    return [spec(i) for i in range(pages_per_step)]


def _head_diagonal(o, t_new, zs):
    heads = HALF_WIDTH // HEAD_DIM
    rows = t_new * heads
    keep = (_iota((rows, HALF_WIDTH), 1) // HEAD_DIM) == (_iota((rows, HALF_WIDTH), 0) % heads)
    o = jnp.where(keep, o, 0.0)
    return jnp.sum(o.reshape(t_new, heads, HALF_WIDTH), axis=1) * zs


def _sb_sample_kernel(pt_ref, wq_ref, knew_ref, vnew_ref, zs_ref, *rest, P, t_new):
    kpages, vpages = rest[:P], rest[P:2 * P]
    o_ref, acc_ref, car_ref = rest[2 * P:]
    c = pl.program_id(1)
    nch = pl.num_programs(1)
    wq = wq_ref[0]
    heads = HALF_WIDTH // HEAD_DIM

    def sub_block(kf, vf, mask):
        n = kf.shape[0]
        z = _dot(kf.astype(BF16), wq)
        ls = -_softplus(z)
        if mask is not None:
            ls = jnp.where(mask, ls, 0.0)
        suffix = jnp.where(_iota((n, n), 1) >= _iota((n, n), 0), 1.0, 0.0).astype(BF16)
        hi, lo = _split_bf16(ls)
        cum = _dot(suffix, hi) + _dot(suffix, lo) + car_ref[...]
        w = jnp.exp(z + cum)
        if mask is not None:
            w = jnp.where(mask, w, 0.0)
        acc_ref[...] += _dot_tn(w.astype(BF16), vf.astype(BF16))
        car_ref[...] = cum[0:1, :]

    @pl.when(c == 0)
    def _():
        acc_ref[...] = jnp.zeros_like(acc_ref)
        car_ref[...] = jnp.zeros_like(car_ref)
        n = knew_ref.shape[1]
        j = _iota((n, LANES), 0)
        t = _iota((n, LANES), 1) // heads
        sub_block(knew_ref[0], vnew_ref[0], jnp.logical_and(j < t, j < t_new))

    for s in reversed(range(P // 2)):
        kf = jnp.concatenate([kpages[2 * s][...], kpages[2 * s + 1][...]], axis=0)
        vf = jnp.concatenate([vpages[2 * s][...], vpages[2 * s + 1][...]], axis=0)
        sub_block(kf, vf, None)

    @pl.when(c == nch - 1)
    def _():
        o_ref[0] = _head_diagonal(acc_ref[0:t_new * heads, :], t_new, zs_ref[0])


def _sb_sample(page_table, wq, knew, vnew, zs, k_pool, v_pool, P):
    B, n_pages = page_table.shape
    t_new = zs.shape[1]
    W = HALF_WIDTH
    page_rows = k_pool.shape[1]
    nch = n_pages // P
    page_of = lambda c, i: (nch - 1 - c) * P + i
    per_b = lambda shape: pl.BlockSpec((1,) + shape, lambda b, c, pt: (b,) + tuple(0 for _ in shape))
    grid_spec = pltpu.PrefetchScalarGridSpec(
        num_scalar_prefetch=1,
        grid=(B, nch),
        in_specs=[per_b((W, LANES)), per_b(knew.shape[1:]), per_b(vnew.shape[1:]), per_b((t_new, W))]
        + _page_specs(P, page_rows, W, page_of) + _page_specs(P, page_rows, W, page_of),
        out_specs=per_b((t_new, W)),
        scratch_shapes=[pltpu.VMEM((LANES, W), F32), pltpu.VMEM((1, LANES), F32)],
    )
    return pl.pallas_call(
        functools.partial(_sb_sample_kernel, P=P, t_new=t_new),
        grid_spec=grid_spec,
        out_shape=jax.ShapeDtypeStruct((B, t_new, W), F32),
        compiler_params=_params(("arbitrary", "arbitrary")),
        name="sb_sample",
    )(page_table, wq, knew, vnew, zs, *([k_pool] * P), *([v_pool] * P))


def _moba_sample_kernel(pt_ref, wq_ref, qg_ref, knew_ref, vnew_ref, zs_ref, *rest, P, t_new, nb):
    kpages, vpages = rest[:P], rest[P:2 * P]
    o_ref, m_ref, l_ref, acc_ref, km_ref = rest[2 * P:]
    c = pl.program_id(1)
    nch = pl.num_programs(1)
    wq = wq_ref[0]
    heads = HALF_WIDTH // HEAD_DIM
    ncol = t_new * heads
    nbp = m_ref.shape[0]

    @pl.when(c == 0)
    def _():
        km_ref[...] = jnp.zeros_like(km_ref)
        m_ref[...] = jnp.zeros_like(m_ref)
        l_ref[...] = jnp.zeros_like(l_ref)

    for s in range(P // 2):
        n = c * (P // 2) + s
        kf = jnp.concatenate([kpages[2 * s][...], kpages[2 * s + 1][...]], axis=0)
        vf = jnp.concatenate([vpages[2 * s][...], vpages[2 * s + 1][...]], axis=0)
        km_ref[pl.ds(n, 1), :] = jnp.mean(kf, axis=0, keepdims=True)
        z = _dot(kf.astype(BF16), wq)
        m = jnp.max(z, axis=0, keepdims=True)
        p = jnp.exp(z - m)
        m_ref[pl.ds(n, 1), :] = m
        l_ref[pl.ds(n, 1), :] = jnp.sum(p, axis=0, keepdims=True)
        acc_ref[n] = _dot_tn(p.astype(BF16), vf.astype(BF16))[0:ncol, :]

    @pl.when(c == nch - 1)
    def _():
        blk = _iota((nbp, LANES), 0)
        blk_f = blk.astype(F32)
        g = _dot_nt(km_ref[...], qg_ref[0], HIGHEST)
        g = jnp.where(blk < nb, g, NEG_INF)
        sel = jnp.zeros((nbp, LANES), F32)
        for _ in range(min(MOBA_TOPK, nb)):
            mx = jnp.max(g, axis=0, keepdims=True)
            is_max = jnp.logical_and(g == mx, mx > NEG_INF)
            idx = jnp.min(jnp.where(is_max, blk_f, float(nbp)), axis=0, keepdims=True)
            pick = blk_f == idx
            sel = jnp.where(pick, 1.0, sel)
            g = jnp.where(pick, NEG_INF, g)
        nn = knew_ref.shape[1]
        j = _iota((nn, LANES), 0)
        t = _iota((nn, LANES), 1) // heads
        zn = jnp.where(jnp.logical_and(j <= t, j < t_new), _dot(knew_ref[0].astype(BF16), wq), NEG_INF)
        m_tot = jnp.maximum(jnp.max(jnp.where(sel > 0.0, m_ref[...], NEG_INF), axis=0, keepdims=True),
                            jnp.max(zn, axis=0, keepdims=True))
        coef = jnp.where(sel > 0.0, jnp.exp(m_ref[...] - m_tot), 0.0)
        p_own = jnp.exp(zn - m_tot)
        denom = jnp.sum(coef * l_ref[...], axis=0, keepdims=True) + jnp.sum(p_own, axis=0, keepdims=True)
        o = _dot_tn(p_own.astype(BF16), vnew_ref[0].astype(BF16))[0:ncol, :]
        coef_t = jnp.transpose(jnp.concatenate([coef, jnp.broadcast_to(denom, (SUBLANES, LANES))], axis=0))
        for n in range(nb):
            o = o + coef_t[0:ncol, n:n + 1] * acc_ref[n]
        o = o / coef_t[0:ncol, nbp:nbp + 1]
        o_ref[0] = _head_diagonal(o, t_new, zs_ref[0])


def _moba_sample(page_table, wq, qg, knew, vnew, zs, k_pool, v_pool, P):
    B, n_pages = page_table.shape
    t_new = zs.shape[1]
    W = HALF_WIDTH
    heads = W // HEAD_DIM
    page_rows = k_pool.shape[1]
    nch = n_pages // P
    nb = n_pages * page_rows // MOBA_BLOCK
    nbp = -(-nb // SUBLANES) * SUBLANES
    page_of = lambda c, i: c * P + i
    per_b = lambda shape: pl.BlockSpec((1,) + shape, lambda b, c, pt: (b,) + tuple(0 for _ in shape))
    grid_spec = pltpu.PrefetchScalarGridSpec(
        num_scalar_prefetch=1,
        grid=(B, nch),
        in_specs=[per_b((W, LANES)), per_b((LANES, W)), per_b(knew.shape[1:]), per_b(vnew.shape[1:]),
                  per_b((t_new, W))]
        + _page_specs(P, page_rows, W, page_of) + _page_specs(P, page_rows, W, page_of),
        out_specs=per_b((t_new, W)),
        scratch_shapes=[pltpu.VMEM((nbp, LANES), F32), pltpu.VMEM((nbp, LANES), F32),
                        pltpu.VMEM((nb, t_new * heads, W), F32), pltpu.VMEM((nbp, W), F32)],
    )
    return pl.pallas_call(
        functools.partial(_moba_sample_kernel, P=P, t_new=t_new, nb=nb),
        grid_spec=grid_spec,
        out_shape=jax.ShapeDtypeStruct((B, t_new, W), F32),
        compiler_params=_params(("arbitrary", "arbitrary")),
        name="moba_sample",
    )(page_table, wq, qg, knew, vnew, zs, *([k_pool] * P), *([v_pool] * P))


def _pad_rows(a, rows):
    return jnp.pad(a, ((0, 0), (0, rows - a.shape[1]), (0, 0)))


def kernel(x_prompt, x_sample, c_prompt, c_sample, page_table, cache_k_sb, cache_v_sb, state_conv_mlstm, state_c_mlstm, state_n_mlstm, state_m_mlstm, cache_k_moba, cache_v_moba, state_conv_rglru, state_h_rglru, norm_g_even, mod_w_even, mod_b_even, w_in_even, conv_w_even, conv_b_even, b_igate_even, b_fgate_even, w_out_even, norm_g_odd, mod_w_odd, mod_b_odd, w_in_odd, conv_w_odd, conv_b_odd, w_rgate_odd, b_rgate_odd, w_igate_odd, b_igate_odd, lru_lambda_odd, w_out_odd, final_g):
    B, T, D = x_prompt.shape
    DB, DT, _ = x_sample.shape
    W = HALF_WIDTH
    H = ML_HEADS
    heads = W // HEAD_DIM
    n_pool, page_rows = cache_k_sb.shape[1], cache_k_sb.shape[2]
    past = page_table.shape[1] * page_rows
    depth = norm_g_even.shape[0] + norm_g_odd.shape[0]
    TM = 256
    PAGES_PER_STEP = 8
    NEW_ROWS = LANES
    ML_PAD = SUBLANES

    xp = x_prompt.reshape(B * T, D)
    xs = x_sample.reshape(DB * DT, D)
    c_all = jnp.concatenate([c_prompt, c_sample], axis=0)
    c_rows = -(-c_all.shape[0] // SUBLANES) * SUBLANES
    c_all = jnp.pad(c_all, ((0, c_rows - c_all.shape[0]), (0, 0)))
    cos_p, sin_p = _rope_tables(jnp.arange(T))
    cos_s, sin_s = _rope_tables(past + jnp.arange(DT))
    cos_s, sin_s = jnp.tile(cos_s, (DB, 1)), jnp.tile(sin_s, (DB, 1))

    def mods(w, b):
        mod = _modulation(c_all, w, b)
        parts = []
        for part in jnp.split(mod, 3, axis=-1):
            p_part = part[:B].reshape(B, 1, D)
            s_part = jnp.repeat(part[B:B + DB], DT, axis=0).reshape(1, DB * DT, D)
            parts.append((p_part, s_part))
        return parts

    p_even, s_even, p_odd, s_odd = [], [], [], []
    y_prompt = y_sample = None
    for l in range(depth):
        j = l // 2
        last = l == depth - 1
        if l % 2 == 0:
            (sh_p, sh_s), (sc_p, sc_s), (ga_p, ga_s) = mods(mod_w_even[j], mod_b_even[j])
            qa, ka, kab, va, vab, za, qk, vb, gb, gt, gtt = _in_proj_even(
                xp, sh_p, sc_p, norm_g_even[j], w_in_even[j], TM)
            ya = _sb_prompt(qa, kab, vab, za, B, T, TM)
            yb, conv_p, caug_p, m_p = _mlstm(
                qk, vb, gb, gt, gtt,
                jnp.zeros((B, SUBLANES, 2 * W), F32), jnp.zeros((B, H, ML_HEAD_DIM, 2 * ML_HEAD_DIM), F32),
                jnp.zeros((B, 1, LANES), F32),
                conv_w_even[j], conv_b_even[j], b_igate_even[j], b_fgate_even[j], B, TM)
            xp = _out_proj(ya, yb, xp, ga_p, w_out_even[j], final_g, TM, last)
            p_even.append((ka.reshape(B, T, heads, HEAD_DIM), va.reshape(B, T, heads, HEAD_DIM),
                           conv_p[:, SUBLANES - (CONV_WIDTH - 1):], caug_p[..., :ML_HEAD_DIM],
                           caug_p[..., ML_HEAD_DIM], m_p[:, 0, :H]))
            qa, ka, kab, va, vab, za, qk, vb, gb, gt, gtt = _in_proj_even(
                xs, sh_s, sc_s, norm_g_even[j], w_in_even[j], DB * DT)
            wq = _query_columns(qa.reshape(DB, DT, W), 1.0)
            ya = _sb_sample(page_table, wq, _pad_rows(ka.reshape(DB, DT, W), NEW_ROWS),
                            _pad_rows(va.reshape(DB, DT, W), NEW_ROWS), za.reshape(DB, DT, W),
                            cache_k_sb[j].reshape(n_pool, page_rows, W), cache_v_sb[j].reshape(n_pool, page_rows, W),
                            PAGES_PER_STEP).reshape(DB * DT, W)
            pad_t = lambda a: _pad_rows(a.reshape(DB, DT, a.shape[-1]), ML_PAD).reshape(DB * ML_PAD, a.shape[-1])
            g_rows = gt.reshape(DB, DT, LANES)[:, :, :2 * H]
            pad_gate = jnp.concatenate([jnp.full((H,), NEG_INF, F32), jnp.full((H,), jnp.inf, F32)])
            g_rows = jnp.concatenate([g_rows, jnp.broadcast_to(pad_gate, (DB, ML_PAD - DT, 2 * H))], axis=1)
            gt_s = jnp.pad(g_rows, ((0, 0), (0, 0), (0, LANES - 2 * H)))
            gtt_s = jnp.transpose(g_rows, (0, 2, 1))
            conv0 = jnp.pad(state_conv_mlstm[j], ((0, 0), (SUBLANES - (CONV_WIDTH - 1), 0), (0, 0)))
            c0aug = jnp.concatenate(
                [state_c_mlstm[j].astype(F32), state_n_mlstm[j].astype(F32)[..., None],
                 jnp.zeros((DB, H, ML_HEAD_DIM, ML_HEAD_DIM - 1), F32)], axis=-1)
            m0 = jnp.pad(state_m_mlstm[j].astype(F32), ((0, 0), (0, LANES - H))).reshape(DB, 1, LANES)
            yb, _, caug_s, m_s = _mlstm(
                pad_t(qk), pad_t(vb), pad_t(gb), gt_s, gtt_s, conv0, c0aug, m0,
                conv_w_even[j], conv_b_even[j], b_igate_even[j], b_fgate_even[j], DB, ML_PAD)
            yb = yb.reshape(DB, ML_PAD, W)[:, :DT].reshape(DB * DT, W)
            conv_s = jnp.concatenate([state_conv_mlstm[j].astype(F32), qk.reshape(DB, DT, 2 * W)],
                                     axis=1)[:, -(CONV_WIDTH - 1):]
            xs = _out_proj(ya, yb, xs, ga_s, w_out_even[j], final_g, DB * DT, last)
            s_even.append((ka.reshape(DB, DT, heads, HEAD_DIM), va.reshape(DB, DT, heads, HEAD_DIM),
                           conv_s, caug_s[..., :ML_HEAD_DIM], caug_s[..., ML_HEAD_DIM], m_s[:, 0, :H]))
        else:
            (sh_p, sh_s), (sc_p, sc_s), (ga_p, ga_s) = mods(mod_w_odd[j], mod_b_odd[j])
            lru = (conv_w_odd[j], conv_b_odd[j], w_rgate_odd[j], b_rgate_odd[j], w_igate_odd[j], b_igate_odd[j],
                   lru_lambda_odd[j])
            qb, qf, kc, kcb, vc, vcb, zc, xd, zd, km = _in_proj_odd(
                xp, sh_p, sc_p, norm_g_odd[j], w_in_odd[j], cos_p, sin_p, TM)
            yc = _moba_prompt(qb, qf, kcb, vcb, km, zc, B, T)
            yd, conv_p, h_p = _rglru_prompt(xd, zd, jnp.zeros((B, SUBLANES, W), F32), jnp.zeros((B, 1, W), F32),
                                            *lru, B, TM)
            xp = _out_proj(yc, yd, xp, ga_p, w_out_odd[j], final_g, TM, last)
            p_odd.append((kc.reshape(B, T, heads, HEAD_DIM), vc.reshape(B, T, heads, HEAD_DIM),
                          conv_p[:, SUBLANES - (CONV_WIDTH - 1):], h_p[:, 0]))
            qb, qf, kc, kcb, vc, vcb, zc, xd, zd, km = _in_proj_odd(
                xs, sh_s, sc_s, norm_g_odd[j], w_in_odd[j], cos_s, sin_s, DB * DT)
            wq = _query_columns(qb.reshape(DB, DT, W), 1.0)
            qg = jnp.transpose(_query_columns(qf.reshape(DB, DT, W), 1.0), (0, 2, 1))
            yc = _moba_sample(page_table, wq, qg, _pad_rows(kc.reshape(DB, DT, W), NEW_ROWS),
                              _pad_rows(vc.reshape(DB, DT, W), NEW_ROWS), zc.reshape(DB, DT, W),
                              cache_k_moba[j].reshape(n_pool, page_rows, W),
                              cache_v_moba[j].reshape(n_pool, page_rows, W),
                              PAGES_PER_STEP).reshape(DB * DT, W)
            tmaj = lambda a: jnp.transpose(a.reshape(DB, DT, W), (1, 0, 2))
            yd, h_s = _rglru_sample(tmaj(xd), tmaj(zd), jnp.transpose(state_conv_rglru[j].astype(F32), (1, 0, 2)),
                                    state_h_rglru[j].astype(F32), *lru)
            yd = jnp.transpose(yd, (1, 0, 2)).reshape(DB * DT, W)
            conv_s = jnp.concatenate([state_conv_rglru[j].astype(F32), xd.reshape(DB, DT, W)],
                                     axis=1)[:, -(CONV_WIDTH - 1):]
            xs = _out_proj(yc, yd, xs, ga_s, w_out_odd[j], final_g, DB * DT, last)
            s_odd.append((kc.reshape(DB, DT, heads, HEAD_DIM), vc.reshape(DB, DT, heads, HEAD_DIM), conv_s, h_s))

    y_prompt = xp.reshape(B, T, D)
    y_sample = xs.reshape(DB, DT, D)
    stack = lambda group: [jnp.stack(a) for a in zip(*group)]
    pk_sb, pv_sb, pconv_m, pc_m, pn_m, pm_m = stack(p_even)
    sk_sb, sv_sb, sconv_m, sc_m, sn_m, sm_m = stack(s_even)
    pk_mb, pv_mb, pconv_d, ph_d = stack(p_odd)
    sk_mb, sv_mb, sconv_d, sh_d = stack(s_odd)
    return (y_prompt, y_sample,
            pk_sb, pv_sb, pconv_m, pc_m, pn_m, pm_m, pk_mb, pv_mb, pconv_d, ph_d,
            sk_sb, sv_sb, sconv_m, sc_m, sn_m, sm_m, sk_mb, sv_mb, sconv_d, sh_d)
```

```python
import functools

import jax
import jax.numpy as jnp
from jax import lax
from jax.experimental import pallas as pl
from jax.experimental.pallas import tpu as pltpu

F32 = jnp.float32
BF16 = jnp.bfloat16
HIGHEST = lax.Precision.HIGHEST

HEAD_DIM = 64
HALF_WIDTH = 512
ML_HEADS = 4
ML_HEAD_DIM = 128
CONV_WIDTH = 4
MOBA_BLOCK = 256
MOBA_TOPK = 3
RG_C = 8.0
ROPE_THETA = 10000.0
NORM_EPS = 1e-6
LANES = 128
SUBLANES = 8
VMEM_LIMIT = 56 * 1024 * 1024
NEG_INF = float("-inf")
SB_EXP_UNDERFLOW = -110.0
SB_BOUND_SLACK = 1.01
MOBA_MASK_PENALTY = -1e30


def _params(sem):
    return pltpu.CompilerParams(dimension_semantics=sem, vmem_limit_bytes=VMEM_LIMIT)


def _sigmoid(x):
    return 1.0 / (1.0 + jnp.exp(-x))


def _silu(x):
    return x * _sigmoid(x)


def _softplus(x):
    return jnp.maximum(x, 0.0) + jnp.log1p(jnp.exp(-jnp.abs(x)))


def _log_sigmoid(x):
    return -_softplus(-x)


def _dot(a, b, precision=None):
    return jnp.dot(a, b, preferred_element_type=F32, precision=precision)


def _dot_nt(a, b, precision=None):
    return lax.dot_general(a, b, (((1,), (1,)), ((), ())), preferred_element_type=F32, precision=precision)


def _dot_tn(a, b, precision=None):
    return lax.dot_general(a, b, (((0,), (0,)), ((), ())), preferred_element_type=F32, precision=precision)


def _split_bf16(x):
    hi = x.astype(BF16)
    lo = (x - hi.astype(F32)).astype(BF16)
    return hi, lo


def _iota(shape, dim):
    return lax.broadcasted_iota(jnp.int32, shape, dim)


def _mod_kernel(c_ref, w_ref, b_ref, o_ref):
    o_ref[...] = _dot(_silu(c_ref[...]), w_ref[...], HIGHEST) + b_ref[...]


def _modulation(c, w, b):
    rows, d = c.shape
    n = w.shape[1]
    tn = 512
    return pl.pallas_call(
        _mod_kernel,
        grid=(n // tn,),
        in_specs=[pl.BlockSpec((rows, d), lambda j: (0, 0)),
                  pl.BlockSpec((d, tn), lambda j: (0, j)),
                  pl.BlockSpec((1, tn), lambda j: (0, j))],
        out_specs=pl.BlockSpec((rows, tn), lambda j: (0, j)),
        out_shape=jax.ShapeDtypeStruct((rows, n), F32),
        compiler_params=_params(("arbitrary",)),
        name="modulation",
    )(c, w, b.reshape(1, n))


def _normed(x_ref, shift_ref, scale_ref, g_ref):
    x = x_ref[...]
    ms = jnp.mean(x * x, axis=-1, keepdims=True)
    h = x * lax.rsqrt(ms + NORM_EPS) * g_ref[...]
    return h * (1.0 + scale_ref[0]) + shift_ref[0]


def _in_even_kernel(x_ref, shift_ref, scale_ref, g_ref, w_ref, wg_ref, wgt_ref,
                    qa_ref, ka_ref, kab_ref, va_ref, vab_ref, za_ref, qk_ref, vb_ref, gb_ref,
                    gt_ref, gtt_ref):
    h = _normed(x_ref, shift_ref, scale_ref, g_ref)
    hb = h.astype(BF16)
    W = HALF_WIDTH

    def proj(c):
        return _dot(hb, w_ref[:, c * W:(c + 1) * W])

    qa_ref[...] = (proj(0) * (HEAD_DIM ** -0.5)).astype(BF16)
    ka = proj(1)
    ka_ref[...] = ka
    kab_ref[...] = ka.astype(BF16)
    va = proj(2)
    va_ref[...] = va
    vab_ref[...] = va.astype(BF16)
    za_ref[...] = _silu(proj(3))
    qk_ref[:, 0:W] = proj(4)
    qk_ref[:, W:2 * W] = proj(5)
    vb_ref[...] = proj(6)
    gb_ref[...] = _sigmoid(proj(7)) * _silu(proj(8))
    gt_ref[0] = _dot(h, wg_ref[...], HIGHEST)
    gtt_ref[0] = _dot_nt(wgt_ref[...], h, HIGHEST)


def _in_proj_even(x, shift, scale, g, w_in, tm):
    rows, d = x.shape
    W = HALF_WIDTH
    nt = rows // tm
    groups = shift.shape[0]
    tiles_per_group = nt // groups
    w_main = w_in[:, :9 * W].astype(BF16)
    w_gate = jnp.pad(w_in[:, 9 * W:], ((0, 0), (0, LANES - 2 * ML_HEADS)))
    w_gate_t = w_in[:, 9 * W:].T
    row_blk = lambda n, dt: (pl.BlockSpec((tm, n), lambda i: (i, 0)), jax.ShapeDtypeStruct((rows, n), dt))
    outs = [row_blk(W, BF16), row_blk(W, F32), row_blk(W, BF16), row_blk(W, F32), row_blk(W, BF16),
            row_blk(W, F32), row_blk(2 * W, F32), row_blk(W, F32), row_blk(W, F32),
            (pl.BlockSpec((1, tm, LANES), lambda i: (i, 0, 0)), jax.ShapeDtypeStruct((nt, tm, LANES), F32)),
            (pl.BlockSpec((1, 2 * ML_HEADS, tm), lambda i: (i, 0, 0)),
             jax.ShapeDtypeStruct((nt, 2 * ML_HEADS, tm), F32))]
    rg = shift.shape[1]
    return pl.pallas_call(
        _in_even_kernel,
        grid=(nt,),
        in_specs=[pl.BlockSpec((tm, d), lambda i: (i, 0)),
                  pl.BlockSpec((1, rg, d), lambda i: (i // tiles_per_group, 0, 0)),
                  pl.BlockSpec((1, rg, d), lambda i: (i // tiles_per_group, 0, 0)),
                  pl.BlockSpec((1, d), lambda i: (0, 0)),
                  pl.BlockSpec((d, 9 * W), lambda i: (0, 0)),
                  pl.BlockSpec((d, LANES), lambda i: (0, 0)),
                  pl.BlockSpec((2 * ML_HEADS, d), lambda i: (0, 0))],
        out_specs=[o[0] for o in outs],
        out_shape=[o[1] for o in outs],
        compiler_params=_params(("arbitrary",)),
        name="in_proj_even",
    )(x, shift, scale, g.reshape(1, d), w_main, w_gate, w_gate_t)


def _in_odd_kernel(x_ref, shift_ref, scale_ref, g_ref, w_ref, cos_ref, sin_ref,
                   qb_ref, qf_ref, kc_ref, kcb_ref, vc_ref, vcb_ref, zc_ref, xd_ref, zd_ref, km_ref):
    h = _normed(x_ref, shift_ref, scale_ref, g_ref)
    hb = h.astype(BF16)
    W = HALF_WIDTH

    def proj(c):
        return _dot(hb, w_ref[:, c * W:(c + 1) * W])

    cos = cos_ref[...]
    sin = sin_ref[...]
    first_half = (_iota(cos.shape, 1) % HEAD_DIM) < (HEAD_DIM // 2)

    def rope(x):
        partner = jnp.where(first_half, pltpu.roll(x, W - HEAD_DIM // 2, 1), pltpu.roll(x, HEAD_DIM // 2, 1))
        return x * cos + partner * sin

    q = rope(proj(0))
    qf_ref[...] = q
    qb_ref[...] = (q * (HEAD_DIM ** -0.5)).astype(BF16)
    k = rope(proj(1))
    kc_ref[...] = k
    kcb_ref[...] = k.astype(BF16)
    km_ref[0] = jnp.mean(k, axis=0, keepdims=True)
    v = proj(2)
    vc_ref[...] = v
    vcb_ref[...] = v.astype(BF16)
    zc_ref[...] = _silu(proj(3))
    xd_ref[...] = proj(4)
    zd_ref[...] = _silu(proj(5))


def _in_proj_odd(x, shift, scale, g, w_in, cos, sin, tm):
    rows, d = x.shape
    W = HALF_WIDTH
    nt = rows // tm
    groups = shift.shape[0]
    tiles_per_group = nt // groups
    rg = shift.shape[1]
    row_blk = lambda n, dt: (pl.BlockSpec((tm, n), lambda i: (i, 0)), jax.ShapeDtypeStruct((rows, n), dt))
    outs = [row_blk(W, BF16), row_blk(W, F32), row_blk(W, F32), row_blk(W, BF16), row_blk(W, F32),
            row_blk(W, BF16), row_blk(W, F32), row_blk(W, F32), row_blk(W, F32),
            (pl.BlockSpec((1, 1, W), lambda i: (i, 0, 0)), jax.ShapeDtypeStruct((nt, 1, W), F32))]
    return pl.pallas_call(
        _in_odd_kernel,
        grid=(nt,),
        in_specs=[pl.BlockSpec((tm, d), lambda i: (i, 0)),
                  pl.BlockSpec((1, rg, d), lambda i: (i // tiles_per_group, 0, 0)),
                  pl.BlockSpec((1, rg, d), lambda i: (i // tiles_per_group, 0, 0)),
                  pl.BlockSpec((1, d), lambda i: (0, 0)),
                  pl.BlockSpec((d, 6 * W), lambda i: (0, 0)),
                  pl.BlockSpec((tm, W), lambda i: (i % tiles_per_group, 0)),
                  pl.BlockSpec((tm, W), lambda i: (i % tiles_per_group, 0))],
        out_specs=[o[0] for o in outs],
        out_shape=[o[1] for o in outs],
        compiler_params=_params(("arbitrary",)),
        name="in_proj_odd",
    )(x, shift, scale, g.reshape(1, d), w_in.astype(BF16), cos, sin)


def _rope_tables(positions):
    half = HEAD_DIM // 2
    freqs = ROPE_THETA ** (-jnp.arange(half, dtype=F32) / half)
    ang = positions.astype(F32)[:, None] * freqs[None, :]
    cos = jnp.cos(ang)
    sin = jnp.sin(ang)
    heads = HALF_WIDTH // HEAD_DIM
    cos_t = jnp.tile(jnp.concatenate([cos, cos], axis=-1), (1, heads))
    sin_t = jnp.tile(jnp.concatenate([-sin, sin], axis=-1), (1, heads))
    return cos_t, sin_t


def _out_proj_kernel(ya_ref, yb_ref, x_ref, gate_ref, w_ref, fg_ref, o_ref, *, final_norm):
    W = HALF_WIDTH
    out = _dot(ya_ref[...].astype(BF16), w_ref[0:W, :]) + _dot(yb_ref[...].astype(BF16), w_ref[W:2 * W, :])
    y = x_ref[...] + gate_ref[0] * out
    if final_norm:
        ms = jnp.mean(y * y, axis=-1, keepdims=True)
        y = y * lax.rsqrt(ms + NORM_EPS) * fg_ref[...]
    o_ref[...] = y


def _out_proj(ya, yb, x, gate, w_out, final_g, tm, final_norm):
    rows, d = x.shape
    W = HALF_WIDTH
    nt = rows // tm
    groups = gate.shape[0]
    tiles_per_group = nt // groups
    rg = gate.shape[1]
    return pl.pallas_call(
        functools.partial(_out_proj_kernel, final_norm=final_norm),
        grid=(nt,),
        in_specs=[pl.BlockSpec((tm, W), lambda i: (i, 0)),
                  pl.BlockSpec((tm, W), lambda i: (i, 0)),
                  pl.BlockSpec((tm, d), lambda i: (i, 0)),
                  pl.BlockSpec((1, rg, d), lambda i: (i // tiles_per_group, 0, 0)),
                  pl.BlockSpec((2 * W, d), lambda i: (0, 0)),
                  pl.BlockSpec((1, d), lambda i: (0, 0))],
        out_specs=pl.BlockSpec((tm, d), lambda i: (i, 0)),
        out_shape=jax.ShapeDtypeStruct((rows, d), F32),
        compiler_params=_params(("arbitrary",)),
        name="out_proj",
    )(ya, yb, x, gate, w_out.astype(BF16), final_g.reshape(1, d))


def _sb_prompt_kernel(q_ref, k_ref, v_ref, zs_ref, o_ref, knorm_ref, *, tq):
    qi = pl.program_id(2)
    q = q_ref[...]
    lane = _iota((tq, LANES), 1)
    head0 = lane < HEAD_DIM
    qs = (jnp.where(head0, q, jnp.zeros_like(q)), jnp.where(head0, jnp.zeros_like(q), q))

    def head_sq_norms(x):
        sq = x.astype(F32) * x.astype(F32)
        return (jnp.sum(jnp.where(head0, sq, 0.0), axis=1, keepdims=True),
                jnp.sum(jnp.where(head0, 0.0, sq), axis=1, keepdims=True))

    @pl.when(qi == 0)
    def _():
        def scan(j, m):
            n0, n1 = head_sq_norms(k_ref[pl.ds(pl.multiple_of(j * tq, tq), tq), :])
            return (jnp.maximum(m[0], jnp.max(n0, axis=0, keepdims=True)),
                    jnp.maximum(m[1], jnp.max(n1, axis=0, keepdims=True)))
        zero = jnp.zeros((1, 1), F32)
        m0, m1 = lax.fori_loop(0, k_ref.shape[0] // tq, scan, (zero, zero))
        knorm_ref[...] = jnp.where(_iota((1, LANES), 1) == 0, m0, m1)

    qn0, qn1 = head_sq_norms(q)
    zb0 = jnp.sqrt(qn0 * knorm_ref[0:1, 0:1]) * SB_BOUND_SLACK
    zb1 = jnp.sqrt(qn1 * knorm_ref[0:1, 1:2]) * SB_BOUND_SLACK
    row = _iota((tq, tq), 0)
    col = _iota((tq, tq), 1)
    suffix = jnp.where(row >= col, 1.0, 0.0).astype(BF16)
    causal = col < row

    def block(j, carry, diag):
        acc, c0, c1 = carry
        start = pl.multiple_of(j * tq, tq)
        kb = k_ref[pl.ds(start, tq), :]
        vb = v_ref[pl.ds(start, tq), :]
        pvs, cs = [], []
        for qh, ch in zip(qs, (c0, c1)):
            z = _dot_nt(qh, kb)
            ls = -_softplus(z)
            if diag:
                ls = jnp.where(causal, ls, 0.0)
            hi, lo = _split_bf16(ls)
            cum = _dot(hi, suffix) + _dot(lo, suffix) + ch
            w = jnp.exp(z + cum)
            if diag:
                w = jnp.where(causal, w, 0.0)
            pvs.append(_dot(w.astype(BF16), vb))
            cs.append(cum[:, 0:1])
        return acc + jnp.where(head0, pvs[0], pvs[1]), cs[0], cs[1]

    def exhausted(c0, c1):
        worst = jnp.max(jnp.maximum(zb0 + c0, zb1 + c1))
        return (worst < SB_EXP_UNDERFLOW).astype(jnp.int32)

    def step(state):
        j, _, acc, c0, c1 = state
        acc, c0, c1 = block(j, (acc, c0, c1), False)
        return j - 1, exhausted(c0, c1), acc, c0, c1

    zero_c = jnp.zeros((tq, 1), F32)
    acc, c0, c1 = block(qi, (jnp.zeros((tq, LANES), F32), zero_c, zero_c), True)
    state = lax.while_loop(lambda s: jnp.logical_and(s[0] >= 0, s[1] == 0), step,
                           (qi - 1, exhausted(c0, c1), acc, c0, c1))
    o_ref[...] = state[2] * zs_ref[...]


def _sb_prompt(q, k, v, zs, batch, seq, tq):
    rows, W = q.shape
    nq = seq // tq
    pairs = W // LANES
    return pl.pallas_call(
        functools.partial(_sb_prompt_kernel, tq=tq),
        grid=(batch, pairs, nq),
        in_specs=[pl.BlockSpec((tq, LANES), lambda b, p, i: (b * nq + i, p)),
                  pl.BlockSpec((seq, LANES), lambda b, p, i: (b, p)),
                  pl.BlockSpec((seq, LANES), lambda b, p, i: (b, p)),
                  pl.BlockSpec((tq, LANES), lambda b, p, i: (b * nq + i, p))],
        out_specs=pl.BlockSpec((tq, LANES), lambda b, p, i: (b * nq + i, p)),
        out_shape=jax.ShapeDtypeStruct((rows, W), F32),
        scratch_shapes=[pltpu.VMEM((1, LANES), F32)],
        compiler_params=_params(("arbitrary", "arbitrary", "arbitrary")),
        name="sb_prompt",
    )(q, k, v, zs)


def _moba_prompt_kernel(q_ref, qf_ref, k_ref, v_ref, km_ref, zs_ref, o_ref, kpad_ref, *, tq, nb):
    qi = pl.program_id(2)

    @pl.when(qi == 0)
    def _():
        kpad_ref[...] = jnp.zeros_like(kpad_ref)
        kpad_ref[0:nb, :] = km_ref[0]

    q = q_ref[...]
    qf = qf_ref[...]
    lane = _iota((tq, LANES), 1)
    lane_f = lane.astype(F32)
    head0 = lane < HEAD_DIM
    row = _iota((tq, tq), 0)
    col = _iota((tq, tq), 1)
    kmean = kpad_ref[...]

    qs, sels = [], []
    for h in range(2):
        hm = head0 if h == 0 else jnp.logical_not(head0)
        qs.append(jnp.where(hm, q, jnp.zeros_like(q)))
        g = _dot_nt(jnp.where(hm, qf, 0.0), kmean, HIGHEST)
        g = jnp.where(lane < qi, g, NEG_INF)
        sel = jnp.zeros((tq, LANES), F32)
        for _ in range(MOBA_TOPK):
            mx = jnp.max(g, axis=1, keepdims=True)
            is_max = jnp.logical_and(g == mx, mx > NEG_INF)
            idx = jnp.min(jnp.where(is_max, lane_f, float(LANES)), axis=1, keepdims=True)
            pick = lane_f == idx
            sel = jnp.where(pick, 1.0, sel)
            g = jnp.where(pick, NEG_INF, g)
        sels.append(sel)

    lhs = [jnp.concatenate([qs[h], jnp.where(sels[h] > 0.0, 0.0, 1.0).astype(BF16)], axis=1) for h in range(2)]
    ones = jnp.ones((tq, LANES), BF16)

    def values(vb):
        return jnp.where(head0, vb, ones), jnp.where(head0, ones, vb)

    start = pl.multiple_of(qi * tq, tq)
    kb = k_ref[pl.ds(start, tq), :]
    vmods = values(v_ref[pl.ds(start, tq), :])
    state = []
    for h in range(2):
        s = jnp.where(col <= row, _dot_nt(qs[h], kb), NEG_INF)
        m = jnp.max(s, axis=1, keepdims=True)
        p = jnp.exp(s - m)
        state += [m, _dot(p.astype(BF16), vmods[h])]

    def blocks(ns, st):
        ks, vs = [], []
        for n in ns:
            start_n = pl.multiple_of(n * tq, tq)
            penalty = jnp.where(lane == n, MOBA_MASK_PENALTY, 0.0).astype(BF16)
            ks.append(jnp.concatenate([k_ref[pl.ds(start_n, tq), :], penalty], axis=1))
            vs.append(values(v_ref[pl.ds(start_n, tq), :]))
        new = []
        for h in range(2):
            m, acc = st[2 * h:2 * h + 2]
            ss = [_dot_nt(lhs[h], kn) for kn in ks]
            m_new = m
            for s in ss:
                m_new = jnp.maximum(m_new, jnp.max(s, axis=1, keepdims=True))
            acc = jnp.exp(m - m_new) * acc
            for s, vn in zip(ss, vs):
                acc = acc + _dot(jnp.exp(s - m_new).astype(BF16), vn[h])
            new += [m_new, acc]
        return tuple(new)

    st = tuple(state)
    st = lax.cond(qi % 2 == 1, lambda s: blocks([0], s), lambda s: s, st)
    first = qi % 2
    st = lax.fori_loop(0, qi // 2, lambda i, s: blocks([first + 2 * i, first + 2 * i + 1], s), st)
    o = jnp.where(head0, st[1] / pltpu.roll(st[1], HEAD_DIM, 1), st[3] / pltpu.roll(st[3], HEAD_DIM, 1))
    o_ref[...] = o * zs_ref[...]


def _moba_prompt(q, qf, k, v, kmean, zs, batch, seq):
    rows, W = q.shape
    tq = MOBA_BLOCK
    nq = seq // tq
    pairs = W // LANES
    return pl.pallas_call(
        functools.partial(_moba_prompt_kernel, tq=tq, nb=nq),
        grid=(batch, pairs, nq),
        in_specs=[pl.BlockSpec((tq, LANES), lambda b, p, i: (b * nq + i, p)),
                  pl.BlockSpec((tq, LANES), lambda b, p, i: (b * nq + i, p)),
                  pl.BlockSpec((seq, LANES), lambda b, p, i: (b, p)),
                  pl.BlockSpec((seq, LANES), lambda b, p, i: (b, p)),
                  pl.BlockSpec((1, nq, LANES), lambda b, p, i: (b, 0, p)),
                  pl.BlockSpec((tq, LANES), lambda b, p, i: (b * nq + i, p))],
        out_specs=pl.BlockSpec((tq, LANES), lambda b, p, i: (b * nq + i, p)),
        out_shape=jax.ShapeDtypeStruct((rows, W), F32),
        scratch_shapes=[pltpu.VMEM((LANES, LANES), F32)],
        compiler_params=_params(("arbitrary", "arbitrary", "arbitrary")),
        name="moba_prompt",
    )(q, qf, k, v, kmean.reshape(batch, nq, W), zs)


def _mlstm_kernel(qk_ref, v_ref, gb_ref, gt_ref, gtt_ref, conv0_ref, c0_ref, m0_ref,
                  cw_ref, cb_ref, brow_ref, bcol_ref,
                  y_ref, conv_out_ref, c_out_ref, m_out_ref,
                  xpad_ref, caug_ref, m_ref, *, L):
    c = pl.program_id(1)
    nc = pl.num_programs(1)
    H, DK = ML_HEADS, ML_HEAD_DIM
    W = HALF_WIDTH
    PADR = SUBLANES

    @pl.when(c == 0)
    def _():
        xpad_ref[0:PADR, :] = conv0_ref[0]
        caug_ref[...] = c0_ref[0]
        m_ref[...] = m0_ref[0]

    xpad_ref[PADR:PADR + L, :] = qk_ref[...]
    y = cb_ref[...]
    for j in range(CONV_WIDTH):
        off = PADR - (CONV_WIDTH - 1) + j
        y = y + cw_ref[j:j + 1, :] * xpad_ref[off:off + L, :]
    tail = xpad_ref[L:L + PADR, :]
    xpad_ref[0:PADR, :] = tail
    conv_out_ref[0] = tail
    qk = _silu(y)

    lane = _iota((L, LANES), 1)
    is_f_col = jnp.logical_and(lane >= H, lane < 2 * H)
    gcol = gt_ref[0] + brow_ref[...]
    lf_col = jnp.where(is_f_col, _log_sigmoid(gcol), 0.0)
    grow = gtt_ref[0] + bcol_ref[...]
    sub = _iota((2 * H, L), 0)
    lf_row = jnp.where(sub >= H, _log_sigmoid(grow), 0.0)
    row = _iota((L, L), 0)
    col = _iota((L, L), 1)
    causal = col <= row
    tri = jnp.where(causal, 1.0, 0.0)
    f_col = _dot(tri, lf_col, HIGHEST)
    f_row = _dot_nt(lf_row, tri, HIGHEST)
    ones_col = jnp.where(_iota((L, DK), 1) == 0, 1.0, 0.0).astype(BF16)
    m_all = m_ref[...]
    m_next = m_all
    lane1 = _iota((1, LANES), 1)

    for h in range(H):
        fc = f_col[:, H + h:H + h + 1]
        fr = f_row[H + h:H + h + 1, :]
        li_c = gcol[:, h:h + 1]
        li_r = grow[h:h + 1, :]
        m_prev = m_all[:, h:h + 1]
        inter = fc + m_prev
        intra = jnp.where(causal, fc - fr + li_r, NEG_INF)
        mt = jnp.maximum(inter, jnp.max(intra, axis=1, keepdims=True))
        w = jnp.exp(intra - mt)
        g = jnp.exp(inter - mt)
        qh = qk[:, h * DK:(h + 1) * DK].astype(BF16)
        kf = qk[:, W + h * DK:W + (h + 1) * DK] * (DK ** -0.5)
        kh = kf.astype(BF16)
        vaug = jnp.concatenate([v_ref[:, h * DK:(h + 1) * DK].astype(BF16), ones_col], axis=1)
        s = _dot_nt(qh, kh) * w
        nd = g * _dot(qh, caug_ref[h].astype(BF16)) + _dot(s.astype(BF16), vaug)
        den = nd[:, DK:DK + 1]
        hout = nd[:, 0:DK] / jnp.maximum(jnp.abs(den), jnp.exp(-mt))
        y_ref[:, h * DK:(h + 1) * DK] = gb_ref[:, h * DK:(h + 1) * DK] * hout
        m_new = mt[L - 1:L, :]
        f_last = fc[L - 1:L, :]
        decay = jnp.exp(f_last + m_prev - m_new)
        ws = jnp.exp(f_last - fc + li_c - m_new)
        caug_ref[h] = decay * caug_ref[h] + _dot_tn((kf * ws).astype(BF16), vaug)
        m_next = jnp.where(lane1 == h, m_new, m_next)

    m_ref[...] = m_next

    @pl.when(c == nc - 1)
    def _():
        c_out_ref[0] = caug_ref[...]
        m_out_ref[0] = m_next


def _mlstm(qk_pre, vb, gb, gates, gates_t, conv0, c0aug, m0, conv_w, conv_b, b_ig, b_fg, batch, L):
    rows, W2 = qk_pre.shape
    W = HALF_WIDTH
    H = ML_HEADS
    nc = rows // batch // L
    brow = jnp.pad(jnp.concatenate([b_ig, b_fg]).reshape(1, 2 * H), ((0, 0), (0, LANES - 2 * H)))
    bcol = jnp.concatenate([b_ig, b_fg]).reshape(2 * H, 1)
    out_shapes = [jax.ShapeDtypeStruct((rows, W), F32),
                  jax.ShapeDtypeStruct((batch, SUBLANES, W2), F32),
                  jax.ShapeDtypeStruct((batch, H, ML_HEAD_DIM, 2 * ML_HEAD_DIM), F32),
                  jax.ShapeDtypeStruct((batch, 1, LANES), F32)]
    return pl.pallas_call(
        functools.partial(_mlstm_kernel, L=L),
        grid=(batch, nc),
        in_specs=[pl.BlockSpec((L, W2), lambda b, c: (b * nc + c, 0)),
                  pl.BlockSpec((L, W), lambda b, c: (b * nc + c, 0)),
                  pl.BlockSpec((L, W), lambda b, c: (b * nc + c, 0)),
                  pl.BlockSpec((1, L, LANES), lambda b, c: (b * nc + c, 0, 0)),
                  pl.BlockSpec((1, 2 * H, L), lambda b, c: (b * nc + c, 0, 0)),
                  pl.BlockSpec((1, SUBLANES, W2), lambda b, c: (b, 0, 0)),
                  pl.BlockSpec((1, H, ML_HEAD_DIM, 2 * ML_HEAD_DIM), lambda b, c: (b, 0, 0, 0)),
                  pl.BlockSpec((1, 1, LANES), lambda b, c: (b, 0, 0)),
                  pl.BlockSpec((CONV_WIDTH, W2), lambda b, c: (0, 0)),
                  pl.BlockSpec((1, W2), lambda b, c: (0, 0)),
                  pl.BlockSpec((1, LANES), lambda b, c: (0, 0)),
                  pl.BlockSpec((2 * H, 1), lambda b, c: (0, 0))],
        out_specs=[pl.BlockSpec((L, W), lambda b, c: (b * nc + c, 0)),
                   pl.BlockSpec((1, SUBLANES, W2), lambda b, c: (b, 0, 0)),
                   pl.BlockSpec((1, H, ML_HEAD_DIM, 2 * ML_HEAD_DIM), lambda b, c: (b, 0, 0, 0)),
                   pl.BlockSpec((1, 1, LANES), lambda b, c: (b, 0, 0))],
        out_shape=out_shapes,
        scratch_shapes=[pltpu.VMEM((L + SUBLANES, W2), F32),
                        pltpu.VMEM((H, ML_HEAD_DIM, 2 * ML_HEAD_DIM), F32),
                        pltpu.VMEM((1, LANES), F32)],
        compiler_params=_params(("arbitrary", "arbitrary")),
        name="mlstm",
    )(qk_pre, vb, gb, gates, gates_t, conv0, c0aug, m0, conv_w, conv_b.reshape(1, W2), brow, bcol)


def _rglru_gates(xc, wr_ref, br_ref, wi_ref, bi_ref, lam_ref):
    xb = xc.astype(BF16)
    r = _sigmoid(_dot(xb, wr_ref[...]) + br_ref[...])
    i = _sigmoid(_dot(xb, wi_ref[...]) + bi_ref[...])
    log_a = RG_C * r * _log_sigmoid(lam_ref[...])
    a = jnp.exp(log_a)
    b = jnp.sqrt(-jnp.tanh(log_a) * (a * a + 1.0)) * (i * xc)
    return a, b


def _rglru_prompt_kernel(xd_ref, zs_ref, conv0_ref, h0_ref, cw_ref, cb_ref, wr_ref, br_ref, wi_ref, bi_ref,
                         lam_ref, y_ref, conv_out_ref, h_out_ref,
                         xpad_ref, a_ref, b_ref, hs_ref, hc_ref, *, L):
    c = pl.program_id(1)
    PADR = SUBLANES

    @pl.when(c == 0)
    def _():
        xpad_ref[0:PADR, :] = conv0_ref[0]
        hc_ref[...] = h0_ref[0]

    xpad_ref[PADR:PADR + L, :] = xd_ref[...]
    xc = cb_ref[...]
    for j in range(CONV_WIDTH):
        off = PADR - (CONV_WIDTH - 1) + j
        xc = xc + cw_ref[j:j + 1, :] * xpad_ref[off:off + L, :]
    tail = xpad_ref[L:L + PADR, :]
    xpad_ref[0:PADR, :] = tail
    conv_out_ref[0] = tail

    a, b = _rglru_gates(xc, wr_ref, br_ref, wi_ref, bi_ref, lam_ref)
    a_ref[...] = a
    b_ref[...] = b

    def step(t, h):
        h = a_ref[pl.ds(t, 1), :] * h + b_ref[pl.ds(t, 1), :]
        hs_ref[pl.ds(t, 1), :] = h
        return h

    h = lax.fori_loop(0, L, step, hc_ref[...], unroll=8)
    hc_ref[...] = h
    h_out_ref[0] = h
    y_ref[...] = hs_ref[...] * zs_ref[...]


def _block_diag(w):
    g, n, _ = w.shape
    eye = jnp.eye(g, dtype=w.dtype)
    return (eye[:, None, :, None] * w[:, :, None, :]).reshape(g * n, g * n)


def _rglru_prompt(xd, zs, conv0, h0, conv_w, conv_b, wr, br, wi, bi, lam, batch, L):
    rows, W = xd.shape
    nc = rows // batch // L
    vec = lambda a: a.reshape(1, W)
    const = lambda shape: pl.BlockSpec(shape, lambda b, c: tuple(0 for _ in shape))
    return pl.pallas_call(
        functools.partial(_rglru_prompt_kernel, L=L),
        grid=(batch, nc),
        in_specs=[pl.BlockSpec((L, W), lambda b, c: (b * nc + c, 0)),
                  pl.BlockSpec((L, W), lambda b, c: (b * nc + c, 0)),
                  pl.BlockSpec((1, SUBLANES, W), lambda b, c: (b, 0, 0)),
                  pl.BlockSpec((1, 1, W), lambda b, c: (b, 0, 0)),
                  const((CONV_WIDTH, W)), const((1, W)), const((W, W)), const((1, W)), const((W, W)),
                  const((1, W)), const((1, W))],
        out_specs=[pl.BlockSpec((L, W), lambda b, c: (b * nc + c, 0)),
                   pl.BlockSpec((1, SUBLANES, W), lambda b, c: (b, 0, 0)),
                   pl.BlockSpec((1, 1, W), lambda b, c: (b, 0, 0))],
        out_shape=[jax.ShapeDtypeStruct((rows, W), F32),
                   jax.ShapeDtypeStruct((batch, SUBLANES, W), F32),
                   jax.ShapeDtypeStruct((batch, 1, W), F32)],
        scratch_shapes=[pltpu.VMEM((L + SUBLANES, W), F32), pltpu.VMEM((L, W), F32), pltpu.VMEM((L, W), F32),
                        pltpu.VMEM((L, W), F32), pltpu.VMEM((1, W), F32)],
        compiler_params=_params(("arbitrary", "arbitrary")),
        name="rglru_prompt",
    )(xd, zs, conv0, h0, conv_w, vec(conv_b), _block_diag(wr).astype(BF16), vec(br),
      _block_diag(wi).astype(BF16), vec(bi), vec(lam))


def _rglru_sample_kernel(xd_ref, zs_ref, conv0_ref, h0_ref, cw_ref, cb_ref, wr_ref, br_ref, wi_ref, bi_ref,
                         lam_ref, y_ref, h_out_ref, *, T):
    xs = [conv0_ref[j] for j in range(CONV_WIDTH - 1)] + [xd_ref[t] for t in range(T)]
    h = h0_ref[...]
    for t in range(T):
        xc = cb_ref[...]
        for j in range(CONV_WIDTH):
            xc = xc + cw_ref[j:j + 1, :] * xs[t + j]
        a, b = _rglru_gates(xc, wr_ref, br_ref, wi_ref, bi_ref, lam_ref)
        h = a * h + b
        y_ref[t] = h * zs_ref[t]
    h_out_ref[...] = h


def _rglru_sample(xd, zs, conv0, h0, conv_w, conv_b, wr, br, wi, bi, lam):
    T, B, W = xd.shape
    vec = lambda a: a.reshape(1, W)
    return pl.pallas_call(
        functools.partial(_rglru_sample_kernel, T=T),
        out_shape=[jax.ShapeDtypeStruct((T, B, W), F32), jax.ShapeDtypeStruct((B, W), F32)],
        compiler_params=pltpu.CompilerParams(vmem_limit_bytes=VMEM_LIMIT),
        name="rglru_sample",
    )(xd, zs, conv0, h0, conv_w, vec(conv_b), _block_diag(wr).astype(BF16), vec(br),
      _block_diag(wi).astype(BF16), vec(bi), vec(lam))


def _query_rows(q):
    b, t, w = q.shape
    heads = w // HEAD_DIM
    onehot = (jnp.arange(heads)[:, None] == (jnp.arange(w) // HEAD_DIM)[None, :]).astype(q.dtype)
    return (q[:, :, None, :] * onehot[None, None]).reshape(b, t * heads, w)


def _channel_major(a, lanes):
    return jnp.pad(jnp.transpose(a, (0, 2, 1)), ((0, 0), (0, 0), (0, lanes - a.shape[1])))


def _pool_channel_major(pool):
    n_pool, page_rows, heads, hd = pool.shape
    return jnp.transpose(pool, (0, 2, 3, 1)).reshape(n_pool, heads * hd, page_rows)


def _page_specs(pages_per_step, page_rows, width, page_of):
    def spec(i):
        return pl.BlockSpec((None, width, page_rows), lambda b, c, pt: (pt[b, page_of(c, i)], 0, 0))
    return [spec(i) for i in range(pages_per_step)]


def _head_diagonal(o, t_new, zs):
    heads = HALF_WIDTH // HEAD_DIM
    rows = t_new * heads
    keep = (_iota((rows, HALF_WIDTH), 1) // HEAD_DIM) == (_iota((rows, HALF_WIDTH), 0) % heads)
    o = jnp.where(keep, o, 0.0)
    return jnp.sum(o.reshape(t_new, heads, HALF_WIDTH), axis=1) * zs


def _sb_sample_kernel(pt_ref, q_ref, knew_ref, vnew_ref, zs_ref, *rest, P, t_new):
    kpages, vpages = rest[:P], rest[P:2 * P]
    o_ref, acc_ref, car_ref = rest[2 * P:]
    c = pl.program_id(1)
    nch = pl.num_programs(1)
    q = q_ref[0]
    heads = HALF_WIDTH // HEAD_DIM
    rows = t_new * heads

    def key_block(kt, vt, mask):
        n = kt.shape[1]
        z = _dot(q, kt.astype(BF16))
        ls = -_softplus(z)
        if mask is not None:
            ls = jnp.where(mask, ls, 0.0)
        suffix = jnp.where(_iota((n, n), 0) >= _iota((n, n), 1), 1.0, 0.0).astype(BF16)
        hi, lo = _split_bf16(ls)
        cum = _dot(hi, suffix) + _dot(lo, suffix) + car_ref[...]
        w = jnp.exp(z + cum)
        if mask is not None:
            w = jnp.where(mask, w, 0.0)
        acc_ref[...] += _dot_nt(w.astype(BF16), vt.astype(BF16))
        car_ref[...] = cum[:, 0:1]

    @pl.when(c == 0)
    def _():
        acc_ref[...] = jnp.zeros_like(acc_ref)
        car_ref[...] = jnp.zeros_like(car_ref)
        n = knew_ref.shape[2]
        j = _iota((rows, n), 1)
        t = _iota((rows, n), 0) // heads
        key_block(knew_ref[0], vnew_ref[0], jnp.logical_and(j < t, j < t_new))

    for i in reversed(range(P)):
        key_block(kpages[i][...], vpages[i][...], None)

    @pl.when(c == nch - 1)
    def _():
        o_ref[0] = _head_diagonal(acc_ref[...], t_new, zs_ref[0])


def _sb_sample(page_table, q_rows, knew_t, vnew_t, zs, k_pool, v_pool, P):
    B, n_pages = page_table.shape
    t_new = zs.shape[1]
    W = HALF_WIDTH
    rows = q_rows.shape[1]
    page_rows = k_pool.shape[2]
    nch = n_pages // P
    page_of = lambda c, i: (nch - 1 - c) * P + i
    per_b = lambda shape: pl.BlockSpec((1,) + shape, lambda b, c, pt: (b,) + tuple(0 for _ in shape))
    grid_spec = pltpu.PrefetchScalarGridSpec(
        num_scalar_prefetch=1,
        grid=(B, nch),
        in_specs=[per_b((rows, W)), per_b(knew_t.shape[1:]), per_b(vnew_t.shape[1:]), per_b((t_new, W))]
        + _page_specs(P, page_rows, W, page_of) + _page_specs(P, page_rows, W, page_of),
        out_specs=per_b((t_new, W)),
        scratch_shapes=[pltpu.VMEM((rows, W), F32), pltpu.VMEM((rows, 1), F32)],
    )
    return pl.pallas_call(
        functools.partial(_sb_sample_kernel, P=P, t_new=t_new),
        grid_spec=grid_spec,
        out_shape=jax.ShapeDtypeStruct((B, t_new, W), F32),
        compiler_params=_params(("arbitrary", "arbitrary")),
        name="sb_sample",
    )(page_table, q_rows, knew_t, vnew_t, zs, *([k_pool] * P), *([v_pool] * P))


def _moba_sample_kernel(pt_ref, q_ref, qg_ref, knew_ref, vnew_ref, zs_ref, *rest, P, t_new, nb, pb):
    kpages, vpages = rest[:P], rest[P:2 * P]
    o_ref, m_ref, l_ref, acc_ref, km_ref = rest[2 * P:]
    c = pl.program_id(1)
    nch = pl.num_programs(1)
    q = q_ref[0]
    heads = HALF_WIDTH // HEAD_DIM
    rows = t_new * heads
    lane = _iota((rows, LANES), 1)

    @pl.when(c == 0)
    def _():
        km_ref[...] = jnp.zeros_like(km_ref)
        m_ref[...] = jnp.zeros_like(m_ref)
        l_ref[...] = jnp.zeros_like(l_ref)

    for s in range(P // pb):
        n = c * (P // pb) + s
        kts = [kpages[pb * s + r][...] for r in range(pb)]
        ksum = kts[0]
        for kt in kts[1:]:
            ksum = ksum + kt
        kmean = jnp.sum(ksum, axis=1, keepdims=True) * (1.0 / MOBA_BLOCK)
        km_ref[...] = jnp.where(_iota(km_ref.shape, 1) == n, kmean, km_ref[...])
        zs = [_dot(q, kt.astype(BF16)) for kt in kts]
        m = jnp.max(zs[0], axis=1, keepdims=True)
        for z in zs[1:]:
            m = jnp.maximum(m, jnp.max(z, axis=1, keepdims=True))
        l = jnp.zeros((rows, 1), F32)
        a = jnp.zeros((rows, HALF_WIDTH), F32)
        for r, z in enumerate(zs):
            p = jnp.exp(z - m)
            l = l + jnp.sum(p, axis=1, keepdims=True)
            a = a + _dot_nt(p.astype(BF16), vpages[pb * s + r][...].astype(BF16))
        m_ref[...] = jnp.where(lane == n, m, m_ref[...])
        l_ref[...] = jnp.where(lane == n, l, l_ref[...])
        acc_ref[n] = a

    @pl.when(c == nch - 1)
    def _():
        lane_f = lane.astype(F32)
        g = _dot(qg_ref[0], km_ref[...], HIGHEST)
        g = jnp.where(lane < nb, g, NEG_INF)
        sel = jnp.zeros((rows, LANES), F32)
        for _ in range(min(MOBA_TOPK, nb)):
            mx = jnp.max(g, axis=1, keepdims=True)
            is_max = jnp.logical_and(g == mx, mx > NEG_INF)
            idx = jnp.min(jnp.where(is_max, lane_f, float(LANES)), axis=1, keepdims=True)
            pick = lane_f == idx
            sel = jnp.where(pick, 1.0, sel)
            g = jnp.where(pick, NEG_INF, g)
        nn = knew_ref.shape[2]
        j = _iota((rows, nn), 1)
        t = _iota((rows, nn), 0) // heads
        zn = jnp.where(jnp.logical_and(j <= t, j < t_new), _dot(q, knew_ref[0].astype(BF16)), NEG_INF)
        m_all = m_ref[...]
        m_tot = jnp.maximum(jnp.max(jnp.where(sel > 0.0, m_all, NEG_INF), axis=1, keepdims=True),
                            jnp.max(zn, axis=1, keepdims=True))
        coef = jnp.where(sel > 0.0, jnp.exp(m_all - m_tot), 0.0)
        p_own = jnp.exp(zn - m_tot)
        denom = jnp.sum(coef * l_ref[...], axis=1, keepdims=True) + jnp.sum(p_own, axis=1, keepdims=True)
        o = _dot_nt(p_own.astype(BF16), vnew_ref[0].astype(BF16))
        for n in range(nb):
            o = o + coef[:, n:n + 1] * acc_ref[n]
        o_ref[0] = _head_diagonal(o / denom, t_new, zs_ref[0])


def _moba_sample(page_table, q_rows, qg_rows, knew_t, vnew_t, zs, k_pool, v_pool, P):
    B, n_pages = page_table.shape
    t_new = zs.shape[1]
    W = HALF_WIDTH
    rows = q_rows.shape[1]
    page_rows = k_pool.shape[2]
    pb = MOBA_BLOCK // page_rows
    nch = n_pages // P
    nb = n_pages // pb
    assert nb <= LANES and P % pb == 0
    page_of = lambda c, i: c * P + i
    per_b = lambda shape: pl.BlockSpec((1,) + shape, lambda b, c, pt: (b,) + tuple(0 for _ in shape))
    grid_spec = pltpu.PrefetchScalarGridSpec(
        num_scalar_prefetch=1,
        grid=(B, nch),
        in_specs=[per_b((rows, W)), per_b((rows, W)), per_b(knew_t.shape[1:]), per_b(vnew_t.shape[1:]),
                  per_b((t_new, W))]
        + _page_specs(P, page_rows, W, page_of) + _page_specs(P, page_rows, W, page_of),
        out_specs=per_b((t_new, W)),
        scratch_shapes=[pltpu.VMEM((rows, LANES), F32), pltpu.VMEM((rows, LANES), F32),
                        pltpu.VMEM((nb, rows, W), F32), pltpu.VMEM((W, LANES), F32)],
    )
    return pl.pallas_call(
        functools.partial(_moba_sample_kernel, P=P, t_new=t_new, nb=nb, pb=pb),
        grid_spec=grid_spec,
        out_shape=jax.ShapeDtypeStruct((B, t_new, W), F32),
        compiler_params=_params(("arbitrary", "arbitrary")),
        name="moba_sample",
    )(page_table, q_rows, qg_rows, knew_t, vnew_t, zs, *([k_pool] * P), *([v_pool] * P))


def _pad_rows(a, rows):
    return jnp.pad(a, ((0, 0), (0, rows - a.shape[1]), (0, 0)))


def kernel(x_prompt, x_sample, c_prompt, c_sample, page_table, cache_k_sb, cache_v_sb, state_conv_mlstm, state_c_mlstm, state_n_mlstm, state_m_mlstm, cache_k_moba, cache_v_moba, state_conv_rglru, state_h_rglru, norm_g_even, mod_w_even, mod_b_even, w_in_even, conv_w_even, conv_b_even, b_igate_even, b_fgate_even, w_out_even, norm_g_odd, mod_w_odd, mod_b_odd, w_in_odd, conv_w_odd, conv_b_odd, w_rgate_odd, b_rgate_odd, w_igate_odd, b_igate_odd, lru_lambda_odd, w_out_odd, final_g):
    B, T, D = x_prompt.shape
    DB, DT, _ = x_sample.shape
    W = HALF_WIDTH
    H = ML_HEADS
    heads = W // HEAD_DIM
    n_pool, page_rows = cache_k_sb.shape[1], cache_k_sb.shape[2]
    past = page_table.shape[1] * page_rows
    depth = norm_g_even.shape[0] + norm_g_odd.shape[0]
    TM = 256
    PAGES_PER_STEP = 8
    NEW_KEYS = LANES
    ML_PAD = SUBLANES

    xp = x_prompt.reshape(B * T, D)
    xs = x_sample.reshape(DB * DT, D)
    c_all = jnp.concatenate([c_prompt, c_sample], axis=0)
    c_rows = -(-c_all.shape[0] // SUBLANES) * SUBLANES
    c_all = jnp.pad(c_all, ((0, c_rows - c_all.shape[0]), (0, 0)))
    cos_p, sin_p = _rope_tables(jnp.arange(T))
    cos_s, sin_s = _rope_tables(past + jnp.arange(DT))
    cos_s, sin_s = jnp.tile(cos_s, (DB, 1)), jnp.tile(sin_s, (DB, 1))

    def mods(w, b):
        mod = _modulation(c_all, w, b)
        parts = []
        for part in jnp.split(mod, 3, axis=-1):
            p_part = part[:B].reshape(B, 1, D)
            s_part = jnp.repeat(part[B:B + DB], DT, axis=0).reshape(1, DB * DT, D)
            parts.append((p_part, s_part))
        return parts

    p_even, s_even, p_odd, s_odd = [], [], [], []
    y_prompt = y_sample = None
    for l in range(depth):
        j = l // 2
        last = l == depth - 1
        if l % 2 == 0:
            (sh_p, sh_s), (sc_p, sc_s), (ga_p, ga_s) = mods(mod_w_even[j], mod_b_even[j])
            qa, ka, kab, va, vab, za, qk, vb, gb, gt, gtt = _in_proj_even(
                xp, sh_p, sc_p, norm_g_even[j], w_in_even[j], TM)
            ya = _sb_prompt(qa, kab, vab, za, B, T, TM)
            yb, conv_p, caug_p, m_p = _mlstm(
                qk, vb, gb, gt, gtt,
                jnp.zeros((B, SUBLANES, 2 * W), F32), jnp.zeros((B, H, ML_HEAD_DIM, 2 * ML_HEAD_DIM), F32),
                jnp.zeros((B, 1, LANES), F32),
                conv_w_even[j], conv_b_even[j], b_igate_even[j], b_fgate_even[j], B, TM)
            xp = _out_proj(ya, yb, xp, ga_p, w_out_even[j], final_g, TM, last)
            p_even.append((ka.reshape(B, T, heads, HEAD_DIM), va.reshape(B, T, heads, HEAD_DIM),
                           conv_p[:, SUBLANES - (CONV_WIDTH - 1):], caug_p[..., :ML_HEAD_DIM],
                           caug_p[..., ML_HEAD_DIM], m_p[:, 0, :H]))
            qa, ka, kab, va, vab, za, qk, vb, gb, gt, gtt = _in_proj_even(
                xs, sh_s, sc_s, norm_g_even[j], w_in_even[j], DB * DT)
            ya = _sb_sample(page_table, _query_rows(qa.reshape(DB, DT, W)),
                            _channel_major(ka.reshape(DB, DT, W), NEW_KEYS),
                            _channel_major(va.reshape(DB, DT, W), NEW_KEYS), za.reshape(DB, DT, W),
                            _pool_channel_major(cache_k_sb[j]), _pool_channel_major(cache_v_sb[j]),
                            PAGES_PER_STEP).reshape(DB * DT, W)
            pad_t = lambda a: _pad_rows(a.reshape(DB, DT, a.shape[-1]), ML_PAD).reshape(DB * ML_PAD, a.shape[-1])
            g_rows = gt.reshape(DB, DT, LANES)[:, :, :2 * H]
            pad_gate = jnp.concatenate([jnp.full((H,), NEG_INF, F32), jnp.full((H,), jnp.inf, F32)])
            g_rows = jnp.concatenate([g_rows, jnp.broadcast_to(pad_gate, (DB, ML_PAD - DT, 2 * H))], axis=1)
            gt_s = jnp.pad(g_rows, ((0, 0), (0, 0), (0, LANES - 2 * H)))
            gtt_s = jnp.transpose(g_rows, (0, 2, 1))
            conv0 = jnp.pad(state_conv_mlstm[j], ((0, 0), (SUBLANES - (CONV_WIDTH - 1), 0), (0, 0)))
            c0aug = jnp.concatenate(
                [state_c_mlstm[j].astype(F32), state_n_mlstm[j].astype(F32)[..., None],
                 jnp.zeros((DB, H, ML_HEAD_DIM, ML_HEAD_DIM - 1), F32)], axis=-1)
            m0 = jnp.pad(state_m_mlstm[j].astype(F32), ((0, 0), (0, LANES - H))).reshape(DB, 1, LANES)
            yb, _, caug_s, m_s = _mlstm(
                pad_t(qk), pad_t(vb), pad_t(gb), gt_s, gtt_s, conv0, c0aug, m0,
                conv_w_even[j], conv_b_even[j], b_igate_even[j], b_fgate_even[j], DB, ML_PAD)
            yb = yb.reshape(DB, ML_PAD, W)[:, :DT].reshape(DB * DT, W)
            conv_s = jnp.concatenate([state_conv_mlstm[j].astype(F32), qk.reshape(DB, DT, 2 * W)],
                                     axis=1)[:, -(CONV_WIDTH - 1):]
            xs = _out_proj(ya, yb, xs, ga_s, w_out_even[j], final_g, DB * DT, last)
            s_even.append((ka.reshape(DB, DT, heads, HEAD_DIM), va.reshape(DB, DT, heads, HEAD_DIM),
                           conv_s, caug_s[..., :ML_HEAD_DIM], caug_s[..., ML_HEAD_DIM], m_s[:, 0, :H]))
        else:
            (sh_p, sh_s), (sc_p, sc_s), (ga_p, ga_s) = mods(mod_w_odd[j], mod_b_odd[j])
            lru = (conv_w_odd[j], conv_b_odd[j], w_rgate_odd[j], b_rgate_odd[j], w_igate_odd[j], b_igate_odd[j],
                   lru_lambda_odd[j])
            qb, qf, kc, kcb, vc, vcb, zc, xd, zd, km = _in_proj_odd(
                xp, sh_p, sc_p, norm_g_odd[j], w_in_odd[j], cos_p, sin_p, TM)
            yc = _moba_prompt(qb, qf, kcb, vcb, km, zc, B, T)
            yd, conv_p, h_p = _rglru_prompt(xd, zd, jnp.zeros((B, SUBLANES, W), F32), jnp.zeros((B, 1, W), F32),
                                            *lru, B, TM)
            xp = _out_proj(yc, yd, xp, ga_p, w_out_odd[j], final_g, TM, last)
            p_odd.append((kc.reshape(B, T, heads, HEAD_DIM), vc.reshape(B, T, heads, HEAD_DIM),
                          conv_p[:, SUBLANES - (CONV_WIDTH - 1):], h_p[:, 0]))
            qb, qf, kc, kcb, vc, vcb, zc, xd, zd, km = _in_proj_odd(
                xs, sh_s, sc_s, norm_g_odd[j], w_in_odd[j], cos_s, sin_s, DB * DT)
            yc = _moba_sample(page_table, _query_rows(qb.reshape(DB, DT, W)), _query_rows(qf.reshape(DB, DT, W)),
                              _channel_major(kc.reshape(DB, DT, W), NEW_KEYS),
                              _channel_major(vc.reshape(DB, DT, W), NEW_KEYS), zc.reshape(DB, DT, W),
                              _pool_channel_major(cache_k_moba[j]), _pool_channel_major(cache_v_moba[j]),
                              PAGES_PER_STEP).reshape(DB * DT, W)
            tmaj = lambda a: jnp.transpose(a.reshape(DB, DT, W), (1, 0, 2))
            yd, h_s = _rglru_sample(tmaj(xd), tmaj(zd), jnp.transpose(state_conv_rglru[j].astype(F32), (1, 0, 2)),
                                    state_h_rglru[j].astype(F32), *lru)
            yd = jnp.transpose(yd, (1, 0, 2)).reshape(DB * DT, W)
            conv_s = jnp.concatenate([state_conv_rglru[j].astype(F32), xd.reshape(DB, DT, W)],
                                     axis=1)[:, -(CONV_WIDTH - 1):]
            xs = _out_proj(yc, yd, xs, ga_s, w_out_odd[j], final_g, DB * DT, last)
            s_odd.append((kc.reshape(DB, DT, heads, HEAD_DIM), vc.reshape(DB, DT, heads, HEAD_DIM), conv_s, h_s))

    y_prompt = xp.reshape(B, T, D)
    y_sample = xs.reshape(DB, DT, D)
    stack = lambda group: [jnp.stack(a) for a in zip(*group)]
    pk_sb, pv_sb, pconv_m, pc_m, pn_m, pm_m = stack(p_even)
    sk_sb, sv_sb, sconv_m, sc_m, sn_m, sm_m = stack(s_even)
    pk_mb, pv_mb, pconv_d, ph_d = stack(p_odd)
    sk_mb, sv_mb, sconv_d, sh_d = stack(s_odd)
    return (y_prompt, y_sample,
            pk_sb, pv_sb, pconv_m, pc_m, pn_m, pm_m, pk_mb, pv_mb, pconv_d, ph_d,
            sk_sb, sv_sb, sconv_m, sc_m, sn_m, sm_m, sk_mb, sv_mb, sconv_d, sh_d)
```

```python
import functools

import jax
import jax.numpy as jnp
from jax import lax
from jax.experimental import pallas as pl
from jax.experimental.pallas import tpu as pltpu

F32 = jnp.float32
BF16 = jnp.bfloat16
HIGHEST = lax.Precision.HIGHEST

HEAD_DIM = 64
HALF_WIDTH = 512
ML_HEADS = 4
ML_HEAD_DIM = 128
CONV_WIDTH = 4
MOBA_BLOCK = 256
MOBA_TOPK = 3
RG_C = 8.0
ROPE_THETA = 10000.0
NORM_EPS = 1e-6
LANES = 128
SUBLANES = 8
VMEM_LIMIT = 56 * 1024 * 1024
NEG_INF = float("-inf")
SB_EXP_UNDERFLOW = -110.0
SB_BOUND_SLACK = 1.01
MOBA_MASK_PENALTY = -1e30
MOBA_GROUP = 4


def _params(sem):
    return pltpu.CompilerParams(dimension_semantics=sem, vmem_limit_bytes=VMEM_LIMIT)


def _sigmoid(x):
    return 1.0 / (1.0 + jnp.exp(-x))


def _silu(x):
    return x * _sigmoid(x)


def _softplus(x):
    return jnp.maximum(x, 0.0) + jnp.log1p(jnp.exp(-jnp.abs(x)))


def _log_sigmoid(x):
    return -_softplus(-x)


def _dot(a, b, precision=None):
    return jnp.dot(a, b, preferred_element_type=F32, precision=precision)


def _dot_nt(a, b, precision=None):
    return lax.dot_general(a, b, (((1,), (1,)), ((), ())), preferred_element_type=F32, precision=precision)


def _dot_tn(a, b, precision=None):
    return lax.dot_general(a, b, (((0,), (0,)), ((), ())), preferred_element_type=F32, precision=precision)


def _split_bf16(x):
    hi = x.astype(BF16)
    lo = (x - hi.astype(F32)).astype(BF16)
    return hi, lo


def _iota(shape, dim):
    return lax.broadcasted_iota(jnp.int32, shape, dim)


def _mod_kernel(c_ref, w_ref, b_ref, o_ref):
    o_ref[...] = _dot(_silu(c_ref[...]), w_ref[...], HIGHEST) + b_ref[...]


def _modulation(c, w, b):
    rows, d = c.shape
    n = w.shape[1]
    tn = 512
    return pl.pallas_call(
        _mod_kernel,
        grid=(n // tn,),
        in_specs=[pl.BlockSpec((rows, d), lambda j: (0, 0)),
                  pl.BlockSpec((d, tn), lambda j: (0, j)),
                  pl.BlockSpec((1, tn), lambda j: (0, j))],
        out_specs=pl.BlockSpec((rows, tn), lambda j: (0, j)),
        out_shape=jax.ShapeDtypeStruct((rows, n), F32),
        compiler_params=_params(("arbitrary",)),
        name="modulation",
    )(c, w, b.reshape(1, n))


def _normed(x_ref, shift_ref, scale_ref, g_ref):
    x = x_ref[...]
    ms = jnp.mean(x * x, axis=-1, keepdims=True)
    h = x * lax.rsqrt(ms + NORM_EPS) * g_ref[...]
    return h * (1.0 + scale_ref[0]) + shift_ref[0]


def _in_even_kernel(x_ref, shift_ref, scale_ref, g_ref, w_ref, wg_ref, wgt_ref,
                    qa_ref, ka_ref, kab_ref, va_ref, vab_ref, za_ref, qk_ref, vb_ref, gb_ref,
                    gt_ref, gtt_ref):
    h = _normed(x_ref, shift_ref, scale_ref, g_ref)
    hb = h.astype(BF16)
    W = HALF_WIDTH

    def proj(c):
        return _dot(hb, w_ref[:, c * W:(c + 1) * W])

    qa_ref[...] = (proj(0) * (HEAD_DIM ** -0.5)).astype(BF16)
    ka = proj(1)
    ka_ref[...] = ka
    kab_ref[...] = ka.astype(BF16)
    va = proj(2)
    va_ref[...] = va
    vab_ref[...] = va.astype(BF16)
    za_ref[...] = _silu(proj(3))
    qk_ref[:, 0:W] = proj(4)
    qk_ref[:, W:2 * W] = proj(5)
    vb_ref[...] = proj(6)
    gb_ref[...] = _sigmoid(proj(7)) * _silu(proj(8))
    gt_ref[0] = _dot(h, wg_ref[...], HIGHEST)
    gtt_ref[0] = _dot_nt(wgt_ref[...], h, HIGHEST)


def _in_proj_even(x, shift, scale, g, w_in, tm):
    rows, d = x.shape
    W = HALF_WIDTH
    nt = rows // tm
    groups = shift.shape[0]
    tiles_per_group = nt // groups
    w_main = w_in[:, :9 * W].astype(BF16)
    w_gate = jnp.pad(w_in[:, 9 * W:], ((0, 0), (0, LANES - 2 * ML_HEADS)))
    w_gate_t = w_in[:, 9 * W:].T
    row_blk = lambda n, dt: (pl.BlockSpec((tm, n), lambda i: (i, 0)), jax.ShapeDtypeStruct((rows, n), dt))
    outs = [row_blk(W, BF16), row_blk(W, F32), row_blk(W, BF16), row_blk(W, F32), row_blk(W, BF16),
            row_blk(W, F32), row_blk(2 * W, F32), row_blk(W, F32), row_blk(W, F32),
            (pl.BlockSpec((1, tm, LANES), lambda i: (i, 0, 0)), jax.ShapeDtypeStruct((nt, tm, LANES), F32)),
            (pl.BlockSpec((1, 2 * ML_HEADS, tm), lambda i: (i, 0, 0)),
             jax.ShapeDtypeStruct((nt, 2 * ML_HEADS, tm), F32))]
    rg = shift.shape[1]
    return pl.pallas_call(
        _in_even_kernel,
        grid=(nt,),
        in_specs=[pl.BlockSpec((tm, d), lambda i: (i, 0)),
                  pl.BlockSpec((1, rg, d), lambda i: (i // tiles_per_group, 0, 0)),
                  pl.BlockSpec((1, rg, d), lambda i: (i // tiles_per_group, 0, 0)),
                  pl.BlockSpec((1, d), lambda i: (0, 0)),
                  pl.BlockSpec((d, 9 * W), lambda i: (0, 0)),
                  pl.BlockSpec((d, LANES), lambda i: (0, 0)),
                  pl.BlockSpec((2 * ML_HEADS, d), lambda i: (0, 0))],
        out_specs=[o[0] for o in outs],
        out_shape=[o[1] for o in outs],
        compiler_params=_params(("arbitrary",)),
        name="in_proj_even",
    )(x, shift, scale, g.reshape(1, d), w_main, w_gate, w_gate_t)


def _in_odd_kernel(x_ref, shift_ref, scale_ref, g_ref, w_ref, cos_ref, sin_ref,
                   qb_ref, qf_ref, kc_ref, kcb_ref, vc_ref, vcb_ref, zc_ref, xd_ref, zd_ref, km_ref):
    h = _normed(x_ref, shift_ref, scale_ref, g_ref)
    hb = h.astype(BF16)
    W = HALF_WIDTH

    def proj(c):
        return _dot(hb, w_ref[:, c * W:(c + 1) * W])

    cos = cos_ref[...]
    sin = sin_ref[...]
    first_half = (_iota(cos.shape, 1) % HEAD_DIM) < (HEAD_DIM // 2)

    def rope(x):
        partner = jnp.where(first_half, pltpu.roll(x, W - HEAD_DIM // 2, 1), pltpu.roll(x, HEAD_DIM // 2, 1))
        return x * cos + partner * sin

    q = rope(proj(0))
    qf_ref[...] = q
    qb_ref[...] = (q * (HEAD_DIM ** -0.5)).astype(BF16)
    k = rope(proj(1))
    kc_ref[...] = k
    kcb_ref[...] = k.astype(BF16)
    km_ref[0] = jnp.mean(k, axis=0, keepdims=True)
    v = proj(2)
    vc_ref[...] = v
    vcb_ref[...] = v.astype(BF16)
    zc_ref[...] = _silu(proj(3))
    xd_ref[...] = proj(4)
    zd_ref[...] = _silu(proj(5))


def _in_proj_odd(x, shift, scale, g, w_in, cos, sin, tm):
    rows, d = x.shape
    W = HALF_WIDTH
    nt = rows // tm
    groups = shift.shape[0]
    tiles_per_group = nt // groups
    rg = shift.shape[1]
    row_blk = lambda n, dt: (pl.BlockSpec((tm, n), lambda i: (i, 0)), jax.ShapeDtypeStruct((rows, n), dt))
    outs = [row_blk(W, BF16), row_blk(W, F32), row_blk(W, F32), row_blk(W, BF16), row_blk(W, F32),
            row_blk(W, BF16), row_blk(W, F32), row_blk(W, F32), row_blk(W, F32),
            (pl.BlockSpec((1, 1, W), lambda i: (i, 0, 0)), jax.ShapeDtypeStruct((nt, 1, W), F32))]
    return pl.pallas_call(
        _in_odd_kernel,
        grid=(nt,),
        in_specs=[pl.BlockSpec((tm, d), lambda i: (i, 0)),
                  pl.BlockSpec((1, rg, d), lambda i: (i // tiles_per_group, 0, 0)),
                  pl.BlockSpec((1, rg, d), lambda i: (i // tiles_per_group, 0, 0)),
                  pl.BlockSpec((1, d), lambda i: (0, 0)),
                  pl.BlockSpec((d, 6 * W), lambda i: (0, 0)),
                  pl.BlockSpec((tm, W), lambda i: (i % tiles_per_group, 0)),
                  pl.BlockSpec((tm, W), lambda i: (i % tiles_per_group, 0))],
        out_specs=[o[0] for o in outs],
        out_shape=[o[1] for o in outs],
        compiler_params=_params(("arbitrary",)),
        name="in_proj_odd",
    )(x, shift, scale, g.reshape(1, d), w_in.astype(BF16), cos, sin)


def _rope_tables(positions):
    half = HEAD_DIM // 2
    freqs = ROPE_THETA ** (-jnp.arange(half, dtype=F32) / half)
    ang = positions.astype(F32)[:, None] * freqs[None, :]
    cos = jnp.cos(ang)
    sin = jnp.sin(ang)
    heads = HALF_WIDTH // HEAD_DIM
    cos_t = jnp.tile(jnp.concatenate([cos, cos], axis=-1), (1, heads))
    sin_t = jnp.tile(jnp.concatenate([-sin, sin], axis=-1), (1, heads))
    return cos_t, sin_t


def _out_proj_kernel(ya_ref, yb_ref, x_ref, gate_ref, w_ref, fg_ref, o_ref, *, final_norm):
    W = HALF_WIDTH
    out = _dot(ya_ref[...].astype(BF16), w_ref[0:W, :]) + _dot(yb_ref[...].astype(BF16), w_ref[W:2 * W, :])
    y = x_ref[...] + gate_ref[0] * out
    if final_norm:
        ms = jnp.mean(y * y, axis=-1, keepdims=True)
        y = y * lax.rsqrt(ms + NORM_EPS) * fg_ref[...]
    o_ref[...] = y


def _out_proj(ya, yb, x, gate, w_out, final_g, tm, final_norm):
    rows, d = x.shape
    W = HALF_WIDTH
    nt = rows // tm
    groups = gate.shape[0]
    tiles_per_group = nt // groups
    rg = gate.shape[1]
    return pl.pallas_call(
        functools.partial(_out_proj_kernel, final_norm=final_norm),
        grid=(nt,),
        in_specs=[pl.BlockSpec((tm, W), lambda i: (i, 0)),
                  pl.BlockSpec((tm, W), lambda i: (i, 0)),
                  pl.BlockSpec((tm, d), lambda i: (i, 0)),
                  pl.BlockSpec((1, rg, d), lambda i: (i // tiles_per_group, 0, 0)),
                  pl.BlockSpec((2 * W, d), lambda i: (0, 0)),
                  pl.BlockSpec((1, d), lambda i: (0, 0))],
        out_specs=pl.BlockSpec((tm, d), lambda i: (i, 0)),
        out_shape=jax.ShapeDtypeStruct((rows, d), F32),
        compiler_params=_params(("arbitrary",)),
        name="out_proj",
    )(ya, yb, x, gate, w_out.astype(BF16), final_g.reshape(1, d))


def _sb_prompt_kernel(q_ref, k_ref, v_ref, zs_ref, o_ref, knorm_ref, *, tq):
    qi = pl.program_id(2)
    q = q_ref[...]
    lane = _iota((tq, LANES), 1)
    head0 = lane < HEAD_DIM
    qs = (jnp.where(head0, q, jnp.zeros_like(q)), jnp.where(head0, jnp.zeros_like(q), q))

    def head_sq_norms(x):
        sq = x.astype(F32) * x.astype(F32)
        return (jnp.sum(jnp.where(head0, sq, 0.0), axis=1, keepdims=True),
                jnp.sum(jnp.where(head0, 0.0, sq), axis=1, keepdims=True))

    @pl.when(qi == 0)
    def _():
        def scan(j, m):
            n0, n1 = head_sq_norms(k_ref[pl.ds(pl.multiple_of(j * tq, tq), tq), :])
            return (jnp.maximum(m[0], jnp.max(n0, axis=0, keepdims=True)),
                    jnp.maximum(m[1], jnp.max(n1, axis=0, keepdims=True)))
        zero = jnp.zeros((1, 1), F32)
        m0, m1 = lax.fori_loop(0, k_ref.shape[0] // tq, scan, (zero, zero))
        knorm_ref[...] = jnp.where(_iota((1, LANES), 1) == 0, m0, m1)

    qn0, qn1 = head_sq_norms(q)
    zb0 = jnp.sqrt(qn0 * knorm_ref[0:1, 0:1]) * SB_BOUND_SLACK
    zb1 = jnp.sqrt(qn1 * knorm_ref[0:1, 1:2]) * SB_BOUND_SLACK
    row = _iota((tq, tq), 0)
    col = _iota((tq, tq), 1)
    suffix = jnp.where(row >= col, 1.0, 0.0).astype(BF16)
    causal = col < row

    def blocks(items, carry):
        acc, c0, c1 = carry
        loaded, parts = [], []
        for j, diag, valid in items:
            start = pl.multiple_of(j * tq, tq)
            kb = k_ref[pl.ds(start, tq), :]
            loaded.append((v_ref[pl.ds(start, tq), :], diag, valid))
            per_head = []
            for qh in qs:
                z = _dot_nt(qh, kb)
                ls = -_softplus(z)
                if diag:
                    ls = jnp.where(causal, ls, 0.0)
                hi, lo = _split_bf16(ls)
                per_head.append((z, _dot(hi, suffix) + _dot(lo, suffix)))
            parts.append(per_head)
        cs = [c0, c1]
        for (vb, diag, valid), per_head in zip(loaded, parts):
            pvs = []
            for h, (z, s) in enumerate(per_head):
                w = jnp.exp(z + s + cs[h])
                if diag:
                    w = jnp.where(causal, w, 0.0)
                total = s[:, 0:1]
                if valid is not None:
                    w = jnp.where(valid, w, 0.0)
                    total = jnp.where(valid, total, 0.0)
                pvs.append(_dot(w.astype(BF16), vb))
                cs[h] = cs[h] + total
            acc = acc + jnp.where(head0, pvs[0], pvs[1])
        return acc, cs[0], cs[1]

    def exhausted(c0, c1):
        worst = jnp.max(jnp.maximum(zb0 + c0, zb1 + c1))
        return (worst < SB_EXP_UNDERFLOW).astype(jnp.int32)

    def step(state):
        j, _, acc, c0, c1 = state
        acc, c0, c1 = blocks([(j, False, None)], (acc, c0, c1))
        return j - 1, exhausted(c0, c1), acc, c0, c1

    zero_c = jnp.zeros((tq, 1), F32)
    acc, c0, c1 = blocks([(qi, True, None), (jnp.maximum(qi - 1, 0), False, qi > 0)],
                         (jnp.zeros((tq, LANES), F32), zero_c, zero_c))
    state = lax.while_loop(lambda s: jnp.logical_and(s[0] >= 0, s[1] == 0), step,
                           (qi - 2, exhausted(c0, c1), acc, c0, c1))
    o_ref[...] = state[2] * zs_ref[...]


def _sb_prompt(q, k, v, zs, batch, seq, tq):
    rows, W = q.shape
    nq = seq // tq
    pairs = W // LANES
    return pl.pallas_call(
        functools.partial(_sb_prompt_kernel, tq=tq),
        grid=(batch, pairs, nq),
        in_specs=[pl.BlockSpec((tq, LANES), lambda b, p, i: (b * nq + i, p)),
                  pl.BlockSpec((seq, LANES), lambda b, p, i: (b, p)),
                  pl.BlockSpec((seq, LANES), lambda b, p, i: (b, p)),
                  pl.BlockSpec((tq, LANES), lambda b, p, i: (b * nq + i, p))],
        out_specs=pl.BlockSpec((tq, LANES), lambda b, p, i: (b * nq + i, p)),
        out_shape=jax.ShapeDtypeStruct((rows, W), F32),
        scratch_shapes=[pltpu.VMEM((1, LANES), F32)],
        compiler_params=_params(("arbitrary", "arbitrary", "arbitrary")),
        name="sb_prompt",
    )(q, k, v, zs)


def _moba_prompt_kernel(q_ref, qf_ref, k_ref, v_ref, km_ref, zs_ref, o_ref, kpad_ref, *, tq, nb):
    qi = pl.program_id(2)

    @pl.when(qi == 0)
    def _():
        kpad_ref[...] = jnp.zeros_like(kpad_ref)
        kpad_ref[0:nb, :] = km_ref[0]

    q = q_ref[...]
    qf = qf_ref[...]
    lane = _iota((tq, LANES), 1)
    lane_f = lane.astype(F32)
    head0 = lane < HEAD_DIM
    row = _iota((tq, tq), 0)
    col = _iota((tq, tq), 1)
    kmean = kpad_ref[...]

    qs, sels = [], []
    for h in range(2):
        hm = head0 if h == 0 else jnp.logical_not(head0)
        qs.append(jnp.where(hm, q, jnp.zeros_like(q)))
        g = _dot_nt(jnp.where(hm, qf, 0.0), kmean, HIGHEST)
        g = jnp.where(lane < qi, g, NEG_INF)
        sel = jnp.zeros((tq, LANES), F32)
        for _ in range(MOBA_TOPK):
            mx = jnp.max(g, axis=1, keepdims=True)
            is_max = jnp.logical_and(g == mx, mx > NEG_INF)
            idx = jnp.min(jnp.where(is_max, lane_f, float(LANES)), axis=1, keepdims=True)
            pick = lane_f == idx
            sel = jnp.where(pick, 1.0, sel)
            g = jnp.where(pick, NEG_INF, g)
        sels.append(sel)

    lhs = [jnp.concatenate([qs[h], jnp.where(sels[h] > 0.0, 0.0, 1.0).astype(BF16)], axis=1) for h in range(2)]
    ones = jnp.ones((tq, LANES), BF16)

    def values(vb):
        return jnp.where(head0, vb, ones), jnp.where(head0, ones, vb)

    start = pl.multiple_of(qi * tq, tq)
    kb = k_ref[pl.ds(start, tq), :]
    vmods = values(v_ref[pl.ds(start, tq), :])
    state = []
    for h in range(2):
        s = jnp.where(col <= row, _dot_nt(qs[h], kb), NEG_INF)
        m = jnp.max(s, axis=1, keepdims=True)
        p = jnp.exp(s - m)
        state += [m, _dot(p.astype(BF16), vmods[h])]

    def blocks(ns, st):
        ks, vs = [], []
        for n in ns:
            start_n = pl.multiple_of(n * tq, tq)
            penalty = jnp.where(lane == n, MOBA_MASK_PENALTY, 0.0).astype(BF16)
            ks.append(jnp.concatenate([k_ref[pl.ds(start_n, tq), :], penalty], axis=1))
            vs.append(values(v_ref[pl.ds(start_n, tq), :]))
        ss = [[_dot_nt(lhs[h], kn) for kn in ks] for h in range(2)]
        new = []
        for h in range(2):
            m, acc = st[2 * h:2 * h + 2]
            m_new = m
            for s in ss[h]:
                m_new = jnp.maximum(m_new, jnp.max(s, axis=1, keepdims=True))
            ps = [jnp.exp(s - m_new).astype(BF16) for s in ss[h]]
            acc = jnp.exp(m - m_new) * acc
            for p, vn in zip(ps, vs):
                acc = acc + _dot(p, vn[h])
            new += [m_new, acc]
        return tuple(new)

    group = MOBA_GROUP if nb % MOBA_GROUP == 0 else 1
    st = lax.fori_loop(0, (qi + group - 1) // group,
                       lambda i, s: blocks([group * i + r for r in range(group)], s), tuple(state))
    o = jnp.where(head0, st[1] / pltpu.roll(st[1], HEAD_DIM, 1), st[3] / pltpu.roll(st[3], HEAD_DIM, 1))
    o_ref[...] = o * zs_ref[...]


def _moba_prompt(q, qf, k, v, kmean, zs, batch, seq):
    rows, W = q.shape
    tq = MOBA_BLOCK
    nq = seq // tq
    pairs = W // LANES
    return pl.pallas_call(
        functools.partial(_moba_prompt_kernel, tq=tq, nb=nq),
        grid=(batch, pairs, nq),
        in_specs=[pl.BlockSpec((tq, LANES), lambda b, p, i: (b * nq + i, p)),
                  pl.BlockSpec((tq, LANES), lambda b, p, i: (b * nq + i, p)),
                  pl.BlockSpec((seq, LANES), lambda b, p, i: (b, p)),
                  pl.BlockSpec((seq, LANES), lambda b, p, i: (b, p)),
                  pl.BlockSpec((1, nq, LANES), lambda b, p, i: (b, 0, p)),
                  pl.BlockSpec((tq, LANES), lambda b, p, i: (b * nq + i, p))],
        out_specs=pl.BlockSpec((tq, LANES), lambda b, p, i: (b * nq + i, p)),
        out_shape=jax.ShapeDtypeStruct((rows, W), F32),
        scratch_shapes=[pltpu.VMEM((LANES, LANES), F32)],
        compiler_params=_params(("arbitrary", "arbitrary", "arbitrary")),
        name="moba_prompt",
    )(q, qf, k, v, kmean.reshape(batch, nq, W), zs)


def _mlstm_kernel(qk_ref, v_ref, gb_ref, gt_ref, gtt_ref, conv0_ref, c0_ref, m0_ref,
                  cw_ref, cb_ref, brow_ref, bcol_ref,
                  y_ref, conv_out_ref, c_out_ref, m_out_ref,
                  xpad_ref, caug_ref, m_ref, *, L):
    c = pl.program_id(1)
    nc = pl.num_programs(1)
    H, DK = ML_HEADS, ML_HEAD_DIM
    W = HALF_WIDTH
    PADR = SUBLANES

    @pl.when(c == 0)
    def _():
        xpad_ref[0:PADR, :] = conv0_ref[0]
        caug_ref[...] = c0_ref[0]
        m_ref[...] = m0_ref[0]

    xpad_ref[PADR:PADR + L, :] = qk_ref[...]
    y = cb_ref[...]
    for j in range(CONV_WIDTH):
        off = PADR - (CONV_WIDTH - 1) + j
        y = y + cw_ref[j:j + 1, :] * xpad_ref[off:off + L, :]
    tail = xpad_ref[L:L + PADR, :]
    xpad_ref[0:PADR, :] = tail
    conv_out_ref[0] = tail
    qk = _silu(y)

    lane = _iota((L, LANES), 1)
    is_f_col = jnp.logical_and(lane >= H, lane < 2 * H)
    gcol = gt_ref[0] + brow_ref[...]
    lf_col = jnp.where(is_f_col, _log_sigmoid(gcol), 0.0)
    grow = gtt_ref[0] + bcol_ref[...]
    sub = _iota((2 * H, L), 0)
    lf_row = jnp.where(sub >= H, _log_sigmoid(grow), 0.0)
    row = _iota((L, L), 0)
    col = _iota((L, L), 1)
    causal = col <= row
    tri = jnp.where(causal, 1.0, 0.0)
    f_col = _dot(tri, lf_col, HIGHEST)
    f_row = _dot_nt(lf_row, tri, HIGHEST)
    ones_col = jnp.where(_iota((L, DK), 1) == 0, 1.0, 0.0).astype(BF16)
    m_all = m_ref[...]
    m_next = m_all
    lane1 = _iota((1, LANES), 1)

    for h in range(H):
        fc = f_col[:, H + h:H + h + 1]
        fr = f_row[H + h:H + h + 1, :]
        li_c = gcol[:, h:h + 1]
        li_r = grow[h:h + 1, :]
        m_prev = m_all[:, h:h + 1]
        inter = fc + m_prev
        intra = jnp.where(causal, fc - fr + li_r, NEG_INF)
        mt = jnp.maximum(inter, jnp.max(intra, axis=1, keepdims=True))
        w = jnp.exp(intra - mt)
        g = jnp.exp(inter - mt)
        qh = qk[:, h * DK:(h + 1) * DK].astype(BF16)
        kf = qk[:, W + h * DK:W + (h + 1) * DK] * (DK ** -0.5)
        kh = kf.astype(BF16)
        vaug = jnp.concatenate([v_ref[:, h * DK:(h + 1) * DK].astype(BF16), ones_col], axis=1)
        s = _dot_nt(qh, kh) * w
        nd = g * _dot(qh, caug_ref[h].astype(BF16)) + _dot(s.astype(BF16), vaug)
        den = nd[:, DK:DK + 1]
        hout = nd[:, 0:DK] / jnp.maximum(jnp.abs(den), jnp.exp(-mt))
        y_ref[:, h * DK:(h + 1) * DK] = gb_ref[:, h * DK:(h + 1) * DK] * hout
        m_new = mt[L - 1:L, :]
        f_last = fc[L - 1:L, :]
        decay = jnp.exp(f_last + m_prev - m_new)
        ws = jnp.exp(f_last - fc + li_c - m_new)
        caug_ref[h] = decay * caug_ref[h] + _dot_tn((kf * ws).astype(BF16), vaug)
        m_next = jnp.where(lane1 == h, m_new, m_next)

    m_ref[...] = m_next

    @pl.when(c == nc - 1)
    def _():
        c_out_ref[0] = caug_ref[...]
        m_out_ref[0] = m_next


def _mlstm(qk_pre, vb, gb, gates, gates_t, conv0, c0aug, m0, conv_w, conv_b, b_ig, b_fg, batch, L):
    rows, W2 = qk_pre.shape
    W = HALF_WIDTH
    H = ML_HEADS
    nc = rows // batch // L
    brow = jnp.pad(jnp.concatenate([b_ig, b_fg]).reshape(1, 2 * H), ((0, 0), (0, LANES - 2 * H)))
    bcol = jnp.concatenate([b_ig, b_fg]).reshape(2 * H, 1)
    out_shapes = [jax.ShapeDtypeStruct((rows, W), F32),
                  jax.ShapeDtypeStruct((batch, SUBLANES, W2), F32),
                  jax.ShapeDtypeStruct((batch, H, ML_HEAD_DIM, 2 * ML_HEAD_DIM), F32),
                  jax.ShapeDtypeStruct((batch, 1, LANES), F32)]
    return pl.pallas_call(
        functools.partial(_mlstm_kernel, L=L),
        grid=(batch, nc),
        in_specs=[pl.BlockSpec((L, W2), lambda b, c: (b * nc + c, 0)),
                  pl.BlockSpec((L, W), lambda b, c: (b * nc + c, 0)),
                  pl.BlockSpec((L, W), lambda b, c: (b * nc + c, 0)),
                  pl.BlockSpec((1, L, LANES), lambda b, c: (b * nc + c, 0, 0)),
                  pl.BlockSpec((1, 2 * H, L), lambda b, c: (b * nc + c, 0, 0)),
                  pl.BlockSpec((1, SUBLANES, W2), lambda b, c: (b, 0, 0)),
                  pl.BlockSpec((1, H, ML_HEAD_DIM, 2 * ML_HEAD_DIM), lambda b, c: (b, 0, 0, 0)),
                  pl.BlockSpec((1, 1, LANES), lambda b, c: (b, 0, 0)),
                  pl.BlockSpec((CONV_WIDTH, W2), lambda b, c: (0, 0)),
                  pl.BlockSpec((1, W2), lambda b, c: (0, 0)),
                  pl.BlockSpec((1, LANES), lambda b, c: (0, 0)),
                  pl.BlockSpec((2 * H, 1), lambda b, c: (0, 0))],
        out_specs=[pl.BlockSpec((L, W), lambda b, c: (b * nc + c, 0)),
                   pl.BlockSpec((1, SUBLANES, W2), lambda b, c: (b, 0, 0)),
                   pl.BlockSpec((1, H, ML_HEAD_DIM, 2 * ML_HEAD_DIM), lambda b, c: (b, 0, 0, 0)),
                   pl.BlockSpec((1, 1, LANES), lambda b, c: (b, 0, 0))],
        out_shape=out_shapes,
        scratch_shapes=[pltpu.VMEM((L + SUBLANES, W2), F32),
                        pltpu.VMEM((H, ML_HEAD_DIM, 2 * ML_HEAD_DIM), F32),
                        pltpu.VMEM((1, LANES), F32)],
        compiler_params=_params(("arbitrary", "arbitrary")),
        name="mlstm",
    )(qk_pre, vb, gb, gates, gates_t, conv0, c0aug, m0, conv_w, conv_b.reshape(1, W2), brow, bcol)


def _rglru_gates(xc, wr_ref, br_ref, wi_ref, bi_ref, lam_ref):
    xb = xc.astype(BF16)
    r = _sigmoid(_dot(xb, wr_ref[...]) + br_ref[...])
    i = _sigmoid(_dot(xb, wi_ref[...]) + bi_ref[...])
    log_a = RG_C * r * _log_sigmoid(lam_ref[...])
    a = jnp.exp(log_a)
    b = jnp.sqrt(-jnp.tanh(log_a) * (a * a + 1.0)) * (i * xc)
    return a, b


def _rglru_prompt_kernel(xd_ref, zs_ref, conv0_ref, h0_ref, cw_ref, cb_ref, wr_ref, br_ref, wi_ref, bi_ref,
                         lam_ref, y_ref, conv_out_ref, h_out_ref,
                         xpad_ref, a_ref, b_ref, hs_ref, hc_ref, *, L):
    c = pl.program_id(1)
    PADR = SUBLANES

    @pl.when(c == 0)
    def _():
        xpad_ref[0:PADR, :] = conv0_ref[0]
        hc_ref[...] = h0_ref[0]

    xpad_ref[PADR:PADR + L, :] = xd_ref[...]
    xc = cb_ref[...]
    for j in range(CONV_WIDTH):
        off = PADR - (CONV_WIDTH - 1) + j
        xc = xc + cw_ref[j:j + 1, :] * xpad_ref[off:off + L, :]
    tail = xpad_ref[L:L + PADR, :]
    xpad_ref[0:PADR, :] = tail
    conv_out_ref[0] = tail

    a, b = _rglru_gates(xc, wr_ref, br_ref, wi_ref, bi_ref, lam_ref)
    a_ref[...] = a
    b_ref[...] = b

    def step(t, h):
        h = a_ref[pl.ds(t, 1), :] * h + b_ref[pl.ds(t, 1), :]
        hs_ref[pl.ds(t, 1), :] = h
        return h

    h = lax.fori_loop(0, L, step, hc_ref[...], unroll=8)
    hc_ref[...] = h
    h_out_ref[0] = h
    y_ref[...] = hs_ref[...] * zs_ref[...]


def _block_diag(w):
    g, n, _ = w.shape
    eye = jnp.eye(g, dtype=w.dtype)
    return (eye[:, None, :, None] * w[:, :, None, :]).reshape(g * n, g * n)


def _rglru_prompt(xd, zs, conv0, h0, conv_w, conv_b, wr, br, wi, bi, lam, batch, L):
    rows, W = xd.shape
    nc = rows // batch // L
    vec = lambda a: a.reshape(1, W)
    const = lambda shape: pl.BlockSpec(shape, lambda b, c: tuple(0 for _ in shape))
    return pl.pallas_call(
        functools.partial(_rglru_prompt_kernel, L=L),
        grid=(batch, nc),
        in_specs=[pl.BlockSpec((L, W), lambda b, c: (b * nc + c, 0)),
                  pl.BlockSpec((L, W), lambda b, c: (b * nc + c, 0)),
                  pl.BlockSpec((1, SUBLANES, W), lambda b, c: (b, 0, 0)),
                  pl.BlockSpec((1, 1, W), lambda b, c: (b, 0, 0)),
                  const((CONV_WIDTH, W)), const((1, W)), const((W, W)), const((1, W)), const((W, W)),
                  const((1, W)), const((1, W))],
        out_specs=[pl.BlockSpec((L, W), lambda b, c: (b * nc + c, 0)),
                   pl.BlockSpec((1, SUBLANES, W), lambda b, c: (b, 0, 0)),
                   pl.BlockSpec((1, 1, W), lambda b, c: (b, 0, 0))],
        out_shape=[jax.ShapeDtypeStruct((rows, W), F32),
                   jax.ShapeDtypeStruct((batch, SUBLANES, W), F32),
                   jax.ShapeDtypeStruct((batch, 1, W), F32)],
        scratch_shapes=[pltpu.VMEM((L + SUBLANES, W), F32), pltpu.VMEM((L, W), F32), pltpu.VMEM((L, W), F32),
                        pltpu.VMEM((L, W), F32), pltpu.VMEM((1, W), F32)],
        compiler_params=_params(("arbitrary", "arbitrary")),
        name="rglru_prompt",
    )(xd, zs, conv0, h0, conv_w, vec(conv_b), _block_diag(wr).astype(BF16), vec(br),
      _block_diag(wi).astype(BF16), vec(bi), vec(lam))


def _rglru_sample_kernel(xd_ref, zs_ref, conv0_ref, h0_ref, cw_ref, cb_ref, wr_ref, br_ref, wi_ref, bi_ref,
                         lam_ref, y_ref, h_out_ref, *, T):
    xs = [conv0_ref[j] for j in range(CONV_WIDTH - 1)] + [xd_ref[t] for t in range(T)]
    h = h0_ref[...]
    for t in range(T):
        xc = cb_ref[...]
        for j in range(CONV_WIDTH):
            xc = xc + cw_ref[j:j + 1, :] * xs[t + j]
        a, b = _rglru_gates(xc, wr_ref, br_ref, wi_ref, bi_ref, lam_ref)
        h = a * h + b
        y_ref[t] = h * zs_ref[t]
    h_out_ref[...] = h


def _rglru_sample(xd, zs, conv0, h0, conv_w, conv_b, wr, br, wi, bi, lam):
    T, B, W = xd.shape
    vec = lambda a: a.reshape(1, W)
    return pl.pallas_call(
        functools.partial(_rglru_sample_kernel, T=T),
        out_shape=[jax.ShapeDtypeStruct((T, B, W), F32), jax.ShapeDtypeStruct((B, W), F32)],
        compiler_params=pltpu.CompilerParams(vmem_limit_bytes=VMEM_LIMIT),
        name="rglru_sample",
    )(xd, zs, conv0, h0, conv_w, vec(conv_b), _block_diag(wr).astype(BF16), vec(br),
      _block_diag(wi).astype(BF16), vec(bi), vec(lam))


def _query_rows(q):
    b, t, w = q.shape
    heads = w // HEAD_DIM
    onehot = (jnp.arange(heads)[:, None] == (jnp.arange(w) // HEAD_DIM)[None, :]).astype(q.dtype)
    return (q[:, :, None, :] * onehot[None, None]).reshape(b, t * heads, w)


def _channel_major(a, lanes):
    return jnp.pad(jnp.transpose(a, (0, 2, 1)), ((0, 0), (0, 0), (0, lanes - a.shape[1])))


def _pool_channel_major(pool):
    n_pool, page_rows, heads, hd = pool.shape
    return jnp.transpose(pool, (0, 2, 3, 1)).reshape(n_pool, heads * hd, page_rows)


def _page_specs(pages_per_step, page_rows, width, page_of):
    def spec(i):
        return pl.BlockSpec((None, width, page_rows), lambda b, c, pt: (pt[b, page_of(c, i)], 0, 0))
    return [spec(i) for i in range(pages_per_step)]


def _head_diagonal(o, t_new, zs):
    heads = HALF_WIDTH // HEAD_DIM
    rows = t_new * heads
    keep = (_iota((rows, HALF_WIDTH), 1) // HEAD_DIM) == (_iota((rows, HALF_WIDTH), 0) % heads)
    o = jnp.where(keep, o, 0.0)
    return jnp.sum(o.reshape(t_new, heads, HALF_WIDTH), axis=1) * zs


def _sb_sample_kernel(pt_ref, q_ref, knew_ref, vnew_ref, zs_ref, *rest, P, t_new):
    kpages, vpages = rest[:P], rest[P:2 * P]
    o_ref, acc_ref, car_ref = rest[2 * P:]
    c = pl.program_id(1)
    nch = pl.num_programs(1)
    q = q_ref[0]
    heads = HALF_WIDTH // HEAD_DIM
    rows = t_new * heads

    def key_blocks(kts, vts, mask):
        n = kts[0].shape[1]
        suffix = jnp.where(_iota((n, n), 0) >= _iota((n, n), 1), 1.0, 0.0).astype(BF16)
        zs = [_dot(q, kt.astype(BF16)) for kt in kts]
        lss = [-_softplus(z) for z in zs]
        if mask is not None:
            lss = [jnp.where(mask, ls, 0.0) for ls in lss]
        sums = []
        for ls in lss:
            hi, lo = _split_bf16(ls)
            sums.append(_dot(hi, suffix) + _dot(lo, suffix))
        carry = car_ref[...]
        acc = acc_ref[...]
        for z, s, vt in zip(zs, sums, vts):
            w = jnp.exp(z + s + carry)
            if mask is not None:
                w = jnp.where(mask, w, 0.0)
            acc = acc + _dot_nt(w.astype(BF16), vt.astype(BF16))
            carry = carry + s[:, 0:1]
        acc_ref[...] = acc
        car_ref[...] = carry

    @pl.when(c == 0)
    def _():
        acc_ref[...] = jnp.zeros_like(acc_ref)
        car_ref[...] = jnp.zeros_like(car_ref)
        n = knew_ref.shape[2]
        j = _iota((rows, n), 1)
        t = _iota((rows, n), 0) // heads
        key_blocks([knew_ref[0]], [vnew_ref[0]], jnp.logical_and(j < t, j < t_new))

    key_blocks([kpages[i][...] for i in reversed(range(P))], [vpages[i][...] for i in reversed(range(P))], None)

    @pl.when(c == nch - 1)
    def _():
        o_ref[0] = _head_diagonal(acc_ref[...], t_new, zs_ref[0])


def _sb_sample(page_table, q_rows, knew_t, vnew_t, zs, k_pool, v_pool, P):
    B, n_pages = page_table.shape
    t_new = zs.shape[1]
    W = HALF_WIDTH
    rows = q_rows.shape[1]
    page_rows = k_pool.shape[2]
    nch = n_pages // P
    page_of = lambda c, i: (nch - 1 - c) * P + i
    per_b = lambda shape: pl.BlockSpec((1,) + shape, lambda b, c, pt: (b,) + tuple(0 for _ in shape))
    grid_spec = pltpu.PrefetchScalarGridSpec(
        num_scalar_prefetch=1,
        grid=(B, nch),
        in_specs=[per_b((rows, W)), per_b(knew_t.shape[1:]), per_b(vnew_t.shape[1:]), per_b((t_new, W))]
        + _page_specs(P, page_rows, W, page_of) + _page_specs(P, page_rows, W, page_of),
        out_specs=per_b((t_new, W)),
        scratch_shapes=[pltpu.VMEM((rows, W), F32), pltpu.VMEM((rows, 1), F32)],
    )
    return pl.pallas_call(
        functools.partial(_sb_sample_kernel, P=P, t_new=t_new),
        grid_spec=grid_spec,
        out_shape=jax.ShapeDtypeStruct((B, t_new, W), F32),
        compiler_params=_params(("arbitrary", "arbitrary")),
        name="sb_sample",
    )(page_table, q_rows, knew_t, vnew_t, zs, *([k_pool] * P), *([v_pool] * P))


def _moba_sample_kernel(pt_ref, q_ref, qg_ref, knew_ref, vnew_ref, zs_ref, *rest, P, t_new, nb, pb):
    kpages, vpages = rest[:P], rest[P:2 * P]
    o_ref, m_ref, l_ref, acc_ref, km_ref = rest[2 * P:]
    c = pl.program_id(1)
    nch = pl.num_programs(1)
    q = q_ref[0]
    heads = HALF_WIDTH // HEAD_DIM
    rows = t_new * heads
    lane = _iota((rows, LANES), 1)

    @pl.when(c == 0)
    def _():
        km_ref[...] = jnp.zeros_like(km_ref)
        m_ref[...] = jnp.zeros_like(m_ref)
        l_ref[...] = jnp.zeros_like(l_ref)

    nblk = P // pb
    zs = [[_dot(q, kpages[pb * s + r][...].astype(BF16)) for r in range(pb)] for s in range(nblk)]
    km_all, m_all, l_all = km_ref[...], m_ref[...], l_ref[...]
    km_lane = _iota(km_ref.shape, 1)
    for s in range(nblk):
        n = c * nblk + s
        ksum = kpages[pb * s][...]
        for r in range(1, pb):
            ksum = ksum + kpages[pb * s + r][...]
        kmean = jnp.sum(ksum, axis=1, keepdims=True) * (1.0 / MOBA_BLOCK)
        km_all = jnp.where(km_lane == n, kmean, km_all)
        m = jnp.max(zs[s][0], axis=1, keepdims=True)
        for z in zs[s][1:]:
            m = jnp.maximum(m, jnp.max(z, axis=1, keepdims=True))
        l = jnp.zeros((rows, 1), F32)
        a = jnp.zeros((rows, HALF_WIDTH), F32)
        for r, z in enumerate(zs[s]):
            p = jnp.exp(z - m)
            l = l + jnp.sum(p, axis=1, keepdims=True)
            a = a + _dot_nt(p.astype(BF16), vpages[pb * s + r][...].astype(BF16))
        m_all = jnp.where(lane == n, m, m_all)
        l_all = jnp.where(lane == n, l, l_all)
        acc_ref[n] = a
    km_ref[...] = km_all
    m_ref[...] = m_all
    l_ref[...] = l_all

    @pl.when(c == nch - 1)
    def _():
        lane_f = lane.astype(F32)
        g = _dot(qg_ref[0], km_ref[...], HIGHEST)
        g = jnp.where(lane < nb, g, NEG_INF)
        sel = jnp.zeros((rows, LANES), F32)
        for _ in range(min(MOBA_TOPK, nb)):
            mx = jnp.max(g, axis=1, keepdims=True)
            is_max = jnp.logical_and(g == mx, mx > NEG_INF)
            idx = jnp.min(jnp.where(is_max, lane_f, float(LANES)), axis=1, keepdims=True)
            pick = lane_f == idx
            sel = jnp.where(pick, 1.0, sel)
            g = jnp.where(pick, NEG_INF, g)
        nn = knew_ref.shape[2]
        j = _iota((rows, nn), 1)
        t = _iota((rows, nn), 0) // heads
        zn = jnp.where(jnp.logical_and(j <= t, j < t_new), _dot(q, knew_ref[0].astype(BF16)), NEG_INF)
        m_all = m_ref[...]
        m_tot = jnp.maximum(jnp.max(jnp.where(sel > 0.0, m_all, NEG_INF), axis=1, keepdims=True),
                            jnp.max(zn, axis=1, keepdims=True))
        coef = jnp.where(sel > 0.0, jnp.exp(m_all - m_tot), 0.0)
        p_own = jnp.exp(zn - m_tot)
        denom = jnp.sum(coef * l_ref[...], axis=1, keepdims=True) + jnp.sum(p_own, axis=1, keepdims=True)
        o = _dot_nt(p_own.astype(BF16), vnew_ref[0].astype(BF16))
        for n in range(nb):
            o = o + coef[:, n:n + 1] * acc_ref[n]
        o_ref[0] = _head_diagonal(o / denom, t_new, zs_ref[0])


def _moba_sample(page_table, q_rows, qg_rows, knew_t, vnew_t, zs, k_pool, v_pool, P):
    B, n_pages = page_table.shape
    t_new = zs.shape[1]
    W = HALF_WIDTH
    rows = q_rows.shape[1]
    page_rows = k_pool.shape[2]
    pb = MOBA_BLOCK // page_rows
    nch = n_pages // P
    nb = n_pages // pb
    assert nb <= LANES and P % pb == 0
    page_of = lambda c, i: c * P + i
    per_b = lambda shape: pl.BlockSpec((1,) + shape, lambda b, c, pt: (b,) + tuple(0 for _ in shape))
    grid_spec = pltpu.PrefetchScalarGridSpec(
        num_scalar_prefetch=1,
        grid=(B, nch),
        in_specs=[per_b((rows, W)), per_b((rows, W)), per_b(knew_t.shape[1:]), per_b(vnew_t.shape[1:]),
                  per_b((t_new, W))]
        + _page_specs(P, page_rows, W, page_of) + _page_specs(P, page_rows, W, page_of),
        out_specs=per_b((t_new, W)),
        scratch_shapes=[pltpu.VMEM((rows, LANES), F32), pltpu.VMEM((rows, LANES), F32),
                        pltpu.VMEM((nb, rows, W), F32), pltpu.VMEM((W, LANES), F32)],
    )
    return pl.pallas_call(
        functools.partial(_moba_sample_kernel, P=P, t_new=t_new, nb=nb, pb=pb),
        grid_spec=grid_spec,
        out_shape=jax.ShapeDtypeStruct((B, t_new, W), F32),
        compiler_params=_params(("arbitrary", "arbitrary")),
        name="moba_sample",
    )(page_table, q_rows, qg_rows, knew_t, vnew_t, zs, *([k_pool] * P), *([v_pool] * P))


def _pad_rows(a, rows):
    return jnp.pad(a, ((0, 0), (0, rows - a.shape[1]), (0, 0)))


def kernel(x_prompt, x_sample, c_prompt, c_sample, page_table, cache_k_sb, cache_v_sb, state_conv_mlstm, state_c_mlstm, state_n_mlstm, state_m_mlstm, cache_k_moba, cache_v_moba, state_conv_rglru, state_h_rglru, norm_g_even, mod_w_even, mod_b_even, w_in_even, conv_w_even, conv_b_even, b_igate_even, b_fgate_even, w_out_even, norm_g_odd, mod_w_odd, mod_b_odd, w_in_odd, conv_w_odd, conv_b_odd, w_rgate_odd, b_rgate_odd, w_igate_odd, b_igate_odd, lru_lambda_odd, w_out_odd, final_g):
    B, T, D = x_prompt.shape
    DB, DT, _ = x_sample.shape
    W = HALF_WIDTH
    H = ML_HEADS
    heads = W // HEAD_DIM
    n_pool, page_rows = cache_k_sb.shape[1], cache_k_sb.shape[2]
    past = page_table.shape[1] * page_rows
    depth = norm_g_even.shape[0] + norm_g_odd.shape[0]
    TM = 256
    PAGES_PER_STEP = 8
    NEW_KEYS = LANES
    ML_PAD = SUBLANES

    xp = x_prompt.reshape(B * T, D)
    xs = x_sample.reshape(DB * DT, D)
    c_all = jnp.concatenate([c_prompt, c_sample], axis=0)
    c_rows = -(-c_all.shape[0] // SUBLANES) * SUBLANES
    c_all = jnp.pad(c_all, ((0, c_rows - c_all.shape[0]), (0, 0)))
    cos_p, sin_p = _rope_tables(jnp.arange(T))
    cos_s, sin_s = _rope_tables(past + jnp.arange(DT))
    cos_s, sin_s = jnp.tile(cos_s, (DB, 1)), jnp.tile(sin_s, (DB, 1))

    def mods(w, b):
        mod = _modulation(c_all, w, b)
        parts = []
        for part in jnp.split(mod, 3, axis=-1):
            p_part = part[:B].reshape(B, 1, D)
            s_part = jnp.repeat(part[B:B + DB], DT, axis=0).reshape(1, DB * DT, D)
            parts.append((p_part, s_part))
        return parts

    p_even, s_even, p_odd, s_odd = [], [], [], []
    y_prompt = y_sample = None
    for l in range(depth):
        j = l // 2
        last = l == depth - 1
        if l % 2 == 0:
            (sh_p, sh_s), (sc_p, sc_s), (ga_p, ga_s) = mods(mod_w_even[j], mod_b_even[j])
            qa, ka, kab, va, vab, za, qk, vb, gb, gt, gtt = _in_proj_even(
                xp, sh_p, sc_p, norm_g_even[j], w_in_even[j], TM)
            ya = _sb_prompt(qa, kab, vab, za, B, T, TM)
            yb, conv_p, caug_p, m_p = _mlstm(
                qk, vb, gb, gt, gtt,
                jnp.zeros((B, SUBLANES, 2 * W), F32), jnp.zeros((B, H, ML_HEAD_DIM, 2 * ML_HEAD_DIM), F32),
                jnp.zeros((B, 1, LANES), F32),
                conv_w_even[j], conv_b_even[j], b_igate_even[j], b_fgate_even[j], B, TM)
            xp = _out_proj(ya, yb, xp, ga_p, w_out_even[j], final_g, TM, last)
            p_even.append((ka.reshape(B, T, heads, HEAD_DIM), va.reshape(B, T, heads, HEAD_DIM),
                           conv_p[:, SUBLANES - (CONV_WIDTH - 1):], caug_p[..., :ML_HEAD_DIM],
                           caug_p[..., ML_HEAD_DIM], m_p[:, 0, :H]))
            qa, ka, kab, va, vab, za, qk, vb, gb, gt, gtt = _in_proj_even(
                xs, sh_s, sc_s, norm_g_even[j], w_in_even[j], DB * DT)
            ya = _sb_sample(page_table, _query_rows(qa.reshape(DB, DT, W)),
                            _channel_major(ka.reshape(DB, DT, W), NEW_KEYS),
                            _channel_major(va.reshape(DB, DT, W), NEW_KEYS), za.reshape(DB, DT, W),
                            _pool_channel_major(cache_k_sb[j]), _pool_channel_major(cache_v_sb[j]),
                            PAGES_PER_STEP).reshape(DB * DT, W)
            pad_t = lambda a: _pad_rows(a.reshape(DB, DT, a.shape[-1]), ML_PAD).reshape(DB * ML_PAD, a.shape[-1])
            g_rows = gt.reshape(DB, DT, LANES)[:, :, :2 * H]
            pad_gate = jnp.concatenate([jnp.full((H,), NEG_INF, F32), jnp.full((H,), jnp.inf, F32)])
            g_rows = jnp.concatenate([g_rows, jnp.broadcast_to(pad_gate, (DB, ML_PAD - DT, 2 * H))], axis=1)
            gt_s = jnp.pad(g_rows, ((0, 0), (0, 0), (0, LANES - 2 * H)))
            gtt_s = jnp.transpose(g_rows, (0, 2, 1))
            conv0 = jnp.pad(state_conv_mlstm[j], ((0, 0), (SUBLANES - (CONV_WIDTH - 1), 0), (0, 0)))
            c0aug = jnp.concatenate(
                [state_c_mlstm[j].astype(F32), state_n_mlstm[j].astype(F32)[..., None],
                 jnp.zeros((DB, H, ML_HEAD_DIM, ML_HEAD_DIM - 1), F32)], axis=-1)
            m0 = jnp.pad(state_m_mlstm[j].astype(F32), ((0, 0), (0, LANES - H))).reshape(DB, 1, LANES)
            yb, _, caug_s, m_s = _mlstm(
                pad_t(qk), pad_t(vb), pad_t(gb), gt_s, gtt_s, conv0, c0aug, m0,
                conv_w_even[j], conv_b_even[j], b_igate_even[j], b_fgate_even[j], DB, ML_PAD)
            yb = yb.reshape(DB, ML_PAD, W)[:, :DT].reshape(DB * DT, W)
            conv_s = jnp.concatenate([state_conv_mlstm[j].astype(F32), qk.reshape(DB, DT, 2 * W)],
                                     axis=1)[:, -(CONV_WIDTH - 1):]
            xs = _out_proj(ya, yb, xs, ga_s, w_out_even[j], final_g, DB * DT, last)
            s_even.append((ka.reshape(DB, DT, heads, HEAD_DIM), va.reshape(DB, DT, heads, HEAD_DIM),
                           conv_s, caug_s[..., :ML_HEAD_DIM], caug_s[..., ML_HEAD_DIM], m_s[:, 0, :H]))
        else:
            (sh_p, sh_s), (sc_p, sc_s), (ga_p, ga_s) = mods(mod_w_odd[j], mod_b_odd[j])
            lru = (conv_w_odd[j], conv_b_odd[j], w_rgate_odd[j], b_rgate_odd[j], w_igate_odd[j], b_igate_odd[j],
                   lru_lambda_odd[j])
            qb, qf, kc, kcb, vc, vcb, zc, xd, zd, km = _in_proj_odd(
                xp, sh_p, sc_p, norm_g_odd[j], w_in_odd[j], cos_p, sin_p, TM)
            yc = _moba_prompt(qb, qf, kcb, vcb, km, zc, B, T)
            yd, conv_p, h_p = _rglru_prompt(xd, zd, jnp.zeros((B, SUBLANES, W), F32), jnp.zeros((B, 1, W), F32),
                                            *lru, B, TM)
            xp = _out_proj(yc, yd, xp, ga_p, w_out_odd[j], final_g, TM, last)
            p_odd.append((kc.reshape(B, T, heads, HEAD_DIM), vc.reshape(B, T, heads, HEAD_DIM),
                          conv_p[:, SUBLANES - (CONV_WIDTH - 1):], h_p[:, 0]))
            qb, qf, kc, kcb, vc, vcb, zc, xd, zd, km = _in_proj_odd(
                xs, sh_s, sc_s, norm_g_odd[j], w_in_odd[j], cos_s, sin_s, DB * DT)
            yc = _moba_sample(page_table, _query_rows(qb.reshape(DB, DT, W)), _query_rows(qf.reshape(DB, DT, W)),
                              _channel_major(kc.reshape(DB, DT, W), NEW_KEYS),
                              _channel_major(vc.reshape(DB, DT, W), NEW_KEYS), zc.reshape(DB, DT, W),
                              _pool_channel_major(cache_k_moba[j]), _pool_channel_major(cache_v_moba[j]),
                              PAGES_PER_STEP).reshape(DB * DT, W)
            tmaj = lambda a: jnp.transpose(a.reshape(DB, DT, W), (1, 0, 2))
            yd, h_s = _rglru_sample(tmaj(xd), tmaj(zd), jnp.transpose(state_conv_rglru[j].astype(F32), (1, 0, 2)),
                                    state_h_rglru[j].astype(F32), *lru)
            yd = jnp.transpose(yd, (1, 0, 2)).reshape(DB * DT, W)
            conv_s = jnp.concatenate([state_conv_rglru[j].astype(F32), xd.reshape(DB, DT, W)],
                                     axis=1)[:, -(CONV_WIDTH - 1):]
            xs = _out_proj(yc, yd, xs, ga_s, w_out_odd[j], final_g, DB * DT, last)
            s_odd.append((kc.reshape(DB, DT, heads, HEAD_DIM), vc.reshape(DB, DT, heads, HEAD_DIM), conv_s, h_s))

    y_prompt = xp.reshape(B, T, D)
    y_sample = xs.reshape(DB, DT, D)
    stack = lambda group: [jnp.stack(a) for a in zip(*group)]
    pk_sb, pv_sb, pconv_m, pc_m, pn_m, pm_m = stack(p_even)
    sk_sb, sv_sb, sconv_m, sc_m, sn_m, sm_m = stack(s_even)
    pk_mb, pv_mb, pconv_d, ph_d = stack(p_odd)
    sk_mb, sv_mb, sconv_d, sh_d = stack(s_odd)
    return (y_prompt, y_sample,
            pk_sb, pv_sb, pconv_m, pc_m, pn_m, pm_m, pk_mb, pv_mb, pconv_d, ph_d,
            sk_sb, sv_sb, sconv_m, sc_m, sn_m, sm_m, sk_mb, sv_mb, sconv_d, sh_d)
```

```python
import functools

import jax
import jax.numpy as jnp
from jax import lax
from jax.experimental import pallas as pl
from jax.experimental.pallas import tpu as pltpu

F32 = jnp.float32
BF16 = jnp.bfloat16
HIGHEST = lax.Precision.HIGHEST

HEAD_DIM = 64
HALF_WIDTH = 512
ML_HEADS = 4
ML_HEAD_DIM = 128
CONV_WIDTH = 4
MOBA_BLOCK = 256
MOBA_TOPK = 3
RG_C = 8.0
ROPE_THETA = 10000.0
NORM_EPS = 1e-6
LANES = 128
SUBLANES = 8
VMEM_LIMIT = 56 * 1024 * 1024
NEG_INF = float("-inf")
SB_EXP_UNDERFLOW = -110.0
SB_BOUND_SLACK = 1.01
MOBA_MASK_PENALTY = -1e30
MOBA_GROUP = 4


def _params(sem):
    return pltpu.CompilerParams(dimension_semantics=sem, vmem_limit_bytes=VMEM_LIMIT)


def _sigmoid(x):
    return 1.0 / (1.0 + jnp.exp(-x))


def _silu(x):
    return x * _sigmoid(x)


def _softplus(x):
    return jnp.maximum(x, 0.0) + jnp.log1p(jnp.exp(-jnp.abs(x)))


def _softplus_scores(x):
    return jnp.maximum(x, 0.0) + jnp.log(1.0 + jnp.exp(-jnp.abs(x)))


def _log_sigmoid(x):
    return -_softplus(-x)


def _dot(a, b, precision=None):
    return jnp.dot(a, b, preferred_element_type=F32, precision=precision)


def _dot_nt(a, b, precision=None):
    return lax.dot_general(a, b, (((1,), (1,)), ((), ())), preferred_element_type=F32, precision=precision)


def _dot_tn(a, b, precision=None):
    return lax.dot_general(a, b, (((0,), (0,)), ((), ())), preferred_element_type=F32, precision=precision)


def _split_bf16(x):
    hi = x.astype(BF16)
    lo = (x - hi.astype(F32)).astype(BF16)
    return hi, lo


def _iota(shape, dim):
    return lax.broadcasted_iota(jnp.int32, shape, dim)


def _mod_kernel(c_ref, w_ref, b_ref, o_ref):
    o_ref[...] = _dot(_silu(c_ref[...]), w_ref[...], HIGHEST) + b_ref[...]


def _modulation(c, w, b):
    rows, d = c.shape
    n = w.shape[1]
    tn = 512
    return pl.pallas_call(
        _mod_kernel,
        grid=(n // tn,),
        in_specs=[pl.BlockSpec((rows, d), lambda j: (0, 0)),
                  pl.BlockSpec((d, tn), lambda j: (0, j)),
                  pl.BlockSpec((1, tn), lambda j: (0, j))],
        out_specs=pl.BlockSpec((rows, tn), lambda j: (0, j)),
        out_shape=jax.ShapeDtypeStruct((rows, n), F32),
        compiler_params=_params(("arbitrary",)),
        name="modulation",
    )(c, w, b.reshape(1, n))


def _normed(x_ref, shift_ref, scale_ref, g_ref):
    x = x_ref[...]
    ms = jnp.mean(x * x, axis=-1, keepdims=True)
    h = x * lax.rsqrt(ms + NORM_EPS) * g_ref[...]
    return h * (1.0 + scale_ref[0]) + shift_ref[0]


def _in_even_kernel(x_ref, shift_ref, scale_ref, g_ref, w_ref, wg_hi_ref, wg_lo_ref,
                    qa_ref, ka_ref, kab_ref, va_ref, vab_ref, za_ref, qk_ref, vb_ref, gb_ref, gt_ref):
    h = _normed(x_ref, shift_ref, scale_ref, g_ref)
    hb, h_lo = _split_bf16(h)
    W = HALF_WIDTH

    def proj(c):
        return _dot(hb, w_ref[:, c * W:(c + 1) * W])

    qa_ref[...] = (proj(0) * (HEAD_DIM ** -0.5)).astype(BF16)
    ka = proj(1)
    ka_ref[...] = ka
    kab_ref[...] = ka.astype(BF16)
    va = proj(2)
    va_ref[...] = va
    vab_ref[...] = va.astype(BF16)
    za_ref[...] = _silu(proj(3))
    qk_ref[:, 0:W] = proj(4)
    qk_ref[:, W:2 * W] = proj(5)
    vb_ref[...] = proj(6)
    gb_ref[...] = _sigmoid(proj(7)) * _silu(proj(8))
    gt_ref[0] = _dot(hb, wg_hi_ref[...]) + _dot(h_lo, wg_hi_ref[...]) + _dot(hb, wg_lo_ref[...])


def _in_proj_even(x, shift, scale, g, w_in, tm):
    rows, d = x.shape
    W = HALF_WIDTH
    nt = rows // tm
    groups = shift.shape[0]
    tiles_per_group = nt // groups
    w_main = w_in[:, :9 * W].astype(BF16)
    w_gate_hi, w_gate_lo = _split_bf16(jnp.pad(w_in[:, 9 * W:], ((0, 0), (0, LANES - 2 * ML_HEADS))))
    row_blk = lambda n, dt: (pl.BlockSpec((tm, n), lambda i: (i, 0)), jax.ShapeDtypeStruct((rows, n), dt))
    outs = [row_blk(W, BF16), row_blk(W, F32), row_blk(W, BF16), row_blk(W, F32), row_blk(W, BF16),
            row_blk(W, F32), row_blk(2 * W, F32), row_blk(W, F32), row_blk(W, F32),
            (pl.BlockSpec((1, tm, LANES), lambda i: (i, 0, 0)), jax.ShapeDtypeStruct((nt, tm, LANES), F32))]
    rg = shift.shape[1]
    return pl.pallas_call(
        _in_even_kernel,
        grid=(nt,),
        in_specs=[pl.BlockSpec((tm, d), lambda i: (i, 0)),
                  pl.BlockSpec((1, rg, d), lambda i: (i // tiles_per_group, 0, 0)),
                  pl.BlockSpec((1, rg, d), lambda i: (i // tiles_per_group, 0, 0)),
                  pl.BlockSpec((1, d), lambda i: (0, 0)),
                  pl.BlockSpec((d, 9 * W), lambda i: (0, 0)),
                  pl.BlockSpec((d, LANES), lambda i: (0, 0)),
                  pl.BlockSpec((d, LANES), lambda i: (0, 0))],
        out_specs=[o[0] for o in outs],
        out_shape=[o[1] for o in outs],
        compiler_params=_params(("arbitrary",)),
        name="in_proj_even",
    )(x, shift, scale, g.reshape(1, d), w_main, w_gate_hi, w_gate_lo)


def _in_odd_kernel(x_ref, shift_ref, scale_ref, g_ref, w_ref, cos_ref, sin_ref,
                   qb_ref, qf_ref, kc_ref, kcb_ref, vc_ref, vcb_ref, zc_ref, xd_ref, zd_ref, km_ref):
    h = _normed(x_ref, shift_ref, scale_ref, g_ref)
    hb = h.astype(BF16)
    W = HALF_WIDTH

    def proj(c):
        return _dot(hb, w_ref[:, c * W:(c + 1) * W])

    cos = cos_ref[...]
    sin = sin_ref[...]
    first_half = (_iota(cos.shape, 1) % HEAD_DIM) < (HEAD_DIM // 2)

    def rope(x):
        partner = jnp.where(first_half, pltpu.roll(x, W - HEAD_DIM // 2, 1), pltpu.roll(x, HEAD_DIM // 2, 1))
        return x * cos + partner * sin

    q = rope(proj(0))
    qf_ref[...] = q
    qb_ref[...] = (q * (HEAD_DIM ** -0.5)).astype(BF16)
    k = rope(proj(1))
    kc_ref[...] = k
    kcb_ref[...] = k.astype(BF16)
    km_ref[0] = jnp.mean(k, axis=0, keepdims=True)
    v = proj(2)
    vc_ref[...] = v
    vcb_ref[...] = v.astype(BF16)
    zc_ref[...] = _silu(proj(3))
    xd_ref[...] = proj(4)
    zd_ref[...] = _silu(proj(5))


def _in_proj_odd(x, shift, scale, g, w_in, cos, sin, tm):
    rows, d = x.shape
    W = HALF_WIDTH
    nt = rows // tm
    groups = shift.shape[0]
    tiles_per_group = nt // groups
    rg = shift.shape[1]
    row_blk = lambda n, dt: (pl.BlockSpec((tm, n), lambda i: (i, 0)), jax.ShapeDtypeStruct((rows, n), dt))
    outs = [row_blk(W, BF16), row_blk(W, F32), row_blk(W, F32), row_blk(W, BF16), row_blk(W, F32),
            row_blk(W, BF16), row_blk(W, F32), row_blk(W, F32), row_blk(W, F32),
            (pl.BlockSpec((1, 1, W), lambda i: (i, 0, 0)), jax.ShapeDtypeStruct((nt, 1, W), F32))]
    return pl.pallas_call(
        _in_odd_kernel,
        grid=(nt,),
        in_specs=[pl.BlockSpec((tm, d), lambda i: (i, 0)),
                  pl.BlockSpec((1, rg, d), lambda i: (i // tiles_per_group, 0, 0)),
                  pl.BlockSpec((1, rg, d), lambda i: (i // tiles_per_group, 0, 0)),
                  pl.BlockSpec((1, d), lambda i: (0, 0)),
                  pl.BlockSpec((d, 6 * W), lambda i: (0, 0)),
                  pl.BlockSpec((tm, W), lambda i: (i % tiles_per_group, 0)),
                  pl.BlockSpec((tm, W), lambda i: (i % tiles_per_group, 0))],
        out_specs=[o[0] for o in outs],
        out_shape=[o[1] for o in outs],
        compiler_params=_params(("arbitrary",)),
        name="in_proj_odd",
    )(x, shift, scale, g.reshape(1, d), w_in.astype(BF16), cos, sin)


def _rope_tables(positions):
    half = HEAD_DIM // 2
    freqs = ROPE_THETA ** (-jnp.arange(half, dtype=F32) / half)
    ang = positions.astype(F32)[:, None] * freqs[None, :]
    cos = jnp.cos(ang)
    sin = jnp.sin(ang)
    heads = HALF_WIDTH // HEAD_DIM
    cos_t = jnp.tile(jnp.concatenate([cos, cos], axis=-1), (1, heads))
    sin_t = jnp.tile(jnp.concatenate([-sin, sin], axis=-1), (1, heads))
    return cos_t, sin_t


def _out_proj_kernel(ya_ref, yb_ref, x_ref, gate_ref, w_ref, fg_ref, o_ref, *, final_norm):
    W = HALF_WIDTH
    out = _dot(ya_ref[...].astype(BF16), w_ref[0:W, :]) + _dot(yb_ref[...].astype(BF16), w_ref[W:2 * W, :])
    y = x_ref[...] + gate_ref[0] * out
    if final_norm:
        ms = jnp.mean(y * y, axis=-1, keepdims=True)
        y = y * lax.rsqrt(ms + NORM_EPS) * fg_ref[...]
    o_ref[...] = y


def _out_proj(ya, yb, x, gate, w_out, final_g, tm, final_norm):
    rows, d = x.shape
    W = HALF_WIDTH
    nt = rows // tm
    groups = gate.shape[0]
    tiles_per_group = nt // groups
    rg = gate.shape[1]
    return pl.pallas_call(
        functools.partial(_out_proj_kernel, final_norm=final_norm),
        grid=(nt,),
        in_specs=[pl.BlockSpec((tm, W), lambda i: (i, 0)),
                  pl.BlockSpec((tm, W), lambda i: (i, 0)),
                  pl.BlockSpec((tm, d), lambda i: (i, 0)),
                  pl.BlockSpec((1, rg, d), lambda i: (i // tiles_per_group, 0, 0)),
                  pl.BlockSpec((2 * W, d), lambda i: (0, 0)),
                  pl.BlockSpec((1, d), lambda i: (0, 0))],
        out_specs=pl.BlockSpec((tm, d), lambda i: (i, 0)),
        out_shape=jax.ShapeDtypeStruct((rows, d), F32),
        compiler_params=_params(("arbitrary",)),
        name="out_proj",
    )(ya, yb, x, gate, w_out.astype(BF16), final_g.reshape(1, d))


def _sb_prompt_kernel(q_ref, k_ref, v_ref, zs_ref, o_ref, knorm_ref, *, tq):
    qi = pl.program_id(2)
    q = q_ref[...]
    lane = _iota((tq, LANES), 1)
    head0 = lane < HEAD_DIM
    qs = (jnp.where(head0, q, jnp.zeros_like(q)), jnp.where(head0, jnp.zeros_like(q), q))

    def head_sq_norms(x):
        sq = x.astype(F32) * x.astype(F32)
        return (jnp.sum(jnp.where(head0, sq, 0.0), axis=1, keepdims=True),
                jnp.sum(jnp.where(head0, 0.0, sq), axis=1, keepdims=True))

    @pl.when(qi == 0)
    def _():
        def scan(j, m):
            n0, n1 = head_sq_norms(k_ref[pl.ds(pl.multiple_of(j * tq, tq), tq), :])
            return (jnp.maximum(m[0], jnp.max(n0, axis=0, keepdims=True)),
                    jnp.maximum(m[1], jnp.max(n1, axis=0, keepdims=True)))
        zero = jnp.zeros((1, 1), F32)
        m0, m1 = lax.fori_loop(0, k_ref.shape[0] // tq, scan, (zero, zero))
        knorm_ref[...] = jnp.where(_iota((1, LANES), 1) == 0, m0, m1)

    qn0, qn1 = head_sq_norms(q)
    zb0 = jnp.sqrt(qn0 * knorm_ref[0:1, 0:1]) * SB_BOUND_SLACK
    zb1 = jnp.sqrt(qn1 * knorm_ref[0:1, 1:2]) * SB_BOUND_SLACK
    row = _iota((tq, tq), 0)
    col = _iota((tq, tq), 1)
    suffix = jnp.where(row >= col, 1.0, 0.0).astype(BF16)
    causal = col < row

    def blocks(items, carry):
        acc, c0, c1 = carry
        loaded, parts = [], []
        for j, diag, valid in items:
            start = pl.multiple_of(j * tq, tq)
            kb = k_ref[pl.ds(start, tq), :]
            loaded.append((v_ref[pl.ds(start, tq), :], diag, valid))
            per_head = []
            for qh in qs:
                z = _dot_nt(qh, kb)
                ls = -_softplus_scores(z)
                if diag:
                    ls = jnp.where(causal, ls, 0.0)
                hi, lo = _split_bf16(ls)
                per_head.append((z, _dot(hi, suffix) + _dot(lo, suffix)))
            parts.append(per_head)
        cs = [c0, c1]
        for (vb, diag, valid), per_head in zip(loaded, parts):
            pvs = []
            for h, (z, s) in enumerate(per_head):
                w = jnp.exp(z + s + cs[h])
                if diag:
                    w = jnp.where(causal, w, 0.0)
                total = s[:, 0:1]
                if valid is not None:
                    w = jnp.where(valid, w, 0.0)
                    total = jnp.where(valid, total, 0.0)
                pvs.append(_dot(w.astype(BF16), vb))
                cs[h] = cs[h] + total
            acc = acc + jnp.where(head0, pvs[0], pvs[1])
        return acc, cs[0], cs[1]

    def exhausted(c0, c1):
        worst = jnp.max(jnp.maximum(zb0 + c0, zb1 + c1))
        return (worst < SB_EXP_UNDERFLOW).astype(jnp.int32)

    def step(state):
        j, _, acc, c0, c1 = state
        acc, c0, c1 = blocks([(j, False, None)], (acc, c0, c1))
        return j - 1, exhausted(c0, c1), acc, c0, c1

    zero_c = jnp.zeros((tq, 1), F32)
    acc, c0, c1 = blocks([(qi, True, None), (jnp.maximum(qi - 1, 0), False, qi > 0)],
                         (jnp.zeros((tq, LANES), F32), zero_c, zero_c))
    state = lax.while_loop(lambda s: jnp.logical_and(s[0] >= 0, s[1] == 0), step,
                           (qi - 2, exhausted(c0, c1), acc, c0, c1))
    o_ref[...] = state[2] * zs_ref[...]


def _sb_prompt(q, k, v, zs, batch, seq, tq):
    rows, W = q.shape
    nq = seq // tq
    pairs = W // LANES
    return pl.pallas_call(
        functools.partial(_sb_prompt_kernel, tq=tq),
        grid=(batch, pairs, nq),
        in_specs=[pl.BlockSpec((tq, LANES), lambda b, p, i: (b * nq + i, p)),
                  pl.BlockSpec((seq, LANES), lambda b, p, i: (b, p)),
                  pl.BlockSpec((seq, LANES), lambda b, p, i: (b, p)),
                  pl.BlockSpec((tq, LANES), lambda b, p, i: (b * nq + i, p))],
        out_specs=pl.BlockSpec((tq, LANES), lambda b, p, i: (b * nq + i, p)),
        out_shape=jax.ShapeDtypeStruct((rows, W), F32),
        scratch_shapes=[pltpu.VMEM((1, LANES), F32)],
        compiler_params=_params(("arbitrary", "arbitrary", "arbitrary")),
        name="sb_prompt",
    )(q, k, v, zs)


def _moba_prompt_kernel(q_ref, qf_ref, k_ref, v_ref, km_ref, zs_ref, o_ref, kpad_ref, *, tq, nb):
    qi = pl.program_id(2)

    @pl.when(qi == 0)
    def _():
        kpad_ref[...] = jnp.zeros_like(kpad_ref)
        kpad_ref[0:nb, :] = km_ref[0]

    q = q_ref[...]
    qf = qf_ref[...]
    lane = _iota((tq, LANES), 1)
    lane_f = lane.astype(F32)
    head0 = lane < HEAD_DIM
    row = _iota((tq, tq), 0)
    col = _iota((tq, tq), 1)
    kmean = kpad_ref[...]

    qs, sels = [], []
    for h in range(2):
        hm = head0 if h == 0 else jnp.logical_not(head0)
        qs.append(jnp.where(hm, q, jnp.zeros_like(q)))
        g = _dot_nt(jnp.where(hm, qf, 0.0), kmean, HIGHEST)
        g = jnp.where(lane < qi, g, NEG_INF)
        sel = jnp.zeros((tq, LANES), F32)
        for _ in range(MOBA_TOPK):
            mx = jnp.max(g, axis=1, keepdims=True)
            is_max = jnp.logical_and(g == mx, mx > NEG_INF)
            idx = jnp.min(jnp.where(is_max, lane_f, float(LANES)), axis=1, keepdims=True)
            pick = lane_f == idx
            sel = jnp.where(pick, 1.0, sel)
            g = jnp.where(pick, NEG_INF, g)
        sels.append(sel)

    flags =[jnp.where(sels[h] > 0.0, 0.0, 1.0) for h in range(2)]
    lhs = [jnp.where(head0, qs[0], pltpu.roll(flags[0], HEAD_DIM, 1).astype(BF16)),
           jnp.where(head0, flags[1].astype(BF16), qs[1])]
    ones = jnp.ones((tq, LANES), BF16)

    def values(vb):
        return jnp.where(head0, vb, ones), jnp.where(head0, ones, vb)

    start = pl.multiple_of(qi * tq, tq)
    kb = k_ref[pl.ds(start, tq), :]
    vmods = values(v_ref[pl.ds(start, tq), :])
    state = []
    for h in range(2):
        s = jnp.where(col <= row, _dot_nt(qs[h], kb), NEG_INF)
        m = jnp.max(s, axis=1, keepdims=True)
        p = jnp.exp(s - m)
        state += [m, _dot(p.astype(BF16), vmods[h])]

    def blocks(ns, st):
        ks, vs = [], []
        for n in ns:
            start_n = pl.multiple_of(n * tq, tq)
            kn = k_ref[pl.ds(start_n, tq), :]
            pen = [jnp.where(lane == n + off, MOBA_MASK_PENALTY, 0.0).astype(BF16) for off in (HEAD_DIM, 0)]
            ks.append((jnp.where(head0, kn, pen[0]), jnp.where(head0, pen[1], kn)))
            vs.append(values(v_ref[pl.ds(start_n, tq), :]))
        ss = [[_dot_nt(lhs[h], kn[h]) for kn in ks] for h in range(2)]
        new = []
        for h in range(2):
            m, acc = st[2 * h:2 * h + 2]
            m_new = m
            for s in ss[h]:
                m_new = jnp.maximum(m_new, jnp.max(s, axis=1, keepdims=True))
            ps = [jnp.exp(s - m_new).astype(BF16) for s in ss[h]]
            acc = jnp.exp(m - m_new) * acc
            for p, vn in zip(ps, vs):
                acc = acc + _dot(p, vn[h])
            new += [m_new, acc]
        return tuple(new)

    group = MOBA_GROUP if nb % MOBA_GROUP == 0 else 1
    st = lax.fori_loop(0, (qi + group - 1) // group,
                       lambda i, s: blocks([group * i + r for r in range(group)], s), tuple(state))
    o = jnp.where(head0, st[1] / pltpu.roll(st[1], HEAD_DIM, 1), st[3] / pltpu.roll(st[3], HEAD_DIM, 1))
    o_ref[...] = o * zs_ref[...]


def _moba_prompt(q, qf, k, v, kmean, zs, batch, seq):
    rows, W = q.shape
    tq = MOBA_BLOCK
    nq = seq // tq
    pairs = W // LANES
    assert nq <= HEAD_DIM
    return pl.pallas_call(
        functools.partial(_moba_prompt_kernel, tq=tq, nb=nq),
        grid=(batch, pairs, nq),
        in_specs=[pl.BlockSpec((tq, LANES), lambda b, p, i: (b * nq + i, p)),
                  pl.BlockSpec((tq, LANES), lambda b, p, i: (b * nq + i, p)),
                  pl.BlockSpec((seq, LANES), lambda b, p, i: (b, p)),
                  pl.BlockSpec((seq, LANES), lambda b, p, i: (b, p)),
                  pl.BlockSpec((1, nq, LANES), lambda b, p, i: (b, 0, p)),
                  pl.BlockSpec((tq, LANES), lambda b, p, i: (b * nq + i, p))],
        out_specs=pl.BlockSpec((tq, LANES), lambda b, p, i: (b * nq + i, p)),
        out_shape=jax.ShapeDtypeStruct((rows, W), F32),
        scratch_shapes=[pltpu.VMEM((LANES, LANES), F32)],
        compiler_params=_params(("arbitrary", "arbitrary", "arbitrary")),
        name="moba_prompt",
    )(q, qf, k, v, kmean.reshape(batch, nq, W), zs)


def _mlstm_kernel(qk_ref, v_ref, gb_ref, gt_ref, conv0_ref, c0_ref, m0_ref,
                  cw_ref, cb_ref, brow_ref,
                  y_ref, conv_out_ref, c_out_ref, m_out_ref,
                  xpad_ref, caug_ref, m_ref, *, L):
    c = pl.program_id(1)
    nc = pl.num_programs(1)
    H, DK = ML_HEADS, ML_HEAD_DIM
    W = HALF_WIDTH
    PADR = SUBLANES

    @pl.when(c == 0)
    def _():
        xpad_ref[0:PADR, :] = conv0_ref[0]
        caug_ref[...] = c0_ref[0]
        m_ref[...] = m0_ref[0]

    xpad_ref[PADR:PADR + L, :] = qk_ref[...]
    y = cb_ref[...]
    for j in range(CONV_WIDTH):
        off = PADR - (CONV_WIDTH - 1) + j
        y = y + cw_ref[j:j + 1, :] * xpad_ref[off:off + L, :]
    tail = xpad_ref[L:L + PADR, :]
    xpad_ref[0:PADR, :] = tail
    conv_out_ref[0] = tail
    qk = _silu(y)

    lane = _iota((L, LANES), 1)
    is_f_col = jnp.logical_and(lane >= H, lane < 2 * H)
    gcol = gt_ref[0] + brow_ref[...]
    lf_col = jnp.where(is_f_col, _log_sigmoid(gcol), 0.0)
    grow = jnp.transpose(gcol)[0:2 * H, :]
    sub = _iota((2 * H, L), 0)
    lf_row = jnp.where(sub >= H, _log_sigmoid(grow), 0.0)
    row = _iota((L, L), 0)
    col = _iota((L, L), 1)
    causal = col <= row
    tri = jnp.where(causal, 1.0, 0.0)
    f_col = _dot(tri, lf_col, HIGHEST)
    f_row = _dot_nt(lf_row, tri, HIGHEST)
    ones_col = jnp.where(_iota((L, DK), 1) == 0, 1.0, 0.0).astype(BF16)
    m_all = m_ref[...]
    m_next = m_all
    lane1 = _iota((1, LANES), 1)

    for h in range(H):
        fc = f_col[:, H + h:H + h + 1]
        fr = f_row[H + h:H + h + 1, :]
        li_c = gcol[:, h:h + 1]
        li_r = grow[h:h + 1, :]
        m_prev = m_all[:, h:h + 1]
        inter = fc + m_prev
        intra = jnp.where(causal, fc - fr + li_r, NEG_INF)
        mt = jnp.maximum(inter, jnp.max(intra, axis=1, keepdims=True))
        w = jnp.exp(intra - mt)
        g = jnp.exp(inter - mt)
        qh = qk[:, h * DK:(h + 1) * DK].astype(BF16)
        kf = qk[:, W + h * DK:W + (h + 1) * DK] * (DK ** -0.5)
        kh = kf.astype(BF16)
        vaug = jnp.concatenate([v_ref[:, h * DK:(h + 1) * DK].astype(BF16), ones_col], axis=1)
        s = _dot_nt(qh, kh) * w
        nd = g * _dot(qh, caug_ref[h].astype(BF16)) + _dot(s.astype(BF16), vaug)
        den = nd[:, DK:DK + 1]
        hout = nd[:, 0:DK] / jnp.maximum(jnp.abs(den), jnp.exp(-mt))
        y_ref[:, h * DK:(h + 1) * DK] = gb_ref[:, h * DK:(h + 1) * DK] * hout
        m_new = mt[L - 1:L, :]
        f_last = fc[L - 1:L, :]
        decay = jnp.exp(f_last + m_prev - m_new)
        ws = jnp.exp(f_last - fc + li_c - m_new)
        caug_ref[h] = decay * caug_ref[h] + _dot_tn((kf * ws).astype(BF16), vaug)
        m_next = jnp.where(lane1 == h, m_new, m_next)

    m_ref[...] = m_next

    @pl.when(c == nc - 1)
    def _():
        c_out_ref[0] = caug_ref[...]
        m_out_ref[0] = m_next


def _mlstm(qk_pre, vb, gb, gates, conv0, c0aug, m0, conv_w, conv_b, b_ig, b_fg, batch, L):
    rows, W2 = qk_pre.shape
    W = HALF_WIDTH
    H = ML_HEADS
    nc = rows // batch // L
    brow = jnp.pad(jnp.concatenate([b_ig, b_fg]).reshape(1, 2 * H), ((0, 0), (0, LANES - 2 * H)))
    out_shapes = [jax.ShapeDtypeStruct((rows, W), F32),
                  jax.ShapeDtypeStruct((batch, SUBLANES, W2), F32),
                  jax.ShapeDtypeStruct((batch, H, ML_HEAD_DIM, 2 * ML_HEAD_DIM), F32),
                  jax.ShapeDtypeStruct((batch, 1, LANES), F32)]
    return pl.pallas_call(
        functools.partial(_mlstm_kernel, L=L),
        grid=(batch, nc),
        in_specs=[pl.BlockSpec((L, W2), lambda b, c: (b * nc + c, 0)),
                  pl.BlockSpec((L, W), lambda b, c: (b * nc + c, 0)),
                  pl.BlockSpec((L, W), lambda b, c: (b * nc + c, 0)),
                  pl.BlockSpec((1, L, LANES), lambda b, c: (b * nc + c, 0, 0)),
                  pl.BlockSpec((1, SUBLANES, W2), lambda b, c: (b, 0, 0)),
                  pl.BlockSpec((1, H, ML_HEAD_DIM, 2 * ML_HEAD_DIM), lambda b, c: (b, 0, 0, 0)),
                  pl.BlockSpec((1, 1, LANES), lambda b, c: (b, 0, 0)),
                  pl.BlockSpec((CONV_WIDTH, W2), lambda b, c: (0, 0)),
                  pl.BlockSpec((1, W2), lambda b, c: (0, 0)),
                  pl.BlockSpec((1, LANES), lambda b, c: (0, 0))],
        out_specs=[pl.BlockSpec((L, W), lambda b, c: (b * nc + c, 0)),
                   pl.BlockSpec((1, SUBLANES, W2), lambda b, c: (b, 0, 0)),
                   pl.BlockSpec((1, H, ML_HEAD_DIM, 2 * ML_HEAD_DIM), lambda b, c: (b, 0, 0, 0)),
                   pl.BlockSpec((1, 1, LANES), lambda b, c: (b, 0, 0))],
        out_shape=out_shapes,
        scratch_shapes=[pltpu.VMEM((L + SUBLANES, W2), F32),
                        pltpu.VMEM((H, ML_HEAD_DIM, 2 * ML_HEAD_DIM), F32),
                        pltpu.VMEM((1, LANES), F32)],
        compiler_params=_params(("arbitrary", "arbitrary")),
        name="mlstm",
    )(qk_pre, vb, gb, gates, conv0, c0aug, m0, conv_w, conv_b.reshape(1, W2), brow)


def _rglru_gates(xc, wr_ref, br_ref, wi_ref, bi_ref, lam_ref):
    xb = xc.astype(BF16)
    r = _sigmoid(_dot(xb, wr_ref[...]) + br_ref[...])
    i = _sigmoid(_dot(xb, wi_ref[...]) + bi_ref[...])
    log_a = RG_C * r * _log_sigmoid(lam_ref[...])
    a = jnp.exp(log_a)
    b = jnp.sqrt(-jnp.tanh(log_a) * (a * a + 1.0)) * (i * xc)
    return a, b


def _rglru_prompt_kernel(xd_ref, zs_ref, conv0_ref, h0_ref, cw_ref, cb_ref, wr_ref, br_ref, wi_ref, bi_ref,
                         lam_ref, y_ref, conv_out_ref, h_out_ref,
                         xpad_ref, a_ref, b_ref, hs_ref, hc_ref, *, L):
    c = pl.program_id(1)
    PADR = SUBLANES

    @pl.when(c == 0)
    def _():
        xpad_ref[0:PADR, :] = conv0_ref[0]
        hc_ref[...] = h0_ref[0]

    xpad_ref[PADR:PADR + L, :] = xd_ref[...]
    xc = cb_ref[...]
    for j in range(CONV_WIDTH):
        off = PADR - (CONV_WIDTH - 1) + j
        xc = xc + cw_ref[j:j + 1, :] * xpad_ref[off:off + L, :]
    tail = xpad_ref[L:L + PADR, :]
    xpad_ref[0:PADR, :] = tail
    conv_out_ref[0] = tail

    a, b = _rglru_gates(xc, wr_ref, br_ref, wi_ref, bi_ref, lam_ref)
    a_ref[...] = a
    b_ref[...] = b

    def step(t, h):
        h = a_ref[pl.ds(t, 1), :] * h + b_ref[pl.ds(t, 1), :]
        hs_ref[pl.ds(t, 1), :] = h
        return h

    h = lax.fori_loop(0, L, step, hc_ref[...], unroll=8)
    hc_ref[...] = h
    h_out_ref[0] = h
    y_ref[...] = hs_ref[...] * zs_ref[...]


def _block_diag(w):
    g, n, _ = w.shape
    eye = jnp.eye(g, dtype=w.dtype)
    return (eye[:, None, :, None] * w[:, :, None, :]).reshape(g * n, g * n)


def _rglru_prompt(xd, zs, conv0, h0, conv_w, conv_b, wr, br, wi, bi, lam, batch, L):
    rows, W = xd.shape
    nc = rows // batch // L
    vec = lambda a: a.reshape(1, W)
    const = lambda shape: pl.BlockSpec(shape, lambda b, c: tuple(0 for _ in shape))
    return pl.pallas_call(
        functools.partial(_rglru_prompt_kernel, L=L),
        grid=(batch, nc),
        in_specs=[pl.BlockSpec((L, W), lambda b, c: (b * nc + c, 0)),
                  pl.BlockSpec((L, W), lambda b, c: (b * nc + c, 0)),
                  pl.BlockSpec((1, SUBLANES, W), lambda b, c: (b, 0, 0)),
                  pl.BlockSpec((1, 1, W), lambda b, c: (b, 0, 0)),
                  const((CONV_WIDTH, W)), const((1, W)), const((W, W)), const((1, W)), const((W, W)),
                  const((1, W)), const((1, W))],
        out_specs=[pl.BlockSpec((L, W), lambda b, c: (b * nc + c, 0)),
                   pl.BlockSpec((1, SUBLANES, W), lambda b, c: (b, 0, 0)),
                   pl.BlockSpec((1, 1, W), lambda b, c: (b, 0, 0))],
        out_shape=[jax.ShapeDtypeStruct((rows, W), F32),
                   jax.ShapeDtypeStruct((batch, SUBLANES, W), F32),
                   jax.ShapeDtypeStruct((batch, 1, W), F32)],
        scratch_shapes=[pltpu.VMEM((L + SUBLANES, W), F32), pltpu.VMEM((L, W), F32), pltpu.VMEM((L, W), F32),
                        pltpu.VMEM((L, W), F32), pltpu.VMEM((1, W), F32)],
        compiler_params=_params(("arbitrary", "arbitrary")),
        name="rglru_prompt",
    )(xd, zs, conv0, h0, conv_w, vec(conv_b), _block_diag(wr).astype(BF16), vec(br),
      _block_diag(wi).astype(BF16), vec(bi), vec(lam))


def _rglru_sample_kernel(xd_ref, zs_ref, conv0_ref, h0_ref, cw_ref, cb_ref, wr_ref, br_ref, wi_ref, bi_ref,
                         lam_ref, y_ref, h_out_ref, *, T):
    xs = [conv0_ref[j] for j in range(CONV_WIDTH - 1)] + [xd_ref[t] for t in range(T)]
    h = h0_ref[...]
    for t in range(T):
        xc = cb_ref[...]
        for j in range(CONV_WIDTH):
            xc = xc + cw_ref[j:j + 1, :] * xs[t + j]
        a, b = _rglru_gates(xc, wr_ref, br_ref, wi_ref, bi_ref, lam_ref)
        h = a * h + b
        y_ref[t] = h * zs_ref[t]
    h_out_ref[...] = h


def _rglru_sample(xd, zs, conv0, h0, conv_w, conv_b, wr, br, wi, bi, lam):
    T, B, W = xd.shape
    vec = lambda a: a.reshape(1, W)
    return pl.pallas_call(
        functools.partial(_rglru_sample_kernel, T=T),
        out_shape=[jax.ShapeDtypeStruct((T, B, W), F32), jax.ShapeDtypeStruct((B, W), F32)],
        compiler_params=pltpu.CompilerParams(vmem_limit_bytes=VMEM_LIMIT),
        name="rglru_sample",
    )(xd, zs, conv0, h0, conv_w, vec(conv_b), _block_diag(wr).astype(BF16), vec(br),
      _block_diag(wi).astype(BF16), vec(bi), vec(lam))


def _query_rows(q):
    b, t, w = q.shape
    heads = w // HEAD_DIM
    onehot = (jnp.arange(heads)[:, None] == (jnp.arange(w) // HEAD_DIM)[None, :]).astype(q.dtype)
    return (q[:, :, None, :] * onehot[None, None]).reshape(b, t * heads, w)


def _channel_major(a, lanes):
    return jnp.pad(jnp.transpose(a, (0, 2, 1)), ((0, 0), (0, 0), (0, lanes - a.shape[1])))


def _pool_channel_major(pool):
    n_pool, page_rows, heads, hd = pool.shape
    return jnp.transpose(pool, (0, 2, 3, 1)).reshape(n_pool, heads * hd, page_rows)


def _page_specs(pages_per_step, page_rows, width, page_of):
    def spec(i):
        return pl.BlockSpec((None, width, page_rows), lambda b, c, pt: (pt[b, page_of(c, i)], 0, 0))
    return [spec(i) for i in range(pages_per_step)]


def _head_diagonal(o, t_new, zs):
    heads = HALF_WIDTH // HEAD_DIM
    rows = t_new * heads
    keep = (_iota((rows, HALF_WIDTH), 1) // HEAD_DIM) == (_iota((rows, HALF_WIDTH), 0) % heads)
    o = jnp.where(keep, o, 0.0)
    return jnp.sum(o.reshape(t_new, heads, HALF_WIDTH), axis=1) * zs


def _sb_sample_kernel(pt_ref, q_ref, knew_ref, vnew_ref, zs_ref, *rest, P, t_new):
    kpages, vpages = rest[:P], rest[P:2 * P]
    o_ref, acc_ref, car_ref = rest[2 * P:]
    c = pl.program_id(1)
    nch = pl.num_programs(1)
    q = q_ref[0]
    heads = HALF_WIDTH // HEAD_DIM
    rows = t_new * heads

    def key_blocks(kts, vts, mask):
        n = kts[0].shape[1]
        suffix = jnp.where(_iota((n, n), 0) >= _iota((n, n), 1), 1.0, 0.0).astype(BF16)
        zs = [_dot(q, kt.astype(BF16)) for kt in kts]
        lss = [-_softplus_scores(z) for z in zs]
        if mask is not None:
            lss = [jnp.where(mask, ls, 0.0) for ls in lss]
        sums = []
        for ls in lss:
            hi, lo = _split_bf16(ls)
            sums.append(_dot(hi, suffix) + _dot(lo, suffix))
        carry = car_ref[...]
        acc = acc_ref[...]
        for z, s, vt in zip(zs, sums, vts):
            w = jnp.exp(z + s + carry)
            if mask is not None:
                w = jnp.where(mask, w, 0.0)
            acc = acc + _dot_nt(w.astype(BF16), vt.astype(BF16))
            carry = carry + s[:, 0:1]
        acc_ref[...] = acc
        car_ref[...] = carry

    @pl.when(c == 0)
    def _():
        acc_ref[...] = jnp.zeros_like(acc_ref)
        car_ref[...] = jnp.zeros_like(car_ref)
        n = knew_ref.shape[2]
        j = _iota((rows, n), 1)
        t = _iota((rows, n), 0) // heads
        key_blocks([knew_ref[0]], [vnew_ref[0]], jnp.logical_and(j < t, j < t_new))

    key_blocks([kpages[i][...] for i in reversed(range(P))], [vpages[i][...] for i in reversed(range(P))], None)

    @pl.when(c == nch - 1)
    def _():
        o_ref[0] = _head_diagonal(acc_ref[...], t_new, zs_ref[0])


def _sb_sample(page_table, q_rows, knew_t, vnew_t, zs, k_pool, v_pool, P):
    B, n_pages = page_table.shape
    t_new = zs.shape[1]
    W = HALF_WIDTH
    rows = q_rows.shape[1]
    page_rows = k_pool.shape[2]
    nch = n_pages // P
    page_of = lambda c, i: (nch - 1 - c) * P + i
    per_b = lambda shape: pl.BlockSpec((1,) + shape, lambda b, c, pt: (b,) + tuple(0 for _ in shape))
    grid_spec = pltpu.PrefetchScalarGridSpec(
        num_scalar_prefetch=1,
        grid=(B, nch),
        in_specs=[per_b((rows, W)), per_b(knew_t.shape[1:]), per_b(vnew_t.shape[1:]), per_b((t_new, W))]
        + _page_specs(P, page_rows, W, page_of) + _page_specs(P, page_rows, W, page_of),
        out_specs=per_b((t_new, W)),
        scratch_shapes=[pltpu.VMEM((rows, W), F32), pltpu.VMEM((rows, 1), F32)],
    )
    return pl.pallas_call(
        functools.partial(_sb_sample_kernel, P=P, t_new=t_new),
        grid_spec=grid_spec,
        out_shape=jax.ShapeDtypeStruct((B, t_new, W), F32),
        compiler_params=_params(("arbitrary", "arbitrary")),
        name="sb_sample",
    )(page_table, q_rows, knew_t, vnew_t, zs, *([k_pool] * P), *([v_pool] * P))


def _moba_sample_kernel(pt_ref, q_ref, qg_ref, knew_ref, vnew_ref, zs_ref, *rest, P, t_new, nb, pb):
    kpages, vpages = rest[:P], rest[P:2 * P]
    o_ref, m_ref, l_ref, acc_ref, km_ref = rest[2 * P:]
    c = pl.program_id(1)
    nch = pl.num_programs(1)
    q = q_ref[0]
    heads = HALF_WIDTH // HEAD_DIM
    rows = t_new * heads
    lane = _iota((rows, LANES), 1)

    @pl.when(c == 0)
    def _():
        km_ref[...] = jnp.zeros_like(km_ref)
        m_ref[...] = jnp.zeros_like(m_ref)
        l_ref[...] = jnp.zeros_like(l_ref)

    nblk = P // pb
    zs = [[_dot(q, kpages[pb * s + r][...].astype(BF16)) for r in range(pb)] for s in range(nblk)]
    km_all, m_all, l_all = km_ref[...], m_ref[...], l_ref[...]
    km_lane = _iota(km_ref.shape, 1)
    for s in range(nblk):
        n = c * nblk + s
        ksum = kpages[pb * s][...]
        for r in range(1, pb):
            ksum = ksum + kpages[pb * s + r][...]
        kmean = jnp.sum(ksum, axis=1, keepdims=True) * (1.0 / MOBA_BLOCK)
        km_all = jnp.where(km_lane == n, kmean, km_all)
        m = jnp.max(zs[s][0], axis=1, keepdims=True)
        for z in zs[s][1:]:
            m = jnp.maximum(m, jnp.max(z, axis=1, keepdims=True))
        l = jnp.zeros((rows, 1), F32)
        a = jnp.zeros((rows, HALF_WIDTH), F32)
        for r, z in enumerate(zs[s]):
            p = jnp.exp(z - m)
            l = l + jnp.sum(p, axis=1, keepdims=True)
            a = a + _dot_nt(p.astype(BF16), vpages[pb * s + r][...].astype(BF16))
        m_all = jnp.where(lane == n, m, m_all)
        l_all = jnp.where(lane == n, l, l_all)
        acc_ref[n] = a
    km_ref[...] = km_all
    m_ref[...] = m_all
    l_ref[...] = l_all

    @pl.when(c == nch - 1)
    def _():
        lane_f = lane.astype(F32)
        g = _dot(qg_ref[0], km_ref[...], HIGHEST)
        g = jnp.where(lane < nb, g, NEG_INF)
        sel = jnp.zeros((rows, LANES), F32)
        for _ in range(min(MOBA_TOPK, nb)):
            mx = jnp.max(g, axis=1, keepdims=True)
            is_max = jnp.logical_and(g == mx, mx > NEG_INF)
            idx = jnp.min(jnp.where(is_max, lane_f, float(LANES)), axis=1, keepdims=True)
            pick = lane_f == idx
            sel = jnp.where(pick, 1.0, sel)
            g = jnp.where(pick, NEG_INF, g)
        nn = knew_ref.shape[2]
        j = _iota((rows, nn), 1)
        t = _iota((rows, nn), 0) // heads
        zn = jnp.where(jnp.logical_and(j <= t, j < t_new), _dot(q, knew_ref[0].astype(BF16)), NEG_INF)
        m_all = m_ref[...]
        m_tot = jnp.maximum(jnp.max(jnp.where(sel > 0.0, m_all, NEG_INF), axis=1, keepdims=True),
                            jnp.max(zn, axis=1, keepdims=True))
        coef = jnp.where(sel > 0.0, jnp.exp(m_all - m_tot), 0.0)
        p_own = jnp.exp(zn - m_tot)
        denom = jnp.sum(coef * l_ref[...], axis=1, keepdims=True) + jnp.sum(p_own, axis=1, keepdims=True)
        o = _dot_nt(p_own.astype(BF16), vnew_ref[0].astype(BF16))
        for n in range(nb):
            o = o + coef[:, n:n + 1] * acc_ref[n]
        o_ref[0] = _head_diagonal(o / denom, t_new, zs_ref[0])


def _moba_sample(page_table, q_rows, qg_rows, knew_t, vnew_t, zs, k_pool, v_pool, P):
    B, n_pages = page_table.shape
    t_new = zs.shape[1]
    W = HALF_WIDTH
    rows = q_rows.shape[1]
    page_rows = k_pool.shape[2]
    pb = MOBA_BLOCK // page_rows
    nch = n_pages // P
    nb = n_pages // pb
    assert nb <= LANES and P % pb == 0
    page_of = lambda c, i: c * P + i
    per_b = lambda shape: pl.BlockSpec((1,) + shape, lambda b, c, pt: (b,) + tuple(0 for _ in shape))
    grid_spec = pltpu.PrefetchScalarGridSpec(
        num_scalar_prefetch=1,
        grid=(B, nch),
        in_specs=[per_b((rows, W)), per_b((rows, W)), per_b(knew_t.shape[1:]), per_b(vnew_t.shape[1:]),
                  per_b((t_new, W))]
        + _page_specs(P, page_rows, W, page_of) + _page_specs(P, page_rows, W, page_of),
        out_specs=per_b((t_new, W)),
        scratch_shapes=[pltpu.VMEM((rows, LANES), F32), pltpu.VMEM((rows, LANES), F32),
                        pltpu.VMEM((nb, rows, W), F32), pltpu.VMEM((W, LANES), F32)],
    )
    return pl.pallas_call(
        functools.partial(_moba_sample_kernel, P=P, t_new=t_new, nb=nb, pb=pb),
        grid_spec=grid_spec,
        out_shape=jax.ShapeDtypeStruct((B, t_new, W), F32),
        compiler_params=_params(("arbitrary", "arbitrary")),
        name="moba_sample",
    )(page_table, q_rows, qg_rows, knew_t, vnew_t, zs, *([k_pool] * P), *([v_pool] * P))


def _pad_rows(a, rows):
    return jnp.pad(a, ((0, 0), (0, rows - a.shape[1]), (0, 0)))


def kernel(x_prompt, x_sample, c_prompt, c_sample, page_table, cache_k_sb, cache_v_sb, state_conv_mlstm, state_c_mlstm, state_n_mlstm, state_m_mlstm, cache_k_moba, cache_v_moba, state_conv_rglru, state_h_rglru, norm_g_even, mod_w_even, mod_b_even, w_in_even, conv_w_even, conv_b_even, b_igate_even, b_fgate_even, w_out_even, norm_g_odd, mod_w_odd, mod_b_odd, w_in_odd, conv_w_odd, conv_b_odd, w_rgate_odd, b_rgate_odd, w_igate_odd, b_igate_odd, lru_lambda_odd, w_out_odd, final_g):
    B, T, D = x_prompt.shape
    DB, DT, _ = x_sample.shape
    W = HALF_WIDTH
    H = ML_HEADS
    heads = W // HEAD_DIM
    n_pool, page_rows = cache_k_sb.shape[1], cache_k_sb.shape[2]
    past = page_table.shape[1] * page_rows
    depth = norm_g_even.shape[0] + norm_g_odd.shape[0]
    TM = 256
    PAGES_PER_STEP = 8
    NEW_KEYS = LANES
    ML_PAD = SUBLANES

    xp = x_prompt.reshape(B * T, D)
    xs = x_sample.reshape(DB * DT, D)
    c_all = jnp.concatenate([c_prompt, c_sample], axis=0)
    c_rows = -(-c_all.shape[0] // SUBLANES) * SUBLANES
    c_all = jnp.pad(c_all, ((0, c_rows - c_all.shape[0]), (0, 0)))
    cos_p, sin_p = _rope_tables(jnp.arange(T))
    cos_s, sin_s = _rope_tables(past + jnp.arange(DT))
    cos_s, sin_s = jnp.tile(cos_s, (DB, 1)), jnp.tile(sin_s, (DB, 1))

    def mods(w, b):
        mod = _modulation(c_all, w, b)
        parts = []
        for part in jnp.split(mod, 3, axis=-1):
            p_part = part[:B].reshape(B, 1, D)
            s_part = jnp.repeat(part[B:B + DB], DT, axis=0).reshape(1, DB * DT, D)
            parts.append((p_part, s_part))
        return parts

    p_even, s_even, p_odd, s_odd = [], [], [], []
    y_prompt = y_sample = None
    for l in range(depth):
        j = l // 2
        last = l == depth - 1
        if l % 2 == 0:
            (sh_p, sh_s), (sc_p, sc_s), (ga_p, ga_s) = mods(mod_w_even[j], mod_b_even[j])
            qa, ka, kab, va, vab, za, qk, vb, gb, gt = _in_proj_even(
                xp, sh_p, sc_p, norm_g_even[j], w_in_even[j], TM)
            ya = _sb_prompt(qa, kab, vab, za, B, T, TM)
            yb, conv_p, caug_p, m_p = _mlstm(
                qk, vb, gb, gt,
                jnp.zeros((B, SUBLANES, 2 * W), F32), jnp.zeros((B, H, ML_HEAD_DIM, 2 * ML_HEAD_DIM), F32),
                jnp.zeros((B, 1, LANES), F32),
                conv_w_even[j], conv_b_even[j], b_igate_even[j], b_fgate_even[j], B, TM)
            xp = _out_proj(ya, yb, xp, ga_p, w_out_even[j], final_g, TM, last)
            p_even.append((ka.reshape(B, T, heads, HEAD_DIM), va.reshape(B, T, heads, HEAD_DIM),
                           conv_p[:, SUBLANES - (CONV_WIDTH - 1):], caug_p[..., :ML_HEAD_DIM],
                           caug_p[..., ML_HEAD_DIM], m_p[:, 0, :H]))
            qa, ka, kab, va, vab, za, qk, vb, gb, gt = _in_proj_even(
                xs, sh_s, sc_s, norm_g_even[j], w_in_even[j], DB * DT)
            ya = _sb_sample(page_table, _query_rows(qa.reshape(DB, DT, W)),
                            _channel_major(ka.reshape(DB, DT, W), NEW_KEYS),
                            _channel_major(va.reshape(DB, DT, W), NEW_KEYS), za.reshape(DB, DT, W),
                            _pool_channel_major(cache_k_sb[j]), _pool_channel_major(cache_v_sb[j]),
                            PAGES_PER_STEP).reshape(DB * DT, W)
            pad_t = lambda a: _pad_rows(a.reshape(DB, DT, a.shape[-1]), ML_PAD).reshape(DB * ML_PAD, a.shape[-1])
            g_rows = gt.reshape(DB, DT, LANES)[:, :, :2 * H]
            pad_gate = jnp.concatenate([jnp.full((H,), NEG_INF, F32), jnp.full((H,), jnp.inf, F32)])
            g_rows = jnp.concatenate([g_rows, jnp.broadcast_to(pad_gate, (DB, ML_PAD - DT, 2 * H))], axis=1)
            gt_s = jnp.pad(g_rows, ((0, 0), (0, 0), (0, LANES - 2 * H)))
            conv0 = jnp.pad(state_conv_mlstm[j], ((0, 0), (SUBLANES - (CONV_WIDTH - 1), 0), (0, 0)))
            c0aug = jnp.concatenate(
                [state_c_mlstm[j].astype(F32), state_n_mlstm[j].astype(F32)[..., None],
                 jnp.zeros((DB, H, ML_HEAD_DIM, ML_HEAD_DIM - 1), F32)], axis=-1)
            m0 = jnp.pad(state_m_mlstm[j].astype(F32), ((0, 0), (0, LANES - H))).reshape(DB, 1, LANES)
            yb, _, caug_s, m_s = _mlstm(
                pad_t(qk), pad_t(vb), pad_t(gb), gt_s, conv0, c0aug, m0,
                conv_w_even[j], conv_b_even[j], b_igate_even[j], b_fgate_even[j], DB, ML_PAD)
            yb = yb.reshape(DB, ML_PAD, W)[:, :DT].reshape(DB * DT, W)
            conv_s = jnp.concatenate([state_conv_mlstm[j].astype(F32), qk.reshape(DB, DT, 2 * W)],
                                     axis=1)[:, -(CONV_WIDTH - 1):]
            xs = _out_proj(ya, yb, xs, ga_s, w_out_even[j], final_g, DB * DT, last)
            s_even.append((ka.reshape(DB, DT, heads, HEAD_DIM), va.reshape(DB, DT, heads, HEAD_DIM),
                           conv_s, caug_s[..., :ML_HEAD_DIM], caug_s[..., ML_HEAD_DIM], m_s[:, 0, :H]))
        else:
            (sh_p, sh_s), (sc_p, sc_s), (ga_p, ga_s) = mods(mod_w_odd[j], mod_b_odd[j])
            lru = (conv_w_odd[j], conv_b_odd[j], w_rgate_odd[j], b_rgate_odd[j], w_igate_odd[j], b_igate_odd[j],
                   lru_lambda_odd[j])
            qb, qf, kc, kcb, vc, vcb, zc, xd, zd, km = _in_proj_odd(
                xp, sh_p, sc_p, norm_g_odd[j], w_in_odd[j], cos_p, sin_p, TM)
            yc = _moba_prompt(qb, qf, kcb, vcb, km, zc, B, T)
            yd, conv_p, h_p = _rglru_prompt(xd, zd, jnp.zeros((B, SUBLANES, W), F32), jnp.zeros((B, 1, W), F32),
                                            *lru, B, TM)
            xp = _out_proj(yc, yd, xp, ga_p, w_out_odd[j], final_g, TM, last)
            p_odd.append((kc.reshape(B, T, heads, HEAD_DIM), vc.reshape(B, T, heads, HEAD_DIM),
                          conv_p[:, SUBLANES - (CONV_WIDTH - 1):], h_p[:, 0]))
            qb, qf, kc, kcb, vc, vcb, zc, xd, zd, km = _in_proj_odd(
                xs, sh_s, sc_s, norm_g_odd[j], w_in_odd[j], cos_s, sin_s, DB * DT)
            yc = _moba_sample(page_table, _query_rows(qb.reshape(DB, DT, W)), _query_rows(qf.reshape(DB, DT, W)),
                              _channel_major(kc.reshape(DB, DT, W), NEW_KEYS),
                              _channel_major(vc.reshape(DB, DT, W), NEW_KEYS), zc.reshape(DB, DT, W),
                              _pool_channel_major(cache_k_moba[j]), _pool_channel_major(cache_v_moba[j]),
                              PAGES_PER_STEP).reshape(DB * DT, W)
            tmaj = lambda a: jnp.transpose(a.reshape(DB, DT, W), (1, 0, 2))
            yd, h_s = _rglru_sample(tmaj(xd), tmaj(zd), jnp.transpose(state_conv_rglru[j].astype(F32), (1, 0, 2)),
                                    state_h_rglru[j].astype(F32), *lru)
            yd = jnp.transpose(yd, (1, 0, 2)).reshape(DB * DT, W)
            conv_s = jnp.concatenate([state_conv_rglru[j].astype(F32), xd.reshape(DB, DT, W)],
                                     axis=1)[:, -(CONV_WIDTH - 1):]
            xs = _out_proj(yc, yd, xs, ga_s, w_out_odd[j], final_g, DB * DT, last)
            s_odd.append((kc.reshape(DB, DT, heads, HEAD_DIM), vc.reshape(DB, DT, heads, HEAD_DIM), conv_s, h_s))

    y_prompt = xp.reshape(B, T, D)
    y_sample = xs.reshape(DB, DT, D)
    stack = lambda group: [jnp.stack(a) for a in zip(*group)]
    pk_sb, pv_sb, pconv_m, pc_m, pn_m, pm_m = stack(p_even)
    sk_sb, sv_sb, sconv_m, sc_m, sn_m, sm_m = stack(s_even)
    pk_mb, pv_mb, pconv_d, ph_d = stack(p_odd)
    sk_mb, sv_mb, sconv_d, sh_d = stack(s_odd)
    return (y_prompt, y_sample,
            pk_sb, pv_sb, pconv_m, pc_m, pn_m, pm_m, pk_mb, pv_mb, pconv_d, ph_d,
            sk_sb, sv_sb, sconv_m, sc_m, sn_m, sm_m, sk_mb, sv_mb, sconv_d, sh_d)
```

```python
import functools

import jax
import jax.numpy as jnp
from jax import lax
from jax.experimental import pallas as pl
from jax.experimental.pallas import tpu as pltpu

F32 = jnp.float32
BF16 = jnp.bfloat16
HIGHEST = lax.Precision.HIGHEST

HEAD_DIM = 64
HALF_WIDTH = 512
ML_HEADS = 4
ML_HEAD_DIM = 128
CONV_WIDTH = 4
MOBA_BLOCK = 256
MOBA_TOPK = 3
RG_C = 8.0
ROPE_THETA = 10000.0
NORM_EPS = 1e-6
LANES = 128
SUBLANES = 8
VMEM_LIMIT = 56 * 1024 * 1024
NEG_INF = float("-inf")
SB_EXP_UNDERFLOW = -110.0
SB_BOUND_SLACK = 1.01
MOBA_MASK_PENALTY = -1e30
MOBA_GROUP = 4


def _params(sem):
    return pltpu.CompilerParams(dimension_semantics=sem, vmem_limit_bytes=VMEM_LIMIT)


def _sigmoid(x):
    return 1.0 / (1.0 + jnp.exp(-x))


def _silu(x):
    return x * _sigmoid(x)


def _softplus(x):
    return jnp.maximum(x, 0.0) + jnp.log1p(jnp.exp(-jnp.abs(x)))


def _softplus_scores(x):
    return jnp.maximum(x, 0.0) + jnp.log(1.0 + jnp.exp(-jnp.abs(x)))


def _log_sigmoid(x):
    return -_softplus(-x)


def _dot(a, b, precision=None):
    return jnp.dot(a, b, preferred_element_type=F32, precision=precision)


def _dot_nt(a, b, precision=None):
    return lax.dot_general(a, b, (((1,), (1,)), ((), ())), preferred_element_type=F32, precision=precision)


def _dot_tn(a, b, precision=None):
    return lax.dot_general(a, b, (((0,), (0,)), ((), ())), preferred_element_type=F32, precision=precision)


def _split_bf16(x):
    hi = x.astype(BF16)
    lo = (x - hi.astype(F32)).astype(BF16)
    return hi, lo


def _iota(shape, dim):
    return lax.broadcasted_iota(jnp.int32, shape, dim)


def _mod_kernel(c_ref, w_ref, b_ref, o_ref):
    o_ref[...] = _dot(_silu(c_ref[...]), w_ref[...], HIGHEST) + b_ref[...]


def _modulation(c, w, b):
    rows, d = c.shape
    n = w.shape[1]
    tn = 512
    return pl.pallas_call(
        _mod_kernel,
        grid=(n // tn,),
        in_specs=[pl.BlockSpec((rows, d), lambda j: (0, 0)),
                  pl.BlockSpec((d, tn), lambda j: (0, j)),
                  pl.BlockSpec((1, tn), lambda j: (0, j))],
        out_specs=pl.BlockSpec((rows, tn), lambda j: (0, j)),
        out_shape=jax.ShapeDtypeStruct((rows, n), F32),
        compiler_params=_params(("arbitrary",)),
        name="modulation",
    )(c, w, b.reshape(1, n))


def _normed(x_ref, shift_ref, scale_ref, g_ref):
    x = x_ref[...]
    ms = jnp.mean(x * x, axis=-1, keepdims=True)
    h = x * lax.rsqrt(ms + NORM_EPS) * g_ref[...]
    return h * (1.0 + scale_ref[0]) + shift_ref[0]


def _in_even_kernel(x_ref, shift_ref, scale_ref, g_ref, w_ref, wg_hi_ref, wg_lo_ref,
                    qa_ref, ka_ref, kab_ref, va_ref, vab_ref, za_ref, qk_ref, vb_ref, gb_ref, gt_ref):
    h = _normed(x_ref, shift_ref, scale_ref, g_ref)
    hb, h_lo = _split_bf16(h)
    W = HALF_WIDTH

    def proj(c):
        return _dot(hb, w_ref[:, c * W:(c + 1) * W])

    qa_ref[...] = (proj(0) * (HEAD_DIM ** -0.5)).astype(BF16)
    ka = proj(1)
    ka_ref[...] = ka
    kab_ref[...] = ka.astype(BF16)
    va = proj(2)
    va_ref[...] = va
    vab_ref[...] = va.astype(BF16)
    za_ref[...] = _silu(proj(3))
    qk_ref[:, 0:W] = proj(4)
    qk_ref[:, W:2 * W] = proj(5)
    vb_ref[...] = proj(6)
    gb_ref[...] = _sigmoid(proj(7)) * _silu(proj(8))
    gt_ref[0] = _dot(hb, wg_hi_ref[...]) + _dot(h_lo, wg_hi_ref[...]) + _dot(hb, wg_lo_ref[...])


def _in_proj_even(x, shift, scale, g, w_in, tm):
    rows, d = x.shape
    W = HALF_WIDTH
    nt = rows // tm
    groups = shift.shape[0]
    tiles_per_group = nt // groups
    w_main = w_in[:, :9 * W].astype(BF16)
    w_gate_hi, w_gate_lo = _split_bf16(jnp.pad(w_in[:, 9 * W:], ((0, 0), (0, LANES - 2 * ML_HEADS))))
    row_blk = lambda n, dt: (pl.BlockSpec((tm, n), lambda i: (i, 0)), jax.ShapeDtypeStruct((rows, n), dt))
    outs = [row_blk(W, BF16), row_blk(W, F32), row_blk(W, BF16), row_blk(W, F32), row_blk(W, BF16),
            row_blk(W, F32), row_blk(2 * W, F32), row_blk(W, F32), row_blk(W, F32),
            (pl.BlockSpec((1, tm, LANES), lambda i: (i, 0, 0)), jax.ShapeDtypeStruct((nt, tm, LANES), F32))]
    rg = shift.shape[1]
    return pl.pallas_call(
        _in_even_kernel,
        grid=(nt,),
        in_specs=[pl.BlockSpec((tm, d), lambda i: (i, 0)),
                  pl.BlockSpec((1, rg, d), lambda i: (i // tiles_per_group, 0, 0)),
                  pl.BlockSpec((1, rg, d), lambda i: (i // tiles_per_group, 0, 0)),
                  pl.BlockSpec((1, d), lambda i: (0, 0)),
                  pl.BlockSpec((d, 9 * W), lambda i: (0, 0)),
                  pl.BlockSpec((d, LANES), lambda i: (0, 0)),
                  pl.BlockSpec((d, LANES), lambda i: (0, 0))],
        out_specs=[o[0] for o in outs],
        out_shape=[o[1] for o in outs],
        compiler_params=_params(("arbitrary",)),
        name="in_proj_even",
    )(x, shift, scale, g.reshape(1, d), w_main, w_gate_hi, w_gate_lo)


def _in_odd_kernel(x_ref, shift_ref, scale_ref, g_ref, w_ref, cos_ref, sin_ref,
                   qb_ref, qf_ref, kc_ref, kcb_ref, vc_ref, vcb_ref, zc_ref, xd_ref, zd_ref, km_ref):
    h = _normed(x_ref, shift_ref, scale_ref, g_ref)
    hb = h.astype(BF16)
    W = HALF_WIDTH

    def proj(c):
        return _dot(hb, w_ref[:, c * W:(c + 1) * W])

    cos = cos_ref[...]
    sin = sin_ref[...]
    first_half = (_iota(cos.shape, 1) % HEAD_DIM) < (HEAD_DIM // 2)

    def rope(x):
        partner = jnp.where(first_half, pltpu.roll(x, W - HEAD_DIM // 2, 1), pltpu.roll(x, HEAD_DIM // 2, 1))
        return x * cos + partner * sin

    q = rope(proj(0))
    qf_ref[...] = q
    qb_ref[...] = (q * (HEAD_DIM ** -0.5)).astype(BF16)
    k = rope(proj(1))
    kc_ref[...] = k
    kcb_ref[...] = k.astype(BF16)
    km_ref[0] = jnp.mean(k, axis=0, keepdims=True)
    v = proj(2)
    vc_ref[...] = v
    vcb_ref[...] = v.astype(BF16)
    zc_ref[...] = _silu(proj(3))
    xd_ref[...] = proj(4)
    zd_ref[...] = _silu(proj(5))


def _in_proj_odd(x, shift, scale, g, w_in, cos, sin, tm):
    rows, d = x.shape
    W = HALF_WIDTH
    nt = rows // tm
    groups = shift.shape[0]
    tiles_per_group = nt // groups
    rg = shift.shape[1]
    row_blk = lambda n, dt: (pl.BlockSpec((tm, n), lambda i: (i, 0)), jax.ShapeDtypeStruct((rows, n), dt))
    outs = [row_blk(W, BF16), row_blk(W, F32), row_blk(W, F32), row_blk(W, BF16), row_blk(W, F32),
            row_blk(W, BF16), row_blk(W, F32), row_blk(W, F32), row_blk(W, F32),
            (pl.BlockSpec((1, 1, W), lambda i: (i, 0, 0)), jax.ShapeDtypeStruct((nt, 1, W), F32))]
    return pl.pallas_call(
        _in_odd_kernel,
        grid=(nt,),
        in_specs=[pl.BlockSpec((tm, d), lambda i: (i, 0)),
                  pl.BlockSpec((1, rg, d), lambda i: (i // tiles_per_group, 0, 0)),
                  pl.BlockSpec((1, rg, d), lambda i: (i // tiles_per_group, 0, 0)),
                  pl.BlockSpec((1, d), lambda i: (0, 0)),
                  pl.BlockSpec((d, 6 * W), lambda i: (0, 0)),
                  pl.BlockSpec((tm, W), lambda i: (i % tiles_per_group, 0)),
                  pl.BlockSpec((tm, W), lambda i: (i % tiles_per_group, 0))],
        out_specs=[o[0] for o in outs],
        out_shape=[o[1] for o in outs],
        compiler_params=_params(("arbitrary",)),
        name="in_proj_odd",
    )(x, shift, scale, g.reshape(1, d), w_in.astype(BF16), cos, sin)


def _rope_tables(positions):
    half = HEAD_DIM // 2
    freqs = ROPE_THETA ** (-jnp.arange(half, dtype=F32) / half)
    ang = positions.astype(F32)[:, None] * freqs[None, :]
    cos = jnp.cos(ang)
    sin = jnp.sin(ang)
    heads = HALF_WIDTH // HEAD_DIM
    cos_t = jnp.tile(jnp.concatenate([cos, cos], axis=-1), (1, heads))
    sin_t = jnp.tile(jnp.concatenate([-sin, sin], axis=-1), (1, heads))
    return cos_t, sin_t


def _out_proj_kernel(ya_ref, yb_ref, x_ref, gate_ref, w_ref, fg_ref, o_ref, *, final_norm):
    W = HALF_WIDTH
    out = _dot(ya_ref[...].astype(BF16), w_ref[0:W, :]) + _dot(yb_ref[...].astype(BF16), w_ref[W:2 * W, :])
    y = x_ref[...] + gate_ref[0] * out
    if final_norm:
        ms = jnp.mean(y * y, axis=-1, keepdims=True)
        y = y * lax.rsqrt(ms + NORM_EPS) * fg_ref[...]
    o_ref[...] = y


def _out_proj(ya, yb, x, gate, w_out, final_g, tm, final_norm):
    rows, d = x.shape
    W = HALF_WIDTH
    nt = rows // tm
    groups = gate.shape[0]
    tiles_per_group = nt // groups
    rg = gate.shape[1]
    return pl.pallas_call(
        functools.partial(_out_proj_kernel, final_norm=final_norm),
        grid=(nt,),
        in_specs=[pl.BlockSpec((tm, W), lambda i: (i, 0)),
                  pl.BlockSpec((tm, W), lambda i: (i, 0)),
                  pl.BlockSpec((tm, d), lambda i: (i, 0)),
                  pl.BlockSpec((1, rg, d), lambda i: (i // tiles_per_group, 0, 0)),
                  pl.BlockSpec((2 * W, d), lambda i: (0, 0)),
                  pl.BlockSpec((1, d), lambda i: (0, 0))],
        out_specs=pl.BlockSpec((tm, d), lambda i: (i, 0)),
        out_shape=jax.ShapeDtypeStruct((rows, d), F32),
        compiler_params=_params(("arbitrary",)),
        name="out_proj",
    )(ya, yb, x, gate, w_out.astype(BF16), final_g.reshape(1, d))


def _sb_prompt_kernel(q_ref, k_ref, v_ref, zs_ref, o_ref, knorm_ref, *, tq):
    qi = pl.program_id(2)
    q = q_ref[...]
    lane = _iota((tq, LANES), 1)
    head0 = lane < HEAD_DIM
    qs = (jnp.where(head0, q, jnp.zeros_like(q)), jnp.where(head0, jnp.zeros_like(q), q))

    def head_sq_norms(x):
        sq = x.astype(F32) * x.astype(F32)
        return (jnp.sum(jnp.where(head0, sq, 0.0), axis=1, keepdims=True),
                jnp.sum(jnp.where(head0, 0.0, sq), axis=1, keepdims=True))

    @pl.when(qi == 0)
    def _():
        def scan(j, m):
            n0, n1 = head_sq_norms(k_ref[pl.ds(pl.multiple_of(j * tq, tq), tq), :])
            return (jnp.maximum(m[0], jnp.max(n0, axis=0, keepdims=True)),
                    jnp.maximum(m[1], jnp.max(n1, axis=0, keepdims=True)))
        zero = jnp.zeros((1, 1), F32)
        m0, m1 = lax.fori_loop(0, k_ref.shape[0] // tq, scan, (zero, zero))
        knorm_ref[...] = jnp.where(_iota((1, LANES), 1) == 0, m0, m1)

    qn0, qn1 = head_sq_norms(q)
    zb0 = jnp.sqrt(qn0 * knorm_ref[0:1, 0:1]) * SB_BOUND_SLACK
    zb1 = jnp.sqrt(qn1 * knorm_ref[0:1, 1:2]) * SB_BOUND_SLACK
    row = _iota((tq, tq), 0)
    col = _iota((tq, tq), 1)
    suffix = jnp.where(row >= col, 1.0, 0.0).astype(BF16)
    causal = col < row

    def blocks(items, carry):
        acc, c0, c1 = carry
        loaded, parts = [], []
        for j, diag, valid in items:
            start = pl.multiple_of(j * tq, tq)
            kb = k_ref[pl.ds(start, tq), :]
            loaded.append((v_ref[pl.ds(start, tq), :], diag, valid))
            per_head = []
            for qh in qs:
                z = _dot_nt(qh, kb)
                ls = -_softplus_scores(z)
                if diag:
                    ls = jnp.where(causal, ls, 0.0)
                hi, lo = _split_bf16(ls)
                per_head.append((z, _dot(hi, suffix) + _dot(lo, suffix)))
            parts.append(per_head)
        cs = [c0, c1]
        for (vb, diag, valid), per_head in zip(loaded, parts):
            pvs = []
            for h, (z, s) in enumerate(per_head):
                w = jnp.exp(z + s + cs[h])
                if diag:
                    w = jnp.where(causal, w, 0.0)
                total = s[:, 0:1]
                if valid is not None:
                    w = jnp.where(valid, w, 0.0)
                    total = jnp.where(valid, total, 0.0)
                pvs.append(_dot(w.astype(BF16), vb))
                cs[h] = cs[h] + total
            acc = acc + jnp.where(head0, pvs[0], pvs[1])
        return acc, cs[0], cs[1]

    def exhausted(c0, c1):
        worst = jnp.max(jnp.maximum(zb0 + c0, zb1 + c1))
        return (worst < SB_EXP_UNDERFLOW).astype(jnp.int32)

    def step(state):
        j, _, acc, c0, c1 = state
        acc, c0, c1 = blocks([(j, False, None)], (acc, c0, c1))
        return j - 1, exhausted(c0, c1), acc, c0, c1

    zero_c = jnp.zeros((tq, 1), F32)
    acc, c0, c1 = blocks([(qi, True, None), (jnp.maximum(qi - 1, 0), False, qi > 0)],
                         (jnp.zeros((tq, LANES), F32), zero_c, zero_c))
    state = lax.while_loop(lambda s: jnp.logical_and(s[0] >= 0, s[1] == 0), step,
                           (qi - 2, exhausted(c0, c1), acc, c0, c1))
    o_ref[...] = state[2] * zs_ref[...]


def _sb_prompt(q, k, v, zs, batch, seq, tq):
    rows, W = q.shape
    nq = seq // tq
    pairs = W // LANES
    return pl.pallas_call(
        functools.partial(_sb_prompt_kernel, tq=tq),
        grid=(batch, pairs, nq),
        in_specs=[pl.BlockSpec((tq, LANES), lambda b, p, i: (b * nq + i, p)),
                  pl.BlockSpec((seq, LANES), lambda b, p, i: (b, p)),
                  pl.BlockSpec((seq, LANES), lambda b, p, i: (b, p)),
                  pl.BlockSpec((tq, LANES), lambda b, p, i: (b * nq + i, p))],
        out_specs=pl.BlockSpec((tq, LANES), lambda b, p, i: (b * nq + i, p)),
        out_shape=jax.ShapeDtypeStruct((rows, W), F32),
        scratch_shapes=[pltpu.VMEM((1, LANES), F32)],
        compiler_params=_params(("arbitrary", "arbitrary", "arbitrary")),
        name="sb_prompt",
    )(q, k, v, zs)


def _moba_prompt_kernel(q_ref, qf_ref, k_ref, v_ref, km_ref, zs_ref, o_ref, kpad_ref, *, tq, nb):
    qi = pl.program_id(2)

    @pl.when(qi == 0)
    def _():
        kpad_ref[...] = jnp.zeros_like(kpad_ref)
        kpad_ref[0:nb, :] = km_ref[0]

    q = q_ref[...]
    qf = qf_ref[...]
    lane = _iota((tq, LANES), 1)
    lane_f = lane.astype(F32)
    head0 = lane < HEAD_DIM
    row = _iota((tq, tq), 0)
    col = _iota((tq, tq), 1)
    kmean = kpad_ref[...]

    qs, sels = [], []
    for h in range(2):
        hm = head0 if h == 0 else jnp.logical_not(head0)
        qs.append(jnp.where(hm, q, jnp.zeros_like(q)))
        g = _dot_nt(jnp.where(hm, qf, 0.0), kmean, HIGHEST)
        g = jnp.where(lane < qi, g, NEG_INF)
        sel = jnp.zeros((tq, LANES), F32)
        for _ in range(MOBA_TOPK):
            mx = jnp.max(g, axis=1, keepdims=True)
            is_max = jnp.logical_and(g == mx, mx > NEG_INF)
            idx = jnp.min(jnp.where(is_max, lane_f, float(LANES)), axis=1, keepdims=True)
            pick = lane_f == idx
            sel = jnp.where(pick, 1.0, sel)
            g = jnp.where(pick, NEG_INF, g)
        sels.append(sel)

    flags =[jnp.where(sels[h] > 0.0, 0.0, 1.0) for h in range(2)]
    lhs = [jnp.where(head0, qs[0], pltpu.roll(flags[0], HEAD_DIM, 1).astype(BF16)),
           jnp.where(head0, flags[1].astype(BF16), qs[1])]
    ones = jnp.ones((tq, LANES), BF16)

    def values(vb):
        return jnp.where(head0, vb, ones), jnp.where(head0, ones, vb)

    start = pl.multiple_of(qi * tq, tq)
    kb = k_ref[pl.ds(start, tq), :]
    vmods = values(v_ref[pl.ds(start, tq), :])
    state = []
    for h in range(2):
        s = jnp.where(col <= row, _dot_nt(qs[h], kb), NEG_INF)
        m = jnp.max(s, axis=1, keepdims=True)
        p = jnp.exp(s - m)
        state += [m, _dot(p.astype(BF16), vmods[h])]

    def blocks(ns, st):
        ks, vs = [], []
        for n in ns:
            start_n = pl.multiple_of(n * tq, tq)
            kn = k_ref[pl.ds(start_n, tq), :]
            pen = [jnp.where(lane == n + off, MOBA_MASK_PENALTY, 0.0).astype(BF16) for off in (HEAD_DIM, 0)]
            ks.append((jnp.where(head0, kn, pen[0]), jnp.where(head0, pen[1], kn)))
            vs.append(values(v_ref[pl.ds(start_n, tq), :]))
        ss = [[_dot_nt(lhs[h], kn[h]) for kn in ks] for h in range(2)]
        new = []
        for h in range(2):
            m, acc = st[2 * h:2 * h + 2]
            m_new = m
            for s in ss[h]:
                m_new = jnp.maximum(m_new, jnp.max(s, axis=1, keepdims=True))
            ps = [jnp.exp(s - m_new).astype(BF16) for s in ss[h]]
            acc = jnp.exp(m - m_new) * acc
            for p, vn in zip(ps, vs):
                acc = acc + _dot(p, vn[h])
            new += [m_new, acc]
        return tuple(new)

    group = MOBA_GROUP if nb % MOBA_GROUP == 0 else 1
    st = lax.fori_loop(0, (qi + group - 1) // group,
                       lambda i, s: blocks([group * i + r for r in range(group)], s), tuple(state))
    o = jnp.where(head0, st[1] / pltpu.roll(st[1], HEAD_DIM, 1), st[3] / pltpu.roll(st[3], HEAD_DIM, 1))
    o_ref[...] = o * zs_ref[...]


def _moba_prompt(q, qf, k, v, kmean, zs, batch, seq):
    rows, W = q.shape
    tq = MOBA_BLOCK
    nq = seq // tq
    pairs = W // LANES
    assert nq <= HEAD_DIM
    return pl.pallas_call(
        functools.partial(_moba_prompt_kernel, tq=tq, nb=nq),
        grid=(batch, pairs, nq),
        in_specs=[pl.BlockSpec((tq, LANES), lambda b, p, i: (b * nq + i, p)),
                  pl.BlockSpec((tq, LANES), lambda b, p, i: (b * nq + i, p)),
                  pl.BlockSpec((seq, LANES), lambda b, p, i: (b, p)),
                  pl.BlockSpec((seq, LANES), lambda b, p, i: (b, p)),
                  pl.BlockSpec((1, nq, LANES), lambda b, p, i: (b, 0, p)),
                  pl.BlockSpec((tq, LANES), lambda b, p, i: (b * nq + i, p))],
        out_specs=pl.BlockSpec((tq, LANES), lambda b, p, i: (b * nq + i, p)),
        out_shape=jax.ShapeDtypeStruct((rows, W), F32),
        scratch_shapes=[pltpu.VMEM((LANES, LANES), F32)],
        compiler_params=_params(("arbitrary", "arbitrary", "arbitrary")),
        name="moba_prompt",
    )(q, qf, k, v, kmean.reshape(batch, nq, W), zs)


def _mlstm_kernel(qk_ref, v_ref, gb_ref, gt_ref, conv0_ref, c0_ref, m0_ref,
                  cw_ref, cb_ref, brow_ref,
                  y_ref, conv_out_ref, c_out_ref, m_out_ref,
                  xpad_ref, caug_ref, m_ref, *, L):
    c = pl.program_id(1)
    nc = pl.num_programs(1)
    H, DK = ML_HEADS, ML_HEAD_DIM
    W = HALF_WIDTH
    PADR = SUBLANES

    @pl.when(c == 0)
    def _():
        xpad_ref[0:PADR, :] = conv0_ref[0]
        caug_ref[...] = c0_ref[0]
        m_ref[...] = m0_ref[0]

    xpad_ref[PADR:PADR + L, :] = qk_ref[...]
    y = cb_ref[...]
    for j in range(CONV_WIDTH):
        off = PADR - (CONV_WIDTH - 1) + j
        y = y + cw_ref[j:j + 1, :] * xpad_ref[off:off + L, :]
    tail = xpad_ref[L:L + PADR, :]
    xpad_ref[0:PADR, :] = tail
    conv_out_ref[0] = tail
    qk = _silu(y)

    lane = _iota((L, LANES), 1)
    is_f_col = jnp.logical_and(lane >= H, lane < 2 * H)
    gcol = gt_ref[0] + brow_ref[...]
    lf_col = jnp.where(is_f_col, _log_sigmoid(gcol), 0.0)
    grow = jnp.transpose(gcol)[0:2 * H, :]
    sub = _iota((2 * H, L), 0)
    lf_row = jnp.where(sub >= H, _log_sigmoid(grow), 0.0)
    row = _iota((L, L), 0)
    col = _iota((L, L), 1)
    causal = col <= row
    tri = jnp.where(causal, 1.0, 0.0)
    f_col = _dot(tri, lf_col, HIGHEST)
    f_row = _dot_nt(lf_row, tri, HIGHEST)
    ones_col = jnp.where(_iota((L, DK), 1) == 0, 1.0, 0.0).astype(BF16)
    m_all = m_ref[...]
    m_next = m_all
    lane1 = _iota((1, LANES), 1)

    for h in range(H):
        fc = f_col[:, H + h:H + h + 1]
        fr = f_row[H + h:H + h + 1, :]
        li_c = gcol[:, h:h + 1]
        li_r = grow[h:h + 1, :]
        m_prev = m_all[:, h:h + 1]
        inter = fc + m_prev
        intra = jnp.where(causal, fc - fr + li_r, NEG_INF)
        mt = jnp.maximum(inter, jnp.max(intra, axis=1, keepdims=True))
        w = jnp.exp(intra - mt)
        g = jnp.exp(inter - mt)
        qh = qk[:, h * DK:(h + 1) * DK].astype(BF16)
        kf = qk[:, W + h * DK:W + (h + 1) * DK] * (DK ** -0.5)
        kh = kf.astype(BF16)
        vaug = jnp.concatenate([v_ref[:, h * DK:(h + 1) * DK].astype(BF16), ones_col], axis=1)
        s = _dot_nt(qh, kh) * w
        nd = g * _dot(qh, caug_ref[h].astype(BF16)) + _dot(s.astype(BF16), vaug)
        den = nd[:, DK:DK + 1]
        hout = nd[:, 0:DK] / jnp.maximum(jnp.abs(den), jnp.exp(-mt))
        y_ref[:, h * DK:(h + 1) * DK] = gb_ref[:, h * DK:(h + 1) * DK] * hout
        m_new = mt[L - 1:L, :]
        f_last = fc[L - 1:L, :]
        decay = jnp.exp(f_last + m_prev - m_new)
        ws = jnp.exp(f_last - fc + li_c - m_new)
        caug_ref[h] = decay * caug_ref[h] + _dot_tn((kf * ws).astype(BF16), vaug)
        m_next = jnp.where(lane1 == h, m_new, m_next)

    m_ref[...] = m_next

    @pl.when(c == nc - 1)
    def _():
        c_out_ref[0] = caug_ref[...]
        m_out_ref[0] = m_next


def _mlstm(qk_pre, vb, gb, gates, conv0, c0aug, m0, conv_w, conv_b, b_ig, b_fg, batch, L):
    rows, W2 = qk_pre.shape
    W = HALF_WIDTH
    H = ML_HEADS
    nc = rows // batch // L
    brow = jnp.pad(jnp.concatenate([b_ig, b_fg]).reshape(1, 2 * H), ((0, 0), (0, LANES - 2 * H)))
    out_shapes = [jax.ShapeDtypeStruct((rows, W), F32),
                  jax.ShapeDtypeStruct((batch, SUBLANES, W2), F32),
                  jax.ShapeDtypeStruct((batch, H, ML_HEAD_DIM, 2 * ML_HEAD_DIM), F32),
                  jax.ShapeDtypeStruct((batch, 1, LANES), F32)]
    return pl.pallas_call(
        functools.partial(_mlstm_kernel, L=L),
        grid=(batch, nc),
        in_specs=[pl.BlockSpec((L, W2), lambda b, c: (b * nc + c, 0)),
                  pl.BlockSpec((L, W), lambda b, c: (b * nc + c, 0)),
                  pl.BlockSpec((L, W), lambda b, c: (b * nc + c, 0)),
                  pl.BlockSpec((1, L, LANES), lambda b, c: (b * nc + c, 0, 0)),
                  pl.BlockSpec((1, SUBLANES, W2), lambda b, c: (b, 0, 0)),
                  pl.BlockSpec((1, H, ML_HEAD_DIM, 2 * ML_HEAD_DIM), lambda b, c: (b, 0, 0, 0)),
                  pl.BlockSpec((1, 1, LANES), lambda b, c: (b, 0, 0)),
                  pl.BlockSpec((CONV_WIDTH, W2), lambda b, c: (0, 0)),
                  pl.BlockSpec((1, W2), lambda b, c: (0, 0)),
                  pl.BlockSpec((1, LANES), lambda b, c: (0, 0))],
        out_specs=[pl.BlockSpec((L, W), lambda b, c: (b * nc + c, 0)),
                   pl.BlockSpec((1, SUBLANES, W2), lambda b, c: (b, 0, 0)),
                   pl.BlockSpec((1, H, ML_HEAD_DIM, 2 * ML_HEAD_DIM), lambda b, c: (b, 0, 0, 0)),
                   pl.BlockSpec((1, 1, LANES), lambda b, c: (b, 0, 0))],
        out_shape=out_shapes,
        scratch_shapes=[pltpu.VMEM((L + SUBLANES, W2), F32),
                        pltpu.VMEM((H, ML_HEAD_DIM, 2 * ML_HEAD_DIM), F32),
                        pltpu.VMEM((1, LANES), F32)],
        compiler_params=_params(("arbitrary", "arbitrary")),
        name="mlstm",
    )(qk_pre, vb, gb, gates, conv0, c0aug, m0, conv_w, conv_b.reshape(1, W2), brow)


def _rglru_gates(xc, wr_ref, br_ref, wi_ref, bi_ref, lam_ref):
    xb = xc.astype(BF16)
    r = _sigmoid(_dot(xb, wr_ref[...]) + br_ref[...])
    i = _sigmoid(_dot(xb, wi_ref[...]) + bi_ref[...])
    log_a = RG_C * r * _log_sigmoid(lam_ref[...])
    a = jnp.exp(log_a)
    b = jnp.sqrt(-jnp.tanh(log_a) * (a * a + 1.0)) * (i * xc)
    return a, b


def _rglru_prompt_kernel(xd_ref, zs_ref, conv0_ref, h0_ref, cw_ref, cb_ref, wr_ref, br_ref, wi_ref, bi_ref,
                         lam_ref, y_ref, conv_out_ref, h_out_ref,
                         xpad_ref, a_ref, b_ref, hs_ref, hc_ref, *, L):
    c = pl.program_id(1)
    PADR = SUBLANES

    @pl.when(c == 0)
    def _():
        xpad_ref[0:PADR, :] = conv0_ref[0]
        hc_ref[...] = h0_ref[0]

    xpad_ref[PADR:PADR + L, :] = xd_ref[...]
    xc = cb_ref[...]
    for j in range(CONV_WIDTH):
        off = PADR - (CONV_WIDTH - 1) + j
        xc = xc + cw_ref[j:j + 1, :] * xpad_ref[off:off + L, :]
    tail = xpad_ref[L:L + PADR, :]
    xpad_ref[0:PADR, :] = tail
    conv_out_ref[0] = tail

    a, b = _rglru_gates(xc, wr_ref, br_ref, wi_ref, bi_ref, lam_ref)
    a_ref[...] = a
    b_ref[...] = b

    def step(t, h):
        h = a_ref[pl.ds(t, 1), :] * h + b_ref[pl.ds(t, 1), :]
        hs_ref[pl.ds(t, 1), :] = h
        return h

    h = lax.fori_loop(0, L, step, hc_ref[...], unroll=8)
    hc_ref[...] = h
    h_out_ref[0] = h
    y_ref[...] = hs_ref[...] * zs_ref[...]


def _block_diag(w):
    g, n, _ = w.shape
    eye = jnp.eye(g, dtype=w.dtype)
    return (eye[:, None, :, None] * w[:, :, None, :]).reshape(g * n, g * n)


def _rglru_prompt(xd, zs, conv0, h0, conv_w, conv_b, wr, br, wi, bi, lam, batch, L):
    rows, W = xd.shape
    nc = rows // batch // L
    vec = lambda a: a.reshape(1, W)
    const = lambda shape: pl.BlockSpec(shape, lambda b, c: tuple(0 for _ in shape))
    return pl.pallas_call(
        functools.partial(_rglru_prompt_kernel, L=L),
        grid=(batch, nc),
        in_specs=[pl.BlockSpec((L, W), lambda b, c: (b * nc + c, 0)),
                  pl.BlockSpec((L, W), lambda b, c: (b * nc + c, 0)),
                  pl.BlockSpec((1, SUBLANES, W), lambda b, c: (b, 0, 0)),
                  pl.BlockSpec((1, 1, W), lambda b, c: (b, 0, 0)),
                  const((CONV_WIDTH, W)), const((1, W)), const((W, W)), const((1, W)), const((W, W)),
                  const((1, W)), const((1, W))],
        out_specs=[pl.BlockSpec((L, W), lambda b, c: (b * nc + c, 0)),
                   pl.BlockSpec((1, SUBLANES, W), lambda b, c: (b, 0, 0)),
                   pl.BlockSpec((1, 1, W), lambda b, c: (b, 0, 0))],
        out_shape=[jax.ShapeDtypeStruct((rows, W), F32),
                   jax.ShapeDtypeStruct((batch, SUBLANES, W), F32),
                   jax.ShapeDtypeStruct((batch, 1, W), F32)],
        scratch_shapes=[pltpu.VMEM((L + SUBLANES, W), F32), pltpu.VMEM((L, W), F32), pltpu.VMEM((L, W), F32),
                        pltpu.VMEM((L, W), F32), pltpu.VMEM((1, W), F32)],
        compiler_params=_params(("arbitrary", "arbitrary")),
        name="rglru_prompt",
    )(xd, zs, conv0, h0, conv_w, vec(conv_b), _block_diag(wr).astype(BF16), vec(br),
      _block_diag(wi).astype(BF16), vec(bi), vec(lam))


def _rglru_sample_kernel(xd_ref, zs_ref, conv0_ref, h0_ref, cw_ref, cb_ref, wr_ref, br_ref, wi_ref, bi_ref,
                         lam_ref, y_ref, h_out_ref, *, T):
    xs = [conv0_ref[j] for j in range(CONV_WIDTH - 1)] + [xd_ref[t] for t in range(T)]
    h = h0_ref[...]
    for t in range(T):
        xc = cb_ref[...]
        for j in range(CONV_WIDTH):
            xc = xc + cw_ref[j:j + 1, :] * xs[t + j]
        a, b = _rglru_gates(xc, wr_ref, br_ref, wi_ref, bi_ref, lam_ref)
        h = a * h + b
        y_ref[t] = h * zs_ref[t]
    h_out_ref[...] = h


def _rglru_sample(xd, zs, conv0, h0, conv_w, conv_b, wr, br, wi, bi, lam):
    T, B, W = xd.shape
    vec = lambda a: a.reshape(1, W)
    return pl.pallas_call(
        functools.partial(_rglru_sample_kernel, T=T),
        out_shape=[jax.ShapeDtypeStruct((T, B, W), F32), jax.ShapeDtypeStruct((B, W), F32)],
        compiler_params=pltpu.CompilerParams(vmem_limit_bytes=VMEM_LIMIT),
        name="rglru_sample",
    )(xd, zs, conv0, h0, conv_w, vec(conv_b), _block_diag(wr).astype(BF16), vec(br),
      _block_diag(wi).astype(BF16), vec(bi), vec(lam))


def _query_rows(q):
    b, t, w = q.shape
    heads = w // HEAD_DIM
    onehot = (jnp.arange(heads)[:, None] == (jnp.arange(w) // HEAD_DIM)[None, :]).astype(q.dtype)
    return (q[:, :, None, :] * onehot[None, None]).reshape(b, t * heads, w)


def _channel_major(a, lanes):
    return jnp.pad(jnp.transpose(a, (0, 2, 1)), ((0, 0), (0, 0), (0, lanes - a.shape[1])))


def _pool_channel_major(pool):
    n_pool, page_rows, heads, hd = pool.shape
    return jnp.transpose(pool, (0, 2, 3, 1)).reshape(n_pool, heads * hd, page_rows)


def _page_specs(pages_per_step, page_rows, width, page_of):
    def spec(i):
        return pl.BlockSpec((None, width, page_rows), lambda b, c, pt: (pt[b, page_of(c, i)], 0, 0))
    return [spec(i) for i in range(pages_per_step)]


def _head_diagonal(o, t_new, zs):
    heads = HALF_WIDTH // HEAD_DIM
    rows = t_new * heads
    keep = (_iota((rows, HALF_WIDTH), 1) // HEAD_DIM) == (_iota((rows, HALF_WIDTH), 0) % heads)
    o = jnp.where(keep, o, 0.0)
    return jnp.sum(o.reshape(t_new, heads, HALF_WIDTH), axis=1) * zs


def _key_absmax_kernel(pt_ref, *rest, P):
    kpages, o_ref = rest[:P], rest[P]
    c = pl.program_id(1)
    m = jnp.abs(kpages[0][...])
    for i in range(1, P):
        m = jnp.maximum(m, jnp.abs(kpages[i][...]))

    @pl.when(c == 0)
    def _():
        o_ref[0] = m

    @pl.when(c > 0)
    def _():
        o_ref[0] = jnp.maximum(o_ref[0], m)


def _key_absmax(page_table, k_pool, P):
    B, n_pages = page_table.shape
    W, page_rows = k_pool.shape[1], k_pool.shape[2]
    grid_spec = pltpu.PrefetchScalarGridSpec(
        num_scalar_prefetch=1,
        grid=(B, n_pages // P),
        in_specs=_page_specs(P, page_rows, W, lambda c, i: c * P + i),
        out_specs=pl.BlockSpec((1, W, page_rows), lambda b, c, pt: (b, 0, 0)),
    )
    return pl.pallas_call(
        functools.partial(_key_absmax_kernel, P=P),
        grid_spec=grid_spec,
        out_shape=jax.ShapeDtypeStruct((B, W, page_rows), F32),
        compiler_params=_params(("arbitrary", "arbitrary")),
        name="key_absmax",
    )(page_table, *([k_pool] * P))


def _sb_sample_kernel(pt_ref, q_ref, knew_ref, vnew_ref, zs_ref, kabs_ref, acc0_ref, car0_ref, *rest,
                      P, t_new, newest):
    kpages, vpages = rest[:P], rest[P:2 * P]
    o_ref, acc_out_ref, car_out_ref, done_ref, acc_ref, car_ref = rest[2 * P:]
    c = pl.program_id(1)
    nch = pl.num_programs(1)
    q = q_ref[0]
    heads = HALF_WIDTH // HEAD_DIM
    rows = t_new * heads

    def key_blocks(kts, vts, mask):
        n = kts[0].shape[1]
        suffix = jnp.where(_iota((n, n), 0) >= _iota((n, n), 1), 1.0, 0.0).astype(BF16)
        zs = [_dot(q, kt.astype(BF16)) for kt in kts]
        lss = [-_softplus_scores(z) for z in zs]
        if mask is not None:
            lss = [jnp.where(mask, ls, 0.0) for ls in lss]
        sums = []
        for ls in lss:
            hi, lo = _split_bf16(ls)
            sums.append(_dot(hi, suffix) + _dot(lo, suffix))
        carry = car_ref[...]
        acc = acc_ref[...]
        for z, s, vt in zip(zs, sums, vts):
            w = jnp.exp(z + s + carry)
            if mask is not None:
                w = jnp.where(mask, w, 0.0)
            acc = acc + _dot_nt(w.astype(BF16), vt.astype(BF16))
            carry = carry + s[:, 0:1]
        acc_ref[...] = acc
        car_ref[...] = carry

    @pl.when(c == 0)
    def _():
        if newest:
            acc_ref[...] = jnp.zeros_like(acc_ref)
            car_ref[...] = jnp.zeros_like(car_ref)
            n = knew_ref.shape[2]
            j = _iota((rows, n), 1)
            t = _iota((rows, n), 0) // heads
            key_blocks([knew_ref[0]], [vnew_ref[0]], jnp.logical_and(j < t, j < t_new))
        else:
            acc_ref[...] = acc0_ref[0]
            car_ref[...] = car0_ref[0][:, 0:1]

    key_blocks([kpages[i][...] for i in reversed(range(P))], [vpages[i][...] for i in reversed(range(P))], None)

    @pl.when(c == nch - 1)
    def _():
        o_ref[0] = _head_diagonal(acc_ref[...], t_new, zs_ref[0])
        acc_out_ref[0] = acc_ref[...]
        car_out_ref[0] = jnp.broadcast_to(car_ref[...], (rows, LANES))
        bound = jnp.max(_dot(jnp.abs(q.astype(F32)), kabs_ref[0], HIGHEST), axis=1, keepdims=True)
        worst = jnp.max(bound * SB_BOUND_SLACK + car_ref[...])
        done_ref[0] = jnp.where(worst < SB_EXP_UNDERFLOW, 1.0, 0.0) * jnp.ones((1, LANES), F32)


def _sb_sample_pages(page_table, q_rows, knew_t, vnew_t, zs, kabs, acc0, car0, k_pool, v_pool, P, first_page,
                     n_used, newest):
    B = page_table.shape[0]
    t_new = zs.shape[1]
    W = HALF_WIDTH
    rows = q_rows.shape[1]
    page_rows = k_pool.shape[2]
    nch = n_used // P
    page_of = lambda c, i: first_page + (nch - 1 - c) * P + i
    per_b = lambda shape: pl.BlockSpec((1,) + shape, lambda b, c, pt: (b,) + tuple(0 for _ in shape))
    grid_spec = pltpu.PrefetchScalarGridSpec(
        num_scalar_prefetch=1,
        grid=(B, nch),
        in_specs=[per_b((rows, W)), per_b(knew_t.shape[1:]), per_b(vnew_t.shape[1:]), per_b((t_new, W)),
                  per_b((W, page_rows)), per_b((rows, W)), per_b((rows, LANES))]
        + _page_specs(P, page_rows, W, page_of) + _page_specs(P, page_rows, W, page_of),
        out_specs=[per_b((t_new, W)), per_b((rows, W)), per_b((rows, LANES)), per_b((1, LANES))],
        scratch_shapes=[pltpu.VMEM((rows, W), F32), pltpu.VMEM((rows, 1), F32)],
    )
    return pl.pallas_call(
        functools.partial(_sb_sample_kernel, P=P, t_new=t_new, newest=newest),
        grid_spec=grid_spec,
        out_shape=[jax.ShapeDtypeStruct((B, t_new, W), F32), jax.ShapeDtypeStruct((B, rows, W), F32),
                   jax.ShapeDtypeStruct((B, rows, LANES), F32), jax.ShapeDtypeStruct((B, 1, LANES), F32)],
        compiler_params=_params(("arbitrary", "arbitrary")),
        name="sb_sample_newest" if newest else "sb_sample_older",
    )(page_table, q_rows, knew_t, vnew_t, zs, kabs, acc0, car0, *([k_pool] * P), *([v_pool] * P))


def _sb_sample(page_table, q_rows, knew_t, vnew_t, zs, k_pool, v_pool, p_scan, p_newest):
    B, n_pages = page_table.shape
    rows = q_rows.shape[1]
    p_newest = min(p_newest, n_pages)
    kabs = _key_absmax(page_table, k_pool, p_scan)
    y, acc, car, done = _sb_sample_pages(
        page_table, q_rows, knew_t, vnew_t, zs, kabs, jnp.zeros((B, rows, HALF_WIDTH), F32),
        jnp.zeros((B, rows, LANES), F32), k_pool, v_pool, p_newest, n_pages - p_newest, p_newest, True)
    n_older = n_pages - p_newest
    if n_older == 0:
        return y
    p_older = max(p for p in range(1, p_scan + 1) if n_older % p == 0)
    older = lambda: _sb_sample_pages(page_table, q_rows, knew_t, vnew_t, zs, kabs, acc, car, k_pool, v_pool,
                                     p_older, 0, n_older, False)[0]
    return lax.cond(jnp.min(done) > 0.5, lambda: y, older)


def _moba_sample_kernel(pt_ref, q_ref, qg_ref, knew_ref, vnew_ref, zs_ref, *rest, P, t_new, nb, pb):
    kpages, vpages = rest[:P], rest[P:2 * P]
    o_ref, m_ref, l_ref, acc_ref, km_ref = rest[2 * P:]
    c = pl.program_id(1)
    nch = pl.num_programs(1)
    q = q_ref[0]
    heads = HALF_WIDTH // HEAD_DIM
    rows = t_new * heads
    lane = _iota((rows, LANES), 1)

    @pl.when(c == 0)
    def _():
        km_ref[...] = jnp.zeros_like(km_ref)
        m_ref[...] = jnp.zeros_like(m_ref)
        l_ref[...] = jnp.zeros_like(l_ref)

    nblk = P // pb
    zs = [[_dot(q, kpages[pb * s + r][...].astype(BF16)) for r in range(pb)] for s in range(nblk)]
    km_all, m_all, l_all = km_ref[...], m_ref[...], l_ref[...]
    km_lane = _iota(km_ref.shape, 1)
    for s in range(nblk):
        n = c * nblk + s
        ksum = kpages[pb * s][...]
        for r in range(1, pb):
            ksum = ksum + kpages[pb * s + r][...]
        kmean = jnp.sum(ksum, axis=1, keepdims=True) * (1.0 / MOBA_BLOCK)
        km_all = jnp.where(km_lane == n, kmean, km_all)
        m = jnp.max(zs[s][0], axis=1, keepdims=True)
        for z in zs[s][1:]:
            m = jnp.maximum(m, jnp.max(z, axis=1, keepdims=True))
        l = jnp.zeros((rows, 1), F32)
        a = jnp.zeros((rows, HALF_WIDTH), F32)
        for r, z in enumerate(zs[s]):
            p = jnp.exp(z - m)
            l = l + jnp.sum(p, axis=1, keepdims=True)
            a = a + _dot_nt(p.astype(BF16), vpages[pb * s + r][...].astype(BF16))
        m_all = jnp.where(lane == n, m, m_all)
        l_all = jnp.where(lane == n, l, l_all)
        acc_ref[n] = a
    km_ref[...] = km_all
    m_ref[...] = m_all
    l_ref[...] = l_all

    @pl.when(c == nch - 1)
    def _():
        lane_f = lane.astype(F32)
        g = _dot(qg_ref[0], km_ref[...], HIGHEST)
        g = jnp.where(lane < nb, g, NEG_INF)
        sel = jnp.zeros((rows, LANES), F32)
        for _ in range(min(MOBA_TOPK, nb)):
            mx = jnp.max(g, axis=1, keepdims=True)
            is_max = jnp.logical_and(g == mx, mx > NEG_INF)
            idx = jnp.min(jnp.where(is_max, lane_f, float(LANES)), axis=1, keepdims=True)
            pick = lane_f == idx
            sel = jnp.where(pick, 1.0, sel)
            g = jnp.where(pick, NEG_INF, g)
        nn = knew_ref.shape[2]
        j = _iota((rows, nn), 1)
        t = _iota((rows, nn), 0) // heads
        zn = jnp.where(jnp.logical_and(j <= t, j < t_new), _dot(q, knew_ref[0].astype(BF16)), NEG_INF)
        m_all = m_ref[...]
        m_tot = jnp.maximum(jnp.max(jnp.where(sel > 0.0, m_all, NEG_INF), axis=1, keepdims=True),
                            jnp.max(zn, axis=1, keepdims=True))
        coef = jnp.where(sel > 0.0, jnp.exp(m_all - m_tot), 0.0)
        p_own = jnp.exp(zn - m_tot)
        denom = jnp.sum(coef * l_ref[...], axis=1, keepdims=True) + jnp.sum(p_own, axis=1, keepdims=True)
        o = _dot_nt(p_own.astype(BF16), vnew_ref[0].astype(BF16))
        for n in range(nb):
            o = o + coef[:, n:n + 1] * acc_ref[n]
        o_ref[0] = _head_diagonal(o / denom, t_new, zs_ref[0])


def _moba_sample(page_table, q_rows, qg_rows, knew_t, vnew_t, zs, k_pool, v_pool, P):
    B, n_pages = page_table.shape
    t_new = zs.shape[1]
    W = HALF_WIDTH
    rows = q_rows.shape[1]
    page_rows = k_pool.shape[2]
    pb = MOBA_BLOCK // page_rows
    nch = n_pages // P
    nb = n_pages // pb
    assert nb <= LANES and P % pb == 0
    page_of = lambda c, i: c * P + i
    per_b = lambda shape: pl.BlockSpec((1,) + shape, lambda b, c, pt: (b,) + tuple(0 for _ in shape))
    grid_spec = pltpu.PrefetchScalarGridSpec(
        num_scalar_prefetch=1,
        grid=(B, nch),
        in_specs=[per_b((rows, W)), per_b((rows, W)), per_b(knew_t.shape[1:]), per_b(vnew_t.shape[1:]),
                  per_b((t_new, W))]
        + _page_specs(P, page_rows, W, page_of) + _page_specs(P, page_rows, W, page_of),
        out_specs=per_b((t_new, W)),
        scratch_shapes=[pltpu.VMEM((rows, LANES), F32), pltpu.VMEM((rows, LANES), F32),
                        pltpu.VMEM((nb, rows, W), F32), pltpu.VMEM((W, LANES), F32)],
    )
    return pl.pallas_call(
        functools.partial(_moba_sample_kernel, P=P, t_new=t_new, nb=nb, pb=pb),
        grid_spec=grid_spec,
        out_shape=jax.ShapeDtypeStruct((B, t_new, W), F32),
        compiler_params=_params(("arbitrary", "arbitrary")),
        name="moba_sample",
    )(page_table, q_rows, qg_rows, knew_t, vnew_t, zs, *([k_pool] * P), *([v_pool] * P))


def _pad_rows(a, rows):
    return jnp.pad(a, ((0, 0), (0, rows - a.shape[1]), (0, 0)))


def kernel(x_prompt, x_sample, c_prompt, c_sample, page_table, cache_k_sb, cache_v_sb, state_conv_mlstm, state_c_mlstm, state_n_mlstm, state_m_mlstm, cache_k_moba, cache_v_moba, state_conv_rglru, state_h_rglru, norm_g_even, mod_w_even, mod_b_even, w_in_even, conv_w_even, conv_b_even, b_igate_even, b_fgate_even, w_out_even, norm_g_odd, mod_w_odd, mod_b_odd, w_in_odd, conv_w_odd, conv_b_odd, w_rgate_odd, b_rgate_odd, w_igate_odd, b_igate_odd, lru_lambda_odd, w_out_odd, final_g):
    B, T, D = x_prompt.shape
    DB, DT, _ = x_sample.shape
    W = HALF_WIDTH
    H = ML_HEADS
    heads = W // HEAD_DIM
    n_pool, page_rows = cache_k_sb.shape[1], cache_k_sb.shape[2]
    past = page_table.shape[1] * page_rows
    depth = norm_g_even.shape[0] + norm_g_odd.shape[0]
    TM = 256
    PAGES_PER_STEP = 8
    SB_NEWEST_PAGES = 4
    NEW_KEYS = LANES
    ML_PAD = SUBLANES

    xp = x_prompt.reshape(B * T, D)
    xs = x_sample.reshape(DB * DT, D)
    c_all = jnp.concatenate([c_prompt, c_sample], axis=0)
    c_rows = -(-c_all.shape[0] // SUBLANES) * SUBLANES
    c_all = jnp.pad(c_all, ((0, c_rows - c_all.shape[0]), (0, 0)))
    cos_p, sin_p = _rope_tables(jnp.arange(T))
    cos_s, sin_s = _rope_tables(past + jnp.arange(DT))
    cos_s, sin_s = jnp.tile(cos_s, (DB, 1)), jnp.tile(sin_s, (DB, 1))

    def mods(w, b):
        mod = _modulation(c_all, w, b)
        parts = []
        for part in jnp.split(mod, 3, axis=-1):
            p_part = part[:B].reshape(B, 1, D)
            s_part = jnp.repeat(part[B:B + DB], DT, axis=0).reshape(1, DB * DT, D)
            parts.append((p_part, s_part))
        return parts

    p_even, s_even, p_odd, s_odd = [], [], [], []
    y_prompt = y_sample = None
    for l in range(depth):
        j = l // 2
        last = l == depth - 1
        if l % 2 == 0:
            (sh_p, sh_s), (sc_p, sc_s), (ga_p, ga_s) = mods(mod_w_even[j], mod_b_even[j])
            qa, ka, kab, va, vab, za, qk, vb, gb, gt = _in_proj_even(
                xp, sh_p, sc_p, norm_g_even[j], w_in_even[j], TM)
            ya = _sb_prompt(qa, kab, vab, za, B, T, TM)
            yb, conv_p, caug_p, m_p = _mlstm(
                qk, vb, gb, gt,
                jnp.zeros((B, SUBLANES, 2 * W), F32), jnp.zeros((B, H, ML_HEAD_DIM, 2 * ML_HEAD_DIM), F32),
                jnp.zeros((B, 1, LANES), F32),
                conv_w_even[j], conv_b_even[j], b_igate_even[j], b_fgate_even[j], B, TM)
            xp = _out_proj(ya, yb, xp, ga_p, w_out_even[j], final_g, TM, last)
            p_even.append((ka.reshape(B, T, heads, HEAD_DIM), va.reshape(B, T, heads, HEAD_DIM),
                           conv_p[:, SUBLANES - (CONV_WIDTH - 1):], caug_p[..., :ML_HEAD_DIM],
                           caug_p[..., ML_HEAD_DIM], m_p[:, 0, :H]))
            qa, ka, kab, va, vab, za, qk, vb, gb, gt = _in_proj_even(
                xs, sh_s, sc_s, norm_g_even[j], w_in_even[j], DB * DT)
            ya = _sb_sample(page_table, _query_rows(qa.reshape(DB, DT, W)),
                            _channel_major(ka.reshape(DB, DT, W), NEW_KEYS),
                            _channel_major(va.reshape(DB, DT, W), NEW_KEYS), za.reshape(DB, DT, W),
                            _pool_channel_major(cache_k_sb[j]), _pool_channel_major(cache_v_sb[j]),
                            PAGES_PER_STEP, SB_NEWEST_PAGES).reshape(DB * DT, W)
            pad_t = lambda a: _pad_rows(a.reshape(DB, DT, a.shape[-1]), ML_PAD).reshape(DB * ML_PAD, a.shape[-1])
            g_rows = gt.reshape(DB, DT, LANES)[:, :, :2 * H]
            pad_gate = jnp.concatenate([jnp.full((H,), NEG_INF, F32), jnp.full((H,), jnp.inf, F32)])
            g_rows = jnp.concatenate([g_rows, jnp.broadcast_to(pad_gate, (DB, ML_PAD - DT, 2 * H))], axis=1)
            gt_s = jnp.pad(g_rows, ((0, 0), (0, 0), (0, LANES - 2 * H)))
            conv0 = jnp.pad(state_conv_mlstm[j], ((0, 0), (SUBLANES - (CONV_WIDTH - 1), 0), (0, 0)))
            c0aug = jnp.concatenate(
                [state_c_mlstm[j].astype(F32), state_n_mlstm[j].astype(F32)[..., None],
                 jnp.zeros((DB, H, ML_HEAD_DIM, ML_HEAD_DIM - 1), F32)], axis=-1)
            m0 = jnp.pad(state_m_mlstm[j].astype(F32), ((0, 0), (0, LANES - H))).reshape(DB, 1, LANES)
            yb, _, caug_s, m_s = _mlstm(
                pad_t(qk), pad_t(vb), pad_t(gb), gt_s, conv0, c0aug, m0,
                conv_w_even[j], conv_b_even[j], b_igate_even[j], b_fgate_even[j], DB, ML_PAD)
            yb = yb.reshape(DB, ML_PAD, W)[:, :DT].reshape(DB * DT, W)
            conv_s = jnp.concatenate([state_conv_mlstm[j].astype(F32), qk.reshape(DB, DT, 2 * W)],
                                     axis=1)[:, -(CONV_WIDTH - 1):]
            xs = _out_proj(ya, yb, xs, ga_s, w_out_even[j], final_g, DB * DT, last)
            s_even.append((ka.reshape(DB, DT, heads, HEAD_DIM), va.reshape(DB, DT, heads, HEAD_DIM),
                           conv_s, caug_s[..., :ML_HEAD_DIM], caug_s[..., ML_HEAD_DIM], m_s[:, 0, :H]))
        else:
            (sh_p, sh_s), (sc_p, sc_s), (ga_p, ga_s) = mods(mod_w_odd[j], mod_b_odd[j])
            lru = (conv_w_odd[j], conv_b_odd[j], w_rgate_odd[j], b_rgate_odd[j], w_igate_odd[j], b_igate_odd[j],
                   lru_lambda_odd[j])
            qb, qf, kc, kcb, vc, vcb, zc, xd, zd, km = _in_proj_odd(
                xp, sh_p, sc_p, norm_g_odd[j], w_in_odd[j], cos_p, sin_p, TM)
            yc = _moba_prompt(qb, qf, kcb, vcb, km, zc, B, T)
            yd, conv_p, h_p = _rglru_prompt(xd, zd, jnp.zeros((B, SUBLANES, W), F32), jnp.zeros((B, 1, W), F32),
                                            *lru, B, TM)
            xp = _out_proj(yc, yd, xp, ga_p, w_out_odd[j], final_g, TM, last)
            p_odd.append((kc.reshape(B, T, heads, HEAD_DIM), vc.reshape(B, T, heads, HEAD_DIM),
                          conv_p[:, SUBLANES - (CONV_WIDTH - 1):], h_p[:, 0]))
            qb, qf, kc, kcb, vc, vcb, zc, xd, zd, km = _in_proj_odd(
                xs, sh_s, sc_s, norm_g_odd[j], w_in_odd[j], cos_s, sin_s, DB * DT)
            yc = _moba_sample(page_table, _query_rows(qb.reshape(DB, DT, W)), _query_rows(qf.reshape(DB, DT, W)),
                              _channel_major(kc.reshape(DB, DT, W), NEW_KEYS),
                              _channel_major(vc.reshape(DB, DT, W), NEW_KEYS), zc.reshape(DB, DT, W),
                              _pool_channel_major(cache_k_moba[j]), _pool_channel_major(cache_v_moba[j]),
                              PAGES_PER_STEP).reshape(DB * DT, W)
            tmaj = lambda a: jnp.transpose(a.reshape(DB, DT, W), (1, 0, 2))
            yd, h_s = _rglru_sample(tmaj(xd), tmaj(zd), jnp.transpose(state_conv_rglru[j].astype(F32), (1, 0, 2)),
                                    state_h_rglru[j].astype(F32), *lru)
            yd = jnp.transpose(yd, (1, 0, 2)).reshape(DB * DT, W)
            conv_s = jnp.concatenate([state_conv_rglru[j].astype(F32), xd.reshape(DB, DT, W)],
                                     axis=1)[:, -(CONV_WIDTH - 1):]
            xs = _out_proj(yc, yd, xs, ga_s, w_out_odd[j], final_g, DB * DT, last)
            s_odd.append((kc.reshape(DB, DT, heads, HEAD_DIM), vc.reshape(DB, DT, heads, HEAD_DIM), conv_s, h_s))

    y_prompt = xp.reshape(B, T, D)
    y_sample = xs.reshape(DB, DT, D)
    stack = lambda group: [jnp.stack(a) for a in zip(*group)]
    pk_sb, pv_sb, pconv_m, pc_m, pn_m, pm_m = stack(p_even)
    sk_sb, sv_sb, sconv_m, sc_m, sn_m, sm_m = stack(s_even)
    pk_mb, pv_mb, pconv_d, ph_d = stack(p_odd)
    sk_mb, sv_mb, sconv_d, sh_d = stack(s_odd)
    return (y_prompt, y_sample,
            pk_sb, pv_sb, pconv_m, pc_m, pn_m, pm_m, pk_mb, pv_mb, pconv_d, ph_d,
            sk_sb, sv_sb, sconv_m, sc_m, sn_m, sm_m, sk_mb, sv_mb, sconv_d, sh_d)
```

```python
import functools

import jax
import jax.numpy as jnp
from jax import lax
from jax.experimental import pallas as pl
from jax.experimental.pallas import tpu as pltpu

F32 = jnp.float32
BF16 = jnp.bfloat16
HIGHEST = lax.Precision.HIGHEST

HEAD_DIM = 64
HALF_WIDTH = 512
ML_HEADS = 4
ML_HEAD_DIM = 128
CONV_WIDTH = 4
MOBA_BLOCK = 256
MOBA_TOPK = 3
RG_C = 8.0
ROPE_THETA = 10000.0
NORM_EPS = 1e-6
LANES = 128
SUBLANES = 8
VMEM_LIMIT = 56 * 1024 * 1024
NEG_INF = float("-inf")
SB_EXP_UNDERFLOW = -110.0
SB_BOUND_SLACK = 1.01
MOBA_MASK_PENALTY = -1e30
MOBA_GROUP = 4


def _params(sem):
    return pltpu.CompilerParams(dimension_semantics=sem, vmem_limit_bytes=VMEM_LIMIT)


def _sigmoid(x):
    return 1.0 / (1.0 + jnp.exp(-x))


def _silu(x):
    return x * _sigmoid(x)


def _softplus(x):
    return jnp.maximum(x, 0.0) + jnp.log1p(jnp.exp(-jnp.abs(x)))


def _softplus_scores(x):
    return jnp.maximum(x, 0.0) + jnp.log(1.0 + jnp.exp(-jnp.abs(x)))


def _log_sigmoid(x):
    return -_softplus(-x)


def _dot(a, b, precision=None):
    return jnp.dot(a, b, preferred_element_type=F32, precision=precision)


def _dot_nt(a, b, precision=None):
    return lax.dot_general(a, b, (((1,), (1,)), ((), ())), preferred_element_type=F32, precision=precision)


def _dot_tn(a, b, precision=None):
    return lax.dot_general(a, b, (((0,), (0,)), ((), ())), preferred_element_type=F32, precision=precision)


def _split_bf16(x):
    hi = x.astype(BF16)
    lo = (x - hi.astype(F32)).astype(BF16)
    return hi, lo


def _iota(shape, dim):
    return lax.broadcasted_iota(jnp.int32, shape, dim)


def _mod_kernel(c_ref, w_ref, b_ref, o_ref):
    o_ref[...] = _dot(_silu(c_ref[...]), w_ref[...], HIGHEST) + b_ref[...]


def _modulation(c, w, b):
    rows, d = c.shape
    n = w.shape[1]
    tn = 512
    return pl.pallas_call(
        _mod_kernel,
        grid=(n // tn,),
        in_specs=[pl.BlockSpec((rows, d), lambda j: (0, 0)),
                  pl.BlockSpec((d, tn), lambda j: (0, j)),
                  pl.BlockSpec((1, tn), lambda j: (0, j))],
        out_specs=pl.BlockSpec((rows, tn), lambda j: (0, j)),
        out_shape=jax.ShapeDtypeStruct((rows, n), F32),
        compiler_params=_params(("arbitrary",)),
        name="modulation",
    )(c, w, b.reshape(1, n))


def _normed(x_ref, shift_ref, scale_ref, g_ref):
    x = x_ref[...]
    ms = jnp.mean(x * x, axis=-1, keepdims=True)
    h = x * lax.rsqrt(ms + NORM_EPS) * g_ref[...]
    return h * (1.0 + scale_ref[0]) + shift_ref[0]


def _in_even_kernel(x_ref, shift_ref, scale_ref, g_ref, w_ref, wg_hi_ref, wg_lo_ref,
                    qa_ref, ka_ref, kab_ref, va_ref, vab_ref, za_ref, qk_ref, vb_ref, gb_ref, gt_ref):
    h = _normed(x_ref, shift_ref, scale_ref, g_ref)
    hb, h_lo = _split_bf16(h)
    W = HALF_WIDTH

    def proj(c):
        return _dot(hb, w_ref[:, c * W:(c + 1) * W])

    qa_ref[...] = (proj(0) * (HEAD_DIM ** -0.5)).astype(BF16)
    ka = proj(1)
    ka_ref[...] = ka
    kab_ref[...] = ka.astype(BF16)
    va = proj(2)
    va_ref[...] = va
    vab_ref[...] = va.astype(BF16)
    za_ref[...] = _silu(proj(3))
    qk_ref[:, 0:W] = proj(4)
    qk_ref[:, W:2 * W] = proj(5)
    vb_ref[...] = proj(6)
    gb_ref[...] = _sigmoid(proj(7)) * _silu(proj(8))
    gt_ref[0] = _dot(hb, wg_hi_ref[...]) + _dot(h_lo, wg_hi_ref[...]) + _dot(hb, wg_lo_ref[...])


def _in_proj_even(x, shift, scale, g, w_in, tm):
    rows, d = x.shape
    W = HALF_WIDTH
    nt = rows // tm
    groups = shift.shape[0]
    tiles_per_group = nt // groups
    w_main = w_in[:, :9 * W].astype(BF16)
    w_gate_hi, w_gate_lo = _split_bf16(jnp.pad(w_in[:, 9 * W:], ((0, 0), (0, LANES - 2 * ML_HEADS))))
    row_blk = lambda n, dt: (pl.BlockSpec((tm, n), lambda i: (i, 0)), jax.ShapeDtypeStruct((rows, n), dt))
    outs = [row_blk(W, BF16), row_blk(W, F32), row_blk(W, BF16), row_blk(W, F32), row_blk(W, BF16),
            row_blk(W, F32), row_blk(2 * W, F32), row_blk(W, F32), row_blk(W, F32),
            (pl.BlockSpec((1, tm, LANES), lambda i: (i, 0, 0)), jax.ShapeDtypeStruct((nt, tm, LANES), F32))]
    rg = shift.shape[1]
    return pl.pallas_call(
        _in_even_kernel,
        grid=(nt,),
        in_specs=[pl.BlockSpec((tm, d), lambda i: (i, 0)),
                  pl.BlockSpec((1, rg, d), lambda i: (i // tiles_per_group, 0, 0)),
                  pl.BlockSpec((1, rg, d), lambda i: (i // tiles_per_group, 0, 0)),
                  pl.BlockSpec((1, d), lambda i: (0, 0)),
                  pl.BlockSpec((d, 9 * W), lambda i: (0, 0)),
                  pl.BlockSpec((d, LANES), lambda i: (0, 0)),
                  pl.BlockSpec((d, LANES), lambda i: (0, 0))],
        out_specs=[o[0] for o in outs],
        out_shape=[o[1] for o in outs],
        compiler_params=_params(("arbitrary",)),
        name="in_proj_even",
    )(x, shift, scale, g.reshape(1, d), w_main, w_gate_hi, w_gate_lo)


def _in_odd_kernel(x_ref, shift_ref, scale_ref, g_ref, w_ref, cos_ref, sin_ref,
                   qb_ref, qf_ref, kc_ref, kcb_ref, vc_ref, vcb_ref, zc_ref, xd_ref, zd_ref, km_ref):
    h = _normed(x_ref, shift_ref, scale_ref, g_ref)
    hb = h.astype(BF16)
    W = HALF_WIDTH

    def proj(c):
        return _dot(hb, w_ref[:, c * W:(c + 1) * W])

    cos = cos_ref[...]
    sin = sin_ref[...]
    first_half = (_iota(cos.shape, 1) % HEAD_DIM) < (HEAD_DIM // 2)

    def rope(x):
        partner = jnp.where(first_half, pltpu.roll(x, W - HEAD_DIM // 2, 1), pltpu.roll(x, HEAD_DIM // 2, 1))
        return x * cos + partner * sin

    q = rope(proj(0))
    qf_ref[...] = q
    qb_ref[...] = (q * (HEAD_DIM ** -0.5)).astype(BF16)
    k = rope(proj(1))
    kc_ref[...] = k
    kcb_ref[...] = k.astype(BF16)
    blk = min(MOBA_BLOCK, k.shape[0])
    for r in range(k.shape[0] // blk):
        km_ref[r] = jnp.mean(k[r * blk:(r + 1) * blk], axis=0, keepdims=True)
    v = proj(2)
    vc_ref[...] = v
    vcb_ref[...] = v.astype(BF16)
    zc_ref[...] = _silu(proj(3))
    xd_ref[...] = proj(4)
    zd_ref[...] = _silu(proj(5))


def _in_proj_odd(x, shift, scale, g, w_in, cos, sin, tm):
    rows, d = x.shape
    W = HALF_WIDTH
    nt = rows // tm
    nkm = max(tm // MOBA_BLOCK, 1)
    groups = shift.shape[0]
    tiles_per_group = nt // groups
    rg = shift.shape[1]
    row_blk = lambda n, dt: (pl.BlockSpec((tm, n), lambda i: (i, 0)), jax.ShapeDtypeStruct((rows, n), dt))
    outs = [row_blk(W, BF16), row_blk(W, F32), row_blk(W, F32), row_blk(W, BF16), row_blk(W, F32),
            row_blk(W, BF16), row_blk(W, F32), row_blk(W, F32), row_blk(W, F32),
            (pl.BlockSpec((nkm, 1, W), lambda i: (i, 0, 0)), jax.ShapeDtypeStruct((nt * nkm, 1, W), F32))]
    return pl.pallas_call(
        _in_odd_kernel,
        grid=(nt,),
        in_specs=[pl.BlockSpec((tm, d), lambda i: (i, 0)),
                  pl.BlockSpec((1, rg, d), lambda i: (i // tiles_per_group, 0, 0)),
                  pl.BlockSpec((1, rg, d), lambda i: (i // tiles_per_group, 0, 0)),
                  pl.BlockSpec((1, d), lambda i: (0, 0)),
                  pl.BlockSpec((d, 6 * W), lambda i: (0, 0)),
                  pl.BlockSpec((tm, W), lambda i: (i % tiles_per_group, 0)),
                  pl.BlockSpec((tm, W), lambda i: (i % tiles_per_group, 0))],
        out_specs=[o[0] for o in outs],
        out_shape=[o[1] for o in outs],
        compiler_params=_params(("arbitrary",)),
        name="in_proj_odd",
    )(x, shift, scale, g.reshape(1, d), w_in.astype(BF16), cos, sin)


def _rope_tables(positions):
    half = HEAD_DIM // 2
    freqs = ROPE_THETA ** (-jnp.arange(half, dtype=F32) / half)
    ang = positions.astype(F32)[:, None] * freqs[None, :]
    cos = jnp.cos(ang)
    sin = jnp.sin(ang)
    heads = HALF_WIDTH // HEAD_DIM
    cos_t = jnp.tile(jnp.concatenate([cos, cos], axis=-1), (1, heads))
    sin_t = jnp.tile(jnp.concatenate([-sin, sin], axis=-1), (1, heads))
    return cos_t, sin_t


def _out_proj_kernel(ya_ref, yb_ref, x_ref, gate_ref, w_ref, fg_ref, o_ref, *, final_norm):
    W = HALF_WIDTH
    out = _dot(ya_ref[...].astype(BF16), w_ref[0:W, :]) + _dot(yb_ref[...].astype(BF16), w_ref[W:2 * W, :])
    y = x_ref[...] + gate_ref[0] * out
    if final_norm:
        ms = jnp.mean(y * y, axis=-1, keepdims=True)
        y = y * lax.rsqrt(ms + NORM_EPS) * fg_ref[...]
    o_ref[...] = y


def _out_proj(ya, yb, x, gate, w_out, final_g, tm, final_norm):
    rows, d = x.shape
    W = HALF_WIDTH
    nt = rows // tm
    groups = gate.shape[0]
    tiles_per_group = nt // groups
    rg = gate.shape[1]
    return pl.pallas_call(
        functools.partial(_out_proj_kernel, final_norm=final_norm),
        grid=(nt,),
        in_specs=[pl.BlockSpec((tm, W), lambda i: (i, 0)),
                  pl.BlockSpec((tm, W), lambda i: (i, 0)),
                  pl.BlockSpec((tm, d), lambda i: (i, 0)),
                  pl.BlockSpec((1, rg, d), lambda i: (i // tiles_per_group, 0, 0)),
                  pl.BlockSpec((2 * W, d), lambda i: (0, 0)),
                  pl.BlockSpec((1, d), lambda i: (0, 0))],
        out_specs=pl.BlockSpec((tm, d), lambda i: (i, 0)),
        out_shape=jax.ShapeDtypeStruct((rows, d), F32),
        compiler_params=_params(("arbitrary",)),
        name="out_proj",
    )(ya, yb, x, gate, w_out.astype(BF16), final_g.reshape(1, d))


def _sb_prompt_kernel(q_ref, k_ref, v_ref, zs_ref, o_ref, knorm_ref, *, tq):
    qi = pl.program_id(2)
    q = q_ref[...]
    lane = _iota((tq, LANES), 1)
    head0 = lane < HEAD_DIM
    qs = (jnp.where(head0, q, jnp.zeros_like(q)), jnp.where(head0, jnp.zeros_like(q), q))

    def head_sq_norms(x):
        sq = x.astype(F32) * x.astype(F32)
        return (jnp.sum(jnp.where(head0, sq, 0.0), axis=1, keepdims=True),
                jnp.sum(jnp.where(head0, 0.0, sq), axis=1, keepdims=True))

    @pl.when(qi == 0)
    def _():
        def scan(j, m):
            n0, n1 = head_sq_norms(k_ref[pl.ds(pl.multiple_of(j * tq, tq), tq), :])
            return (jnp.maximum(m[0], jnp.max(n0, axis=0, keepdims=True)),
                    jnp.maximum(m[1], jnp.max(n1, axis=0, keepdims=True)))
        zero = jnp.zeros((1, 1), F32)
        m0, m1 = lax.fori_loop(0, k_ref.shape[0] // tq, scan, (zero, zero))
        knorm_ref[...] = jnp.where(_iota((1, LANES), 1) == 0, m0, m1)

    qn0, qn1 = head_sq_norms(q)
    zb0 = jnp.sqrt(qn0 * knorm_ref[0:1, 0:1]) * SB_BOUND_SLACK
    zb1 = jnp.sqrt(qn1 * knorm_ref[0:1, 1:2]) * SB_BOUND_SLACK
    row = _iota((tq, tq), 0)
    col = _iota((tq, tq), 1)
    suffix = jnp.where(row >= col, 1.0, 0.0).astype(BF16)
    causal = col < row

    def blocks(items, carry):
        acc, c0, c1 = carry
        loaded, parts = [], []
        for j, diag, valid in items:
            start = pl.multiple_of(j * tq, tq)
            kb = k_ref[pl.ds(start, tq), :]
            loaded.append((v_ref[pl.ds(start, tq), :], diag, valid))
            per_head = []
            for qh in qs:
                z = _dot_nt(qh, kb)
                ls = -_softplus_scores(z)
                if diag:
                    ls = jnp.where(causal, ls, 0.0)
                hi, lo = _split_bf16(ls)
                per_head.append((z, _dot(hi, suffix) + _dot(lo, suffix)))
            parts.append(per_head)
        cs = [c0, c1]
        for (vb, diag, valid), per_head in zip(loaded, parts):
            pvs = []
            for h, (z, s) in enumerate(per_head):
                w = jnp.exp(z + s + cs[h])
                if diag:
                    w = jnp.where(causal, w, 0.0)
                total = s[:, 0:1]
                if valid is not None:
                    w = jnp.where(valid, w, 0.0)
                    total = jnp.where(valid, total, 0.0)
                pvs.append(_dot(w.astype(BF16), vb))
                cs[h] = cs[h] + total
            acc = acc + jnp.where(head0, pvs[0], pvs[1])
        return acc, cs[0], cs[1]

    def exhausted(c0, c1):
        worst = jnp.max(jnp.maximum(zb0 + c0, zb1 + c1))
        return (worst < SB_EXP_UNDERFLOW).astype(jnp.int32)

    def step(state):
        j, _, acc, c0, c1 = state
        acc, c0, c1 = blocks([(j, False, None)], (acc, c0, c1))
        return j - 1, exhausted(c0, c1), acc, c0, c1

    zero_c = jnp.zeros((tq, 1), F32)
    acc, c0, c1 = blocks([(qi, True, None), (jnp.maximum(qi - 1, 0), False, qi > 0)],
                         (jnp.zeros((tq, LANES), F32), zero_c, zero_c))
    state = lax.while_loop(lambda s: jnp.logical_and(s[0] >= 0, s[1] == 0), step,
                           (qi - 2, exhausted(c0, c1), acc, c0, c1))
    o_ref[...] = state[2] * zs_ref[...]


def _sb_prompt(q, k, v, zs, batch, seq, tq):
    rows, W = q.shape
    nq = seq // tq
    pairs = W // LANES
    return pl.pallas_call(
        functools.partial(_sb_prompt_kernel, tq=tq),
        grid=(batch, pairs, nq),
        in_specs=[pl.BlockSpec((tq, LANES), lambda b, p, i: (b * nq + i, p)),
                  pl.BlockSpec((seq, LANES), lambda b, p, i: (b, p)),
                  pl.BlockSpec((seq, LANES), lambda b, p, i: (b, p)),
                  pl.BlockSpec((tq, LANES), lambda b, p, i: (b * nq + i, p))],
        out_specs=pl.BlockSpec((tq, LANES), lambda b, p, i: (b * nq + i, p)),
        out_shape=jax.ShapeDtypeStruct((rows, W), F32),
        scratch_shapes=[pltpu.VMEM((1, LANES), F32)],
        compiler_params=_params(("arbitrary", "arbitrary", "arbitrary")),
        name="sb_prompt",
    )(q, k, v, zs)


def _moba_prompt_kernel(q_ref, qf_ref, k_ref, v_ref, km_ref, zs_ref, o_ref, kpad_ref, *, tq, nb):
    qi = pl.program_id(2)

    @pl.when(qi == 0)
    def _():
        kpad_ref[...] = jnp.zeros_like(kpad_ref)
        kpad_ref[0:nb, :] = km_ref[0]

    q = q_ref[...]
    qf = qf_ref[...]
    lane = _iota((tq, LANES), 1)
    lane_f = lane.astype(F32)
    head0 = lane < HEAD_DIM
    row = _iota((tq, tq), 0)
    col = _iota((tq, tq), 1)
    kmean = kpad_ref[...]

    qs, sels = [], []
    for h in range(2):
        hm = head0 if h == 0 else jnp.logical_not(head0)
        qs.append(jnp.where(hm, q, jnp.zeros_like(q)))
        g = _dot_nt(jnp.where(hm, qf, 0.0), kmean, HIGHEST)
        g = jnp.where(lane < qi, g, NEG_INF)
        sel = jnp.zeros((tq, LANES), F32)
        for _ in range(MOBA_TOPK):
            mx = jnp.max(g, axis=1, keepdims=True)
            is_max = jnp.logical_and(g == mx, mx > NEG_INF)
            idx = jnp.min(jnp.where(is_max, lane_f, float(LANES)), axis=1, keepdims=True)
            pick = lane_f == idx
            sel = jnp.where(pick, 1.0, sel)
            g = jnp.where(pick, NEG_INF, g)
        sels.append(sel)

    flags =[jnp.where(sels[h] > 0.0, 0.0, 1.0) for h in range(2)]
    lhs = [jnp.where(head0, qs[0], pltpu.roll(flags[0], HEAD_DIM, 1).astype(BF16)),
           jnp.where(head0, flags[1].astype(BF16), qs[1])]
    ones = jnp.ones((tq, LANES), BF16)

    def values(vb):
        return jnp.where(head0, vb, ones), jnp.where(head0, ones, vb)

    start = pl.multiple_of(qi * tq, tq)
    kb = k_ref[pl.ds(start, tq), :]
    vmods = values(v_ref[pl.ds(start, tq), :])
    state = []
    for h in range(2):
        s = jnp.where(col <= row, _dot_nt(qs[h], kb), NEG_INF)
        m = jnp.max(s, axis=1, keepdims=True)
        p = jnp.exp(s - m)
        state += [m, _dot(p.astype(BF16), vmods[h])]

    def blocks(ns, st):
        ks, vs = [], []
        for n in ns:
            start_n = pl.multiple_of(n * tq, tq)
            kn = k_ref[pl.ds(start_n, tq), :]
            pen = [jnp.where(lane == n + off, MOBA_MASK_PENALTY, 0.0).astype(BF16) for off in (HEAD_DIM, 0)]
            ks.append((jnp.where(head0, kn, pen[0]), jnp.where(head0, pen[1], kn)))
            vs.append(values(v_ref[pl.ds(start_n, tq), :]))
        ss = [[_dot_nt(lhs[h], kn[h]) for kn in ks] for h in range(2)]
        new = []
        for h in range(2):
            m, acc = st[2 * h:2 * h + 2]
            m_new = m
            for s in ss[h]:
                m_new = jnp.maximum(m_new, jnp.max(s, axis=1, keepdims=True))
            ps = [jnp.exp(s - m_new).astype(BF16) for s in ss[h]]
            acc = jnp.exp(m - m_new) * acc
            for p, vn in zip(ps, vs):
                acc = acc + _dot(p, vn[h])
            new += [m_new, acc]
        return tuple(new)

    group = MOBA_GROUP if nb % MOBA_GROUP == 0 else 1
    st = lax.fori_loop(0, (qi + group - 1) // group,
                       lambda i, s: blocks([group * i + r for r in range(group)], s), tuple(state))
    o = jnp.where(head0, st[1] / pltpu.roll(st[1], HEAD_DIM, 1), st[3] / pltpu.roll(st[3], HEAD_DIM, 1))
    o_ref[...] = o * zs_ref[...]


def _moba_prompt(q, qf, k, v, kmean, zs, batch, seq):
    rows, W = q.shape
    tq = MOBA_BLOCK
    nq = seq // tq
    pairs = W // LANES
    assert nq <= HEAD_DIM
    return pl.pallas_call(
        functools.partial(_moba_prompt_kernel, tq=tq, nb=nq),
        grid=(batch, pairs, nq),
        in_specs=[pl.BlockSpec((tq, LANES), lambda b, p, i: (b * nq + i, p)),
                  pl.BlockSpec((tq, LANES), lambda b, p, i: (b * nq + i, p)),
                  pl.BlockSpec((seq, LANES), lambda b, p, i: (b, p)),
                  pl.BlockSpec((seq, LANES), lambda b, p, i: (b, p)),
                  pl.BlockSpec((1, nq, LANES), lambda b, p, i: (b, 0, p)),
                  pl.BlockSpec((tq, LANES), lambda b, p, i: (b * nq + i, p))],
        out_specs=pl.BlockSpec((tq, LANES), lambda b, p, i: (b * nq + i, p)),
        out_shape=jax.ShapeDtypeStruct((rows, W), F32),
        scratch_shapes=[pltpu.VMEM((LANES, LANES), F32)],
        compiler_params=_params(("arbitrary", "arbitrary", "arbitrary")),
        name="moba_prompt",
    )(q, qf, k, v, kmean.reshape(batch, nq, W), zs)


def _mlstm_kernel(qk_ref, v_ref, gb_ref, gt_ref, conv0_ref, c0_ref, m0_ref,
                  cw_ref, cb_ref, brow_ref,
                  y_ref, conv_out_ref, c_out_ref, m_out_ref,
                  xpad_ref, caug_ref, m_ref, *, L):
    c = pl.program_id(1)
    nc = pl.num_programs(1)
    H, DK = ML_HEADS, ML_HEAD_DIM
    W = HALF_WIDTH
    PADR = SUBLANES

    @pl.when(c == 0)
    def _():
        xpad_ref[0:PADR, :] = conv0_ref[0]
        caug_ref[...] = c0_ref[0]
        m_ref[...] = m0_ref[0]

    xpad_ref[PADR:PADR + L, :] = qk_ref[...]
    y = cb_ref[...]
    for j in range(CONV_WIDTH):
        off = PADR - (CONV_WIDTH - 1) + j
        y = y + cw_ref[j:j + 1, :] * xpad_ref[off:off + L, :]
    tail = xpad_ref[L:L + PADR, :]
    xpad_ref[0:PADR, :] = tail
    conv_out_ref[0] = tail
    qk = _silu(y)

    lane = _iota((L, LANES), 1)
    is_f_col = jnp.logical_and(lane >= H, lane < 2 * H)
    gcol = gt_ref[0] + brow_ref[...]
    lf_col = jnp.where(is_f_col, _log_sigmoid(gcol), 0.0)
    grow = jnp.transpose(gcol)[0:2 * H, :]
    sub = _iota((2 * H, L), 0)
    lf_row = jnp.where(sub >= H, _log_sigmoid(grow), 0.0)
    row = _iota((L, L), 0)
    col = _iota((L, L), 1)
    causal = col <= row
    tri = jnp.where(causal, 1.0, 0.0)
    f_col = _dot(tri, lf_col, HIGHEST)
    f_row = _dot_nt(lf_row, tri, HIGHEST)
    ones_col = jnp.where(_iota((L, DK), 1) == 0, 1.0, 0.0).astype(BF16)
    m_all = m_ref[...]
    m_next = m_all
    lane1 = _iota((1, LANES), 1)

    for h in range(H):
        fc = f_col[:, H + h:H + h + 1]
        fr = f_row[H + h:H + h + 1, :]
        li_c = gcol[:, h:h + 1]
        li_r = grow[h:h + 1, :]
        m_prev = m_all[:, h:h + 1]
        inter = fc + m_prev
        intra = jnp.where(causal, fc - fr + li_r, NEG_INF)
        mt = jnp.maximum(inter, jnp.max(intra, axis=1, keepdims=True))
        w = jnp.exp(intra - mt)
        g = jnp.exp(inter - mt)
        qh = qk[:, h * DK:(h + 1) * DK].astype(BF16)
        kf = qk[:, W + h * DK:W + (h + 1) * DK] * (DK ** -0.5)
        kh = kf.astype(BF16)
        vaug = jnp.concatenate([v_ref[:, h * DK:(h + 1) * DK].astype(BF16), ones_col], axis=1)
        s = _dot_nt(qh, kh) * w
        nd = g * _dot(qh, caug_ref[h].astype(BF16)) + _dot(s.astype(BF16), vaug)
        den = nd[:, DK:DK + 1]
        hout = nd[:, 0:DK] / jnp.maximum(jnp.abs(den), jnp.exp(-mt))
        y_ref[:, h * DK:(h + 1) * DK] = gb_ref[:, h * DK:(h + 1) * DK] * hout
        m_new = mt[L - 1:L, :]
        f_last = fc[L - 1:L, :]
        decay = jnp.exp(f_last + m_prev - m_new)
        ws = jnp.exp(f_last - fc + li_c - m_new)
        caug_ref[h] = decay * caug_ref[h] + _dot_tn((kf * ws).astype(BF16), vaug)
        m_next = jnp.where(lane1 == h, m_new, m_next)

    m_ref[...] = m_next

    @pl.when(c == nc - 1)
    def _():
        c_out_ref[0] = caug_ref[...]
        m_out_ref[0] = m_next


def _mlstm(qk_pre, vb, gb, gates, conv0, c0aug, m0, conv_w, conv_b, b_ig, b_fg, batch, L):
    rows, W2 = qk_pre.shape
    W = HALF_WIDTH
    H = ML_HEADS
    nc = rows // batch // L
    brow = jnp.pad(jnp.concatenate([b_ig, b_fg]).reshape(1, 2 * H), ((0, 0), (0, LANES - 2 * H)))
    out_shapes = [jax.ShapeDtypeStruct((rows, W), F32),
                  jax.ShapeDtypeStruct((batch, SUBLANES, W2), F32),
                  jax.ShapeDtypeStruct((batch, H, ML_HEAD_DIM, 2 * ML_HEAD_DIM), F32),
                  jax.ShapeDtypeStruct((batch, 1, LANES), F32)]
    return pl.pallas_call(
        functools.partial(_mlstm_kernel, L=L),
        grid=(batch, nc),
        in_specs=[pl.BlockSpec((L, W2), lambda b, c: (b * nc + c, 0)),
                  pl.BlockSpec((L, W), lambda b, c: (b * nc + c, 0)),
                  pl.BlockSpec((L, W), lambda b, c: (b * nc + c, 0)),
                  pl.BlockSpec((1, L, LANES), lambda b, c: (b * nc + c, 0, 0)),
                  pl.BlockSpec((1, SUBLANES, W2), lambda b, c: (b, 0, 0)),
                  pl.BlockSpec((1, H, ML_HEAD_DIM, 2 * ML_HEAD_DIM), lambda b, c: (b, 0, 0, 0)),
                  pl.BlockSpec((1, 1, LANES), lambda b, c: (b, 0, 0)),
                  pl.BlockSpec((CONV_WIDTH, W2), lambda b, c: (0, 0)),
                  pl.BlockSpec((1, W2), lambda b, c: (0, 0)),
                  pl.BlockSpec((1, LANES), lambda b, c: (0, 0))],
        out_specs=[pl.BlockSpec((L, W), lambda b, c: (b * nc + c, 0)),
                   pl.BlockSpec((1, SUBLANES, W2), lambda b, c: (b, 0, 0)),
                   pl.BlockSpec((1, H, ML_HEAD_DIM, 2 * ML_HEAD_DIM), lambda b, c: (b, 0, 0, 0)),
                   pl.BlockSpec((1, 1, LANES), lambda b, c: (b, 0, 0))],
        out_shape=out_shapes,
        scratch_shapes=[pltpu.VMEM((L + SUBLANES, W2), F32),
                        pltpu.VMEM((H, ML_HEAD_DIM, 2 * ML_HEAD_DIM), F32),
                        pltpu.VMEM((1, LANES), F32)],
        compiler_params=_params(("arbitrary", "arbitrary")),
        name="mlstm",
    )(qk_pre, vb, gb, gates, conv0, c0aug, m0, conv_w, conv_b.reshape(1, W2), brow)


def _rglru_gates(xc, wr_ref, br_ref, wi_ref, bi_ref, lam_ref):
    xb = xc.astype(BF16)
    r = _sigmoid(_dot(xb, wr_ref[...]) + br_ref[...])
    i = _sigmoid(_dot(xb, wi_ref[...]) + bi_ref[...])
    log_a = RG_C * r * _log_sigmoid(lam_ref[...])
    a = jnp.exp(log_a)
    b = jnp.sqrt(-jnp.tanh(log_a) * (a * a + 1.0)) * (i * xc)
    return a, b


def _rglru_prompt_kernel(xd_ref, zs_ref, conv0_ref, h0_ref, cw_ref, cb_ref, wr_ref, br_ref, wi_ref, bi_ref,
                         lam_ref, y_ref, conv_out_ref, h_out_ref,
                         xpad_ref, a_ref, b_ref, hs_ref, hc_ref, *, L):
    c = pl.program_id(1)
    PADR = SUBLANES

    @pl.when(c == 0)
    def _():
        xpad_ref[0:PADR, :] = conv0_ref[0]
        hc_ref[...] = h0_ref[0]

    xpad_ref[PADR:PADR + L, :] = xd_ref[...]
    xc = cb_ref[...]
    for j in range(CONV_WIDTH):
        off = PADR - (CONV_WIDTH - 1) + j
        xc = xc + cw_ref[j:j + 1, :] * xpad_ref[off:off + L, :]
    tail = xpad_ref[L:L + PADR, :]
    xpad_ref[0:PADR, :] = tail
    conv_out_ref[0] = tail

    a, b = _rglru_gates(xc, wr_ref, br_ref, wi_ref, bi_ref, lam_ref)
    a_ref[...] = a
    b_ref[...] = b

    def step(t, h):
        h = a_ref[pl.ds(t, 1), :] * h + b_ref[pl.ds(t, 1), :]
        hs_ref[pl.ds(t, 1), :] = h
        return h

    h = lax.fori_loop(0, L, step, hc_ref[...], unroll=8)
    hc_ref[...] = h
    h_out_ref[0] = h
    y_ref[...] = hs_ref[...] * zs_ref[...]


def _block_diag(w):
    g, n, _ = w.shape
    eye = jnp.eye(g, dtype=w.dtype)
    return (eye[:, None, :, None] * w[:, :, None, :]).reshape(g * n, g * n)


def _rglru_prompt(xd, zs, conv0, h0, conv_w, conv_b, wr, br, wi, bi, lam, batch, L):
    rows, W = xd.shape
    nc = rows // batch // L
    vec = lambda a: a.reshape(1, W)
    const = lambda shape: pl.BlockSpec(shape, lambda b, c: tuple(0 for _ in shape))
    return pl.pallas_call(
        functools.partial(_rglru_prompt_kernel, L=L),
        grid=(batch, nc),
        in_specs=[pl.BlockSpec((L, W), lambda b, c: (b * nc + c, 0)),
                  pl.BlockSpec((L, W), lambda b, c: (b * nc + c, 0)),
                  pl.BlockSpec((1, SUBLANES, W), lambda b, c: (b, 0, 0)),
                  pl.BlockSpec((1, 1, W), lambda b, c: (b, 0, 0)),
                  const((CONV_WIDTH, W)), const((1, W)), const((W, W)), const((1, W)), const((W, W)),
                  const((1, W)), const((1, W))],
        out_specs=[pl.BlockSpec((L, W), lambda b, c: (b * nc + c, 0)),
                   pl.BlockSpec((1, SUBLANES, W), lambda b, c: (b, 0, 0)),
                   pl.BlockSpec((1, 1, W), lambda b, c: (b, 0, 0))],
        out_shape=[jax.ShapeDtypeStruct((rows, W), F32),
                   jax.ShapeDtypeStruct((batch, SUBLANES, W), F32),
                   jax.ShapeDtypeStruct((batch, 1, W), F32)],
        scratch_shapes=[pltpu.VMEM((L + SUBLANES, W), F32), pltpu.VMEM((L, W), F32), pltpu.VMEM((L, W), F32),
                        pltpu.VMEM((L, W), F32), pltpu.VMEM((1, W), F32)],
        compiler_params=_params(("arbitrary", "arbitrary")),
        name="rglru_prompt",
    )(xd, zs, conv0, h0, conv_w, vec(conv_b), _block_diag(wr).astype(BF16), vec(br),
      _block_diag(wi).astype(BF16), vec(bi), vec(lam))


def _rglru_sample_kernel(xd_ref, zs_ref, conv0_ref, h0_ref, cw_ref, cb_ref, wr_ref, br_ref, wi_ref, bi_ref,
                         lam_ref, y_ref, h_out_ref, *, T):
    xs = [conv0_ref[j] for j in range(CONV_WIDTH - 1)] + [xd_ref[t] for t in range(T)]
    h = h0_ref[...]
    for t in range(T):
        xc = cb_ref[...]
        for j in range(CONV_WIDTH):
            xc = xc + cw_ref[j:j + 1, :] * xs[t + j]
        a, b = _rglru_gates(xc, wr_ref, br_ref, wi_ref, bi_ref, lam_ref)
        h = a * h + b
        y_ref[t] = h * zs_ref[t]
    h_out_ref[...] = h


def _rglru_sample(xd, zs, conv0, h0, conv_w, conv_b, wr, br, wi, bi, lam):
    T, B, W = xd.shape
    vec = lambda a: a.reshape(1, W)
    return pl.pallas_call(
        functools.partial(_rglru_sample_kernel, T=T),
        out_shape=[jax.ShapeDtypeStruct((T, B, W), F32), jax.ShapeDtypeStruct((B, W), F32)],
        compiler_params=pltpu.CompilerParams(vmem_limit_bytes=VMEM_LIMIT),
        name="rglru_sample",
    )(xd, zs, conv0, h0, conv_w, vec(conv_b), _block_diag(wr).astype(BF16), vec(br),
      _block_diag(wi).astype(BF16), vec(bi), vec(lam))


def _query_rows(q):
    b, t, w = q.shape
    heads = w // HEAD_DIM
    onehot = (jnp.arange(heads)[:, None] == (jnp.arange(w) // HEAD_DIM)[None, :]).astype(q.dtype)
    return (q[:, :, None, :] * onehot[None, None]).reshape(b, t * heads, w)


def _channel_major(a, lanes):
    return jnp.pad(jnp.transpose(a, (0, 2, 1)), ((0, 0), (0, 0), (0, lanes - a.shape[1])))


def _pool_channel_major(pool):
    n_pool, page_rows, heads, hd = pool.shape
    return jnp.transpose(pool, (0, 2, 3, 1)).reshape(n_pool, heads * hd, page_rows)


def _page_specs(pages_per_step, page_rows, width, page_of):
    def spec(i):
        return pl.BlockSpec((None, width, page_rows), lambda b, c, pt: (pt[b, page_of(c, i)], 0, 0))
    return [spec(i) for i in range(pages_per_step)]


def _head_diagonal(o, t_new, zs):
    heads = HALF_WIDTH // HEAD_DIM
    rows = t_new * heads
    keep = (_iota((rows, HALF_WIDTH), 1) // HEAD_DIM) == (_iota((rows, HALF_WIDTH), 0) % heads)
    o = jnp.where(keep, o, 0.0)
    return jnp.sum(o.reshape(t_new, heads, HALF_WIDTH), axis=1) * zs


def _key_absmax_kernel(pt_ref, *rest, P):
    kpages, o_ref = rest[:P], rest[P]
    c = pl.program_id(1)
    m = jnp.abs(kpages[0][...])
    for i in range(1, P):
        m = jnp.maximum(m, jnp.abs(kpages[i][...]))

    @pl.when(c == 0)
    def _():
        o_ref[0] = m

    @pl.when(c > 0)
    def _():
        o_ref[0] = jnp.maximum(o_ref[0], m)


def _key_absmax(page_table, k_pool, P):
    B, n_pages = page_table.shape
    W, page_rows = k_pool.shape[1], k_pool.shape[2]
    grid_spec = pltpu.PrefetchScalarGridSpec(
        num_scalar_prefetch=1,
        grid=(B, n_pages // P),
        in_specs=_page_specs(P, page_rows, W, lambda c, i: c * P + i),
        out_specs=pl.BlockSpec((1, W, page_rows), lambda b, c, pt: (b, 0, 0)),
    )
    return pl.pallas_call(
        functools.partial(_key_absmax_kernel, P=P),
        grid_spec=grid_spec,
        out_shape=jax.ShapeDtypeStruct((B, W, page_rows), F32),
        compiler_params=_params(("arbitrary", "arbitrary")),
        name="key_absmax",
    )(page_table, *([k_pool] * P))


def _sb_sample_kernel(pt_ref, q_ref, knew_ref, vnew_ref, zs_ref, kabs_ref, acc0_ref, car0_ref, *rest,
                      P, t_new, newest):
    kpages, vpages = rest[:P], rest[P:2 * P]
    o_ref, acc_out_ref, car_out_ref, done_ref, acc_ref, car_ref = rest[2 * P:]
    c = pl.program_id(1)
    nch = pl.num_programs(1)
    q = q_ref[0]
    heads = HALF_WIDTH // HEAD_DIM
    rows = t_new * heads

    def key_blocks(kts, vts, mask):
        n = kts[0].shape[1]
        suffix = jnp.where(_iota((n, n), 0) >= _iota((n, n), 1), 1.0, 0.0).astype(BF16)
        zs = [_dot(q, kt.astype(BF16)) for kt in kts]
        lss = [-_softplus_scores(z) for z in zs]
        if mask is not None:
            lss = [jnp.where(mask, ls, 0.0) for ls in lss]
        sums = []
        for ls in lss:
            hi, lo = _split_bf16(ls)
            sums.append(_dot(hi, suffix) + _dot(lo, suffix))
        carry = car_ref[...]
        acc = acc_ref[...]
        for z, s, vt in zip(zs, sums, vts):
            w = jnp.exp(z + s + carry)
            if mask is not None:
                w = jnp.where(mask, w, 0.0)
            acc = acc + _dot_nt(w.astype(BF16), vt.astype(BF16))
            carry = carry + s[:, 0:1]
        acc_ref[...] = acc
        car_ref[...] = carry

    @pl.when(c == 0)
    def _():
        if newest:
            acc_ref[...] = jnp.zeros_like(acc_ref)
            car_ref[...] = jnp.zeros_like(car_ref)
            n = knew_ref.shape[2]
            j = _iota((rows, n), 1)
            t = _iota((rows, n), 0) // heads
            key_blocks([knew_ref[0]], [vnew_ref[0]], jnp.logical_and(j < t, j < t_new))
        else:
            acc_ref[...] = acc0_ref[0]
            car_ref[...] = car0_ref[0][:, 0:1]

    key_blocks([kpages[i][...] for i in reversed(range(P))], [vpages[i][...] for i in reversed(range(P))], None)

    @pl.when(c == nch - 1)
    def _():
        o_ref[0] = _head_diagonal(acc_ref[...], t_new, zs_ref[0])
        acc_out_ref[0] = acc_ref[...]
        car_out_ref[0] = jnp.broadcast_to(car_ref[...], (rows, LANES))
        bound = jnp.max(_dot(jnp.abs(q.astype(F32)), kabs_ref[0], HIGHEST), axis=1, keepdims=True)
        worst = jnp.max(bound * SB_BOUND_SLACK + car_ref[...])
        done_ref[0] = jnp.where(worst < SB_EXP_UNDERFLOW, 1.0, 0.0) * jnp.ones((1, LANES), F32)


def _sb_sample_pages(page_table, q_rows, knew_t, vnew_t, zs, kabs, acc0, car0, k_pool, v_pool, P, first_page,
                     n_used, newest):
    B = page_table.shape[0]
    t_new = zs.shape[1]
    W = HALF_WIDTH
    rows = q_rows.shape[1]
    page_rows = k_pool.shape[2]
    nch = n_used // P
    page_of = lambda c, i: first_page + (nch - 1 - c) * P + i
    per_b = lambda shape: pl.BlockSpec((1,) + shape, lambda b, c, pt: (b,) + tuple(0 for _ in shape))
    grid_spec = pltpu.PrefetchScalarGridSpec(
        num_scalar_prefetch=1,
        grid=(B, nch),
        in_specs=[per_b((rows, W)), per_b(knew_t.shape[1:]), per_b(vnew_t.shape[1:]), per_b((t_new, W)),
                  per_b((W, page_rows)), per_b((rows, W)), per_b((rows, LANES))]
        + _page_specs(P, page_rows, W, page_of) + _page_specs(P, page_rows, W, page_of),
        out_specs=[per_b((t_new, W)), per_b((rows, W)), per_b((rows, LANES)), per_b((1, LANES))],
        scratch_shapes=[pltpu.VMEM((rows, W), F32), pltpu.VMEM((rows, 1), F32)],
    )
    return pl.pallas_call(
        functools.partial(_sb_sample_kernel, P=P, t_new=t_new, newest=newest),
        grid_spec=grid_spec,
        out_shape=[jax.ShapeDtypeStruct((B, t_new, W), F32), jax.ShapeDtypeStruct((B, rows, W), F32),
                   jax.ShapeDtypeStruct((B, rows, LANES), F32), jax.ShapeDtypeStruct((B, 1, LANES), F32)],
        compiler_params=_params(("arbitrary", "arbitrary")),
        name="sb_sample_newest" if newest else "sb_sample_older",
    )(page_table, q_rows, knew_t, vnew_t, zs, kabs, acc0, car0, *([k_pool] * P), *([v_pool] * P))


def _sb_sample(page_table, q_rows, knew_t, vnew_t, zs, k_pool, v_pool, p_scan, p_newest):
    B, n_pages = page_table.shape
    rows = q_rows.shape[1]
    p_newest = min(p_newest, n_pages)
    kabs = _key_absmax(page_table, k_pool, p_scan)
    y, acc, car, done = _sb_sample_pages(
        page_table, q_rows, knew_t, vnew_t, zs, kabs, jnp.zeros((B, rows, HALF_WIDTH), F32),
        jnp.zeros((B, rows, LANES), F32), k_pool, v_pool, p_newest, n_pages - p_newest, p_newest, True)
    n_older = n_pages - p_newest
    if n_older == 0:
        return y
    p_older = max(p for p in range(1, p_scan + 1) if n_older % p == 0)
    older = lambda: _sb_sample_pages(page_table, q_rows, knew_t, vnew_t, zs, kabs, acc, car, k_pool, v_pool,
                                     p_older, 0, n_older, False)[0]
    return lax.cond(jnp.min(done) > 0.5, lambda: y, older)


def _moba_sample_kernel(pt_ref, q_ref, qg_ref, knew_ref, vnew_ref, zs_ref, *rest, P, t_new, nb, pb):
    kpages, vpages = rest[:P], rest[P:2 * P]
    o_ref, m_ref, l_ref, acc_ref, km_ref = rest[2 * P:]
    c = pl.program_id(1)
    nch = pl.num_programs(1)
    q = q_ref[0]
    heads = HALF_WIDTH // HEAD_DIM
    rows = t_new * heads
    lane = _iota((rows, LANES), 1)

    @pl.when(c == 0)
    def _():
        km_ref[...] = jnp.zeros_like(km_ref)
        m_ref[...] = jnp.zeros_like(m_ref)
        l_ref[...] = jnp.zeros_like(l_ref)

    nblk = P // pb
    zs = [[_dot(q, kpages[pb * s + r][...].astype(BF16)) for r in range(pb)] for s in range(nblk)]
    km_all, m_all, l_all = km_ref[...], m_ref[...], l_ref[...]
    km_lane = _iota(km_ref.shape, 1)
    for s in range(nblk):
        n = c * nblk + s
        ksum = kpages[pb * s][...]
        for r in range(1, pb):
            ksum = ksum + kpages[pb * s + r][...]
        kmean = jnp.sum(ksum, axis=1, keepdims=True) * (1.0 / MOBA_BLOCK)
        km_all = jnp.where(km_lane == n, kmean, km_all)
        m = jnp.max(zs[s][0], axis=1, keepdims=True)
        for z in zs[s][1:]:
            m = jnp.maximum(m, jnp.max(z, axis=1, keepdims=True))
        l = jnp.zeros((rows, 1), F32)
        a = jnp.zeros((rows, HALF_WIDTH), F32)
        for r, z in enumerate(zs[s]):
            p = jnp.exp(z - m)
            l = l + jnp.sum(p, axis=1, keepdims=True)
            a = a + _dot_nt(p.astype(BF16), vpages[pb * s + r][...].astype(BF16))
        m_all = jnp.where(lane == n, m, m_all)
        l_all = jnp.where(lane == n, l, l_all)
        acc_ref[n] = a
    km_ref[...] = km_all
    m_ref[...] = m_all
    l_ref[...] = l_all

    @pl.when(c == nch - 1)
    def _():
        lane_f = lane.astype(F32)
        g = _dot(qg_ref[0], km_ref[...], HIGHEST)
        g = jnp.where(lane < nb, g, NEG_INF)
        sel = jnp.zeros((rows, LANES), F32)
        for _ in range(min(MOBA_TOPK, nb)):
            mx = jnp.max(g, axis=1, keepdims=True)
            is_max = jnp.logical_and(g == mx, mx > NEG_INF)
            idx = jnp.min(jnp.where(is_max, lane_f, float(LANES)), axis=1, keepdims=True)
            pick = lane_f == idx
            sel = jnp.where(pick, 1.0, sel)
            g = jnp.where(pick, NEG_INF, g)
        nn = knew_ref.shape[2]
        j = _iota((rows, nn), 1)
        t = _iota((rows, nn), 0) // heads
        zn = jnp.where(jnp.logical_and(j <= t, j < t_new), _dot(q, knew_ref[0].astype(BF16)), NEG_INF)
        m_all = m_ref[...]
        m_tot = jnp.maximum(jnp.max(jnp.where(sel > 0.0, m_all, NEG_INF), axis=1, keepdims=True),
                            jnp.max(zn, axis=1, keepdims=True))
        coef = jnp.where(sel > 0.0, jnp.exp(m_all - m_tot), 0.0)
        p_own = jnp.exp(zn - m_tot)
        denom = jnp.sum(coef * l_ref[...], axis=1, keepdims=True) + jnp.sum(p_own, axis=1, keepdims=True)
        o = _dot_nt(p_own.astype(BF16), vnew_ref[0].astype(BF16))
        for n in range(nb):
            o = o + coef[:, n:n + 1] * acc_ref[n]
        o_ref[0] = _head_diagonal(o / denom, t_new, zs_ref[0])


def _moba_sample(page_table, q_rows, qg_rows, knew_t, vnew_t, zs, k_pool, v_pool, P):
    B, n_pages = page_table.shape
    t_new = zs.shape[1]
    W = HALF_WIDTH
    rows = q_rows.shape[1]
    page_rows = k_pool.shape[2]
    pb = MOBA_BLOCK // page_rows
    nch = n_pages // P
    nb = n_pages // pb
    assert nb <= LANES and P % pb == 0
    page_of = lambda c, i: c * P + i
    per_b = lambda shape: pl.BlockSpec((1,) + shape, lambda b, c, pt: (b,) + tuple(0 for _ in shape))
    grid_spec = pltpu.PrefetchScalarGridSpec(
        num_scalar_prefetch=1,
        grid=(B, nch),
        in_specs=[per_b((rows, W)), per_b((rows, W)), per_b(knew_t.shape[1:]), per_b(vnew_t.shape[1:]),
                  per_b((t_new, W))]
        + _page_specs(P, page_rows, W, page_of) + _page_specs(P, page_rows, W, page_of),
        out_specs=per_b((t_new, W)),
        scratch_shapes=[pltpu.VMEM((rows, LANES), F32), pltpu.VMEM((rows, LANES), F32),
                        pltpu.VMEM((nb, rows, W), F32), pltpu.VMEM((W, LANES), F32)],
    )
    return pl.pallas_call(
        functools.partial(_moba_sample_kernel, P=P, t_new=t_new, nb=nb, pb=pb),
        grid_spec=grid_spec,
        out_shape=jax.ShapeDtypeStruct((B, t_new, W), F32),
        compiler_params=_params(("arbitrary", "arbitrary")),
        name="moba_sample",
    )(page_table, q_rows, qg_rows, knew_t, vnew_t, zs, *([k_pool] * P), *([v_pool] * P))


def _pad_rows(a, rows):
    return jnp.pad(a, ((0, 0), (0, rows - a.shape[1]), (0, 0)))


def kernel(x_prompt, x_sample, c_prompt, c_sample, page_table, cache_k_sb, cache_v_sb, state_conv_mlstm, state_c_mlstm, state_n_mlstm, state_m_mlstm, cache_k_moba, cache_v_moba, state_conv_rglru, state_h_rglru, norm_g_even, mod_w_even, mod_b_even, w_in_even, conv_w_even, conv_b_even, b_igate_even, b_fgate_even, w_out_even, norm_g_odd, mod_w_odd, mod_b_odd, w_in_odd, conv_w_odd, conv_b_odd, w_rgate_odd, b_rgate_odd, w_igate_odd, b_igate_odd, lru_lambda_odd, w_out_odd, final_g):
    B, T, D = x_prompt.shape
    DB, DT, _ = x_sample.shape
    W = HALF_WIDTH
    H = ML_HEADS
    heads = W // HEAD_DIM
    n_pool, page_rows = cache_k_sb.shape[1], cache_k_sb.shape[2]
    past = page_table.shape[1] * page_rows
    depth = norm_g_even.shape[0] + norm_g_odd.shape[0]
    TM = 256
    TP = 512 if T % 512 == 0 else TM
    n_pages = page_table.shape[1]
    PAGES_PER_STEP = max(p for p in range(2, 17, 2) if n_pages % p == 0)
    SB_NEWEST_PAGES = 4
    NEW_KEYS = LANES
    ML_PAD = SUBLANES

    xp = x_prompt.reshape(B * T, D)
    xs = x_sample.reshape(DB * DT, D)
    c_all = jnp.concatenate([c_prompt, c_sample], axis=0)
    c_rows = -(-c_all.shape[0] // SUBLANES) * SUBLANES
    c_all = jnp.pad(c_all, ((0, c_rows - c_all.shape[0]), (0, 0)))
    cos_p, sin_p = _rope_tables(jnp.arange(T))
    cos_s, sin_s = _rope_tables(past + jnp.arange(DT))
    cos_s, sin_s = jnp.tile(cos_s, (DB, 1)), jnp.tile(sin_s, (DB, 1))

    def mods(w, b):
        mod = _modulation(c_all, w, b)
        parts = []
        for part in jnp.split(mod, 3, axis=-1):
            p_part = part[:B].reshape(B, 1, D)
            s_part = jnp.repeat(part[B:B + DB], DT, axis=0).reshape(1, DB * DT, D)
            parts.append((p_part, s_part))
        return parts

    p_even, s_even, p_odd, s_odd = [], [], [], []
    y_prompt = y_sample = None
    for l in range(depth):
        j = l // 2
        last = l == depth - 1
        if l % 2 == 0:
            (sh_p, sh_s), (sc_p, sc_s), (ga_p, ga_s) = mods(mod_w_even[j], mod_b_even[j])
            qa, ka, kab, va, vab, za, qk, vb, gb, gt = _in_proj_even(
                xp, sh_p, sc_p, norm_g_even[j], w_in_even[j], TP)
            ya = _sb_prompt(qa, kab, vab, za, B, T, TM)
            yb, conv_p, caug_p, m_p = _mlstm(
                qk, vb, gb, gt.reshape(B * T // TM, TM, LANES),
                jnp.zeros((B, SUBLANES, 2 * W), F32), jnp.zeros((B, H, ML_HEAD_DIM, 2 * ML_HEAD_DIM), F32),
                jnp.zeros((B, 1, LANES), F32),
                conv_w_even[j], conv_b_even[j], b_igate_even[j], b_fgate_even[j], B, TM)
            xp = _out_proj(ya, yb, xp, ga_p, w_out_even[j], final_g, TP, last)
            p_even.append((ka.reshape(B, T, heads, HEAD_DIM), va.reshape(B, T, heads, HEAD_DIM),
                           conv_p[:, SUBLANES - (CONV_WIDTH - 1):], caug_p[..., :ML_HEAD_DIM],
                           caug_p[..., ML_HEAD_DIM], m_p[:, 0, :H]))
            qa, ka, kab, va, vab, za, qk, vb, gb, gt = _in_proj_even(
                xs, sh_s, sc_s, norm_g_even[j], w_in_even[j], DB * DT)
            ya = _sb_sample(page_table, _query_rows(qa.reshape(DB, DT, W)),
                            _channel_major(ka.reshape(DB, DT, W), NEW_KEYS),
                            _channel_major(va.reshape(DB, DT, W), NEW_KEYS), za.reshape(DB, DT, W),
                            _pool_channel_major(cache_k_sb[j]), _pool_channel_major(cache_v_sb[j]),
                            PAGES_PER_STEP, SB_NEWEST_PAGES).reshape(DB * DT, W)
            pad_t = lambda a: _pad_rows(a.reshape(DB, DT, a.shape[-1]), ML_PAD).reshape(DB * ML_PAD, a.shape[-1])
            g_rows = gt.reshape(DB, DT, LANES)[:, :, :2 * H]
            pad_gate = jnp.concatenate([jnp.full((H,), NEG_INF, F32), jnp.full((H,), jnp.inf, F32)])
            g_rows = jnp.concatenate([g_rows, jnp.broadcast_to(pad_gate, (DB, ML_PAD - DT, 2 * H))], axis=1)
            gt_s = jnp.pad(g_rows, ((0, 0), (0, 0), (0, LANES - 2 * H)))
            conv0 = jnp.pad(state_conv_mlstm[j], ((0, 0), (SUBLANES - (CONV_WIDTH - 1), 0), (0, 0)))
            c0aug = jnp.concatenate(
                [state_c_mlstm[j].astype(F32), state_n_mlstm[j].astype(F32)[..., None],
                 jnp.zeros((DB, H, ML_HEAD_DIM, ML_HEAD_DIM - 1), F32)], axis=-1)
            m0 = jnp.pad(state_m_mlstm[j].astype(F32), ((0, 0), (0, LANES - H))).reshape(DB, 1, LANES)
            yb, _, caug_s, m_s = _mlstm(
                pad_t(qk), pad_t(vb), pad_t(gb), gt_s, conv0, c0aug, m0,
                conv_w_even[j], conv_b_even[j], b_igate_even[j], b_fgate_even[j], DB, ML_PAD)
            yb = yb.reshape(DB, ML_PAD, W)[:, :DT].reshape(DB * DT, W)
            conv_s = jnp.concatenate([state_conv_mlstm[j].astype(F32), qk.reshape(DB, DT, 2 * W)],
                                     axis=1)[:, -(CONV_WIDTH - 1):]
            xs = _out_proj(ya, yb, xs, ga_s, w_out_even[j], final_g, DB * DT, last)
            s_even.append((ka.reshape(DB, DT, heads, HEAD_DIM), va.reshape(DB, DT, heads, HEAD_DIM),
                           conv_s, caug_s[..., :ML_HEAD_DIM], caug_s[..., ML_HEAD_DIM], m_s[:, 0, :H]))
        else:
            (sh_p, sh_s), (sc_p, sc_s), (ga_p, ga_s) = mods(mod_w_odd[j], mod_b_odd[j])
            lru = (conv_w_odd[j], conv_b_odd[j], w_rgate_odd[j], b_rgate_odd[j], w_igate_odd[j], b_igate_odd[j],
                   lru_lambda_odd[j])
            qb, qf, kc, kcb, vc, vcb, zc, xd, zd, km = _in_proj_odd(
                xp, sh_p, sc_p, norm_g_odd[j], w_in_odd[j], cos_p, sin_p, TP)
            yc = _moba_prompt(qb, qf, kcb, vcb, km, zc, B, T)
            yd, conv_p, h_p = _rglru_prompt(xd, zd, jnp.zeros((B, SUBLANES, W), F32), jnp.zeros((B, 1, W), F32),
                                            *lru, B, TM)
            xp = _out_proj(yc, yd, xp, ga_p, w_out_odd[j], final_g, TP, last)
            p_odd.append((kc.reshape(B, T, heads, HEAD_DIM), vc.reshape(B, T, heads, HEAD_DIM),
                          conv_p[:, SUBLANES - (CONV_WIDTH - 1):], h_p[:, 0]))
            qb, qf, kc, kcb, vc, vcb, zc, xd, zd, km = _in_proj_odd(
                xs, sh_s, sc_s, norm_g_odd[j], w_in_odd[j], cos_s, sin_s, DB * DT)
            yc = _moba_sample(page_table, _query_rows(qb.reshape(DB, DT, W)), _query_rows(qf.reshape(DB, DT, W)),
                              _channel_major(kc.reshape(DB, DT, W), NEW_KEYS),
                              _channel_major(vc.reshape(DB, DT, W), NEW_KEYS), zc.reshape(DB, DT, W),
                              _pool_channel_major(cache_k_moba[j]), _pool_channel_major(cache_v_moba[j]),
                              PAGES_PER_STEP).reshape(DB * DT, W)
            tmaj = lambda a: jnp.transpose(a.reshape(DB, DT, W), (1, 0, 2))
            yd, h_s = _rglru_sample(tmaj(xd), tmaj(zd), jnp.transpose(state_conv_rglru[j].astype(F32), (1, 0, 2)),
                                    state_h_rglru[j].astype(F32), *lru)
            yd = jnp.transpose(yd, (1, 0, 2)).reshape(DB * DT, W)
            conv_s = jnp.concatenate([state_conv_rglru[j].astype(F32), xd.reshape(DB, DT, W)],
                                     axis=1)[:, -(CONV_WIDTH - 1):]
            xs = _out_proj(yc, yd, xs, ga_s, w_out_odd[j], final_g, DB * DT, last)
            s_odd.append((kc.reshape(DB, DT, heads, HEAD_DIM), vc.reshape(DB, DT, heads, HEAD_DIM), conv_s, h_s))

    y_prompt = xp.reshape(B, T, D)
    y_sample = xs.reshape(DB, DT, D)
    stack = lambda group: [jnp.stack(a) for a in zip(*group)]
    pk_sb, pv_sb, pconv_m, pc_m, pn_m, pm_m = stack(p_even)
    sk_sb, sv_sb, sconv_m, sc_m, sn_m, sm_m = stack(s_even)
    pk_mb, pv_mb, pconv_d, ph_d = stack(p_odd)
    sk_mb, sv_mb, sconv_d, sh_d = stack(s_odd)
    return (y_prompt, y_sample,
            pk_sb, pv_sb, pconv_m, pc_m, pn_m, pm_m, pk_mb, pv_mb, pconv_d, ph_d,
            sk_sb, sv_sb, sconv_m, sc_m, sn_m, sm_m, sk_mb, sv_mb, sconv_d, sh_d)
```

```python
import functools

import jax
import jax.numpy as jnp
from jax import lax
from jax.experimental import pallas as pl
from jax.experimental.pallas import tpu as pltpu

F32 = jnp.float32
BF16 = jnp.bfloat16
HIGHEST = lax.Precision.HIGHEST

HEAD_DIM = 64
HALF_WIDTH = 512
ML_HEADS = 4
ML_HEAD_DIM = 128
CONV_WIDTH = 4
MOBA_BLOCK = 256
MOBA_TOPK = 3
RG_C = 8.0
ROPE_THETA = 10000.0
NORM_EPS = 1e-6
LANES = 128
SUBLANES = 8
VMEM_LIMIT = 56 * 1024 * 1024
NEG_INF = float("-inf")
SB_EXP_UNDERFLOW = -110.0
SB_BOUND_SLACK = 1.01
MOBA_MASK_PENALTY = -1e30
MOBA_GROUP = 4


def _params(sem):
    return pltpu.CompilerParams(dimension_semantics=sem, vmem_limit_bytes=VMEM_LIMIT)


def _sigmoid(x):
    return 1.0 / (1.0 + jnp.exp(-x))


def _silu(x):
    return x * _sigmoid(x)


def _softplus(x):
    return jnp.maximum(x, 0.0) + jnp.log1p(jnp.exp(-jnp.abs(x)))


def _softplus_scores(x):
    return jnp.maximum(x, 0.0) + jnp.log(1.0 + jnp.exp(-jnp.abs(x)))


def _log_sigmoid(x):
    return -_softplus(-x)


def _dot(a, b, precision=None):
    return jnp.dot(a, b, preferred_element_type=F32, precision=precision)


def _dot_nt(a, b, precision=None):
    return lax.dot_general(a, b, (((1,), (1,)), ((), ())), preferred_element_type=F32, precision=precision)


def _dot_tn(a, b, precision=None):
    return lax.dot_general(a, b, (((0,), (0,)), ((), ())), preferred_element_type=F32, precision=precision)


def _split_bf16(x):
    hi = x.astype(BF16)
    lo = (x - hi.astype(F32)).astype(BF16)
    return hi, lo


def _iota(shape, dim):
    return lax.broadcasted_iota(jnp.int32, shape, dim)


def _mod_kernel(c_ref, w_ref, b_ref, o_ref):
    o_ref[...] = _dot(_silu(c_ref[...]), w_ref[...], HIGHEST) + b_ref[...]


def _modulation(c, w, b):
    rows, d = c.shape
    n = w.shape[1]
    tn = 512
    return pl.pallas_call(
        _mod_kernel,
        grid=(n // tn,),
        in_specs=[pl.BlockSpec((rows, d), lambda j: (0, 0)),
                  pl.BlockSpec((d, tn), lambda j: (0, j)),
                  pl.BlockSpec((1, tn), lambda j: (0, j))],
        out_specs=pl.BlockSpec((rows, tn), lambda j: (0, j)),
        out_shape=jax.ShapeDtypeStruct((rows, n), F32),
        compiler_params=_params(("arbitrary",)),
        name="modulation",
    )(c, w, b.reshape(1, n))


def _normed(x_ref, shift_ref, scale_ref, g_ref):
    x = x_ref[...]
    ms = jnp.mean(x * x, axis=-1, keepdims=True)
    h = x * lax.rsqrt(ms + NORM_EPS) * g_ref[...]
    return h * (1.0 + scale_ref[0]) + shift_ref[0]


def _in_even_kernel(x_ref, shift_ref, scale_ref, g_ref, w_ref, wg_hi_ref, wg_lo_ref,
                    qa_ref, ka_ref, kab_ref, va_ref, vab_ref, za_ref, qk_ref, vb_ref, gb_ref, gt_ref):
    h = _normed(x_ref, shift_ref, scale_ref, g_ref)
    hb, h_lo = _split_bf16(h)
    W = HALF_WIDTH

    def proj(c):
        return _dot(hb, w_ref[:, c * W:(c + 1) * W])

    qa_ref[...] = (proj(0) * (HEAD_DIM ** -0.5)).astype(BF16)
    ka = proj(1)
    ka_ref[...] = ka
    kab_ref[...] = ka.astype(BF16)
    va = proj(2)
    va_ref[...] = va
    vab_ref[...] = va.astype(BF16)
    za_ref[...] = _silu(proj(3))
    qk_ref[:, 0:W] = proj(4)
    qk_ref[:, W:2 * W] = proj(5)
    vb_ref[...] = proj(6)
    gb_ref[...] = _sigmoid(proj(7)) * _silu(proj(8))
    gt_ref[0] = _dot(hb, wg_hi_ref[...]) + _dot(h_lo, wg_hi_ref[...]) + _dot(hb, wg_lo_ref[...])


def _in_proj_even(x, shift, scale, g, w_in, tm):
    rows, d = x.shape
    W = HALF_WIDTH
    nt = rows // tm
    groups = shift.shape[0]
    tiles_per_group = nt // groups
    w_main = w_in[:, :9 * W].astype(BF16)
    w_gate_hi, w_gate_lo = _split_bf16(jnp.pad(w_in[:, 9 * W:], ((0, 0), (0, LANES - 2 * ML_HEADS))))
    row_blk = lambda n, dt: (pl.BlockSpec((tm, n), lambda i: (i, 0)), jax.ShapeDtypeStruct((rows, n), dt))
    outs = [row_blk(W, BF16), row_blk(W, F32), row_blk(W, BF16), row_blk(W, F32), row_blk(W, BF16),
            row_blk(W, F32), row_blk(2 * W, F32), row_blk(W, F32), row_blk(W, F32),
            (pl.BlockSpec((1, tm, LANES), lambda i: (i, 0, 0)), jax.ShapeDtypeStruct((nt, tm, LANES), F32))]
    rg = shift.shape[1]
    return pl.pallas_call(
        _in_even_kernel,
        grid=(nt,),
        in_specs=[pl.BlockSpec((tm, d), lambda i: (i, 0)),
                  pl.BlockSpec((1, rg, d), lambda i: (i // tiles_per_group, 0, 0)),
                  pl.BlockSpec((1, rg, d), lambda i: (i // tiles_per_group, 0, 0)),
                  pl.BlockSpec((1, d), lambda i: (0, 0)),
                  pl.BlockSpec((d, 9 * W), lambda i: (0, 0)),
                  pl.BlockSpec((d, LANES), lambda i: (0, 0)),
                  pl.BlockSpec((d, LANES), lambda i: (0, 0))],
        out_specs=[o[0] for o in outs],
        out_shape=[o[1] for o in outs],
        compiler_params=_params(("arbitrary",)),
        name="in_proj_even",
    )(x, shift, scale, g.reshape(1, d), w_main, w_gate_hi, w_gate_lo)


def _in_odd_kernel(x_ref, shift_ref, scale_ref, g_ref, w_ref, cos_ref, sin_ref,
                   qb_ref, qf_ref, kc_ref, kcb_ref, vc_ref, vcb_ref, zc_ref, xd_ref, zd_ref, km_ref):
    h = _normed(x_ref, shift_ref, scale_ref, g_ref)
    hb = h.astype(BF16)
    W = HALF_WIDTH

    def proj(c):
        return _dot(hb, w_ref[:, c * W:(c + 1) * W])

    cos = cos_ref[...]
    sin = sin_ref[...]
    first_half = (_iota(cos.shape, 1) % HEAD_DIM) < (HEAD_DIM // 2)

    def rope(x):
        partner = jnp.where(first_half, pltpu.roll(x, W - HEAD_DIM // 2, 1), pltpu.roll(x, HEAD_DIM // 2, 1))
        return x * cos + partner * sin

    q = rope(proj(0))
    qf_ref[...] = q
    qb_ref[...] = (q * (HEAD_DIM ** -0.5)).astype(BF16)
    k = rope(proj(1))
    kc_ref[...] = k
    kcb_ref[...] = k.astype(BF16)
    blk = min(MOBA_BLOCK, k.shape[0])
    for r in range(k.shape[0] // blk):
        km_ref[r] = jnp.mean(k[r * blk:(r + 1) * blk], axis=0, keepdims=True)
    v = proj(2)
    vc_ref[...] = v
    vcb_ref[...] = v.astype(BF16)
    zc_ref[...] = _silu(proj(3))
    xd_ref[...] = proj(4)
    zd_ref[...] = _silu(proj(5))


def _in_proj_odd(x, shift, scale, g, w_in, cos, sin, tm):
    rows, d = x.shape
    W = HALF_WIDTH
    nt = rows // tm
    nkm = max(tm // MOBA_BLOCK, 1)
    groups = shift.shape[0]
    tiles_per_group = nt // groups
    rg = shift.shape[1]
    row_blk = lambda n, dt: (pl.BlockSpec((tm, n), lambda i: (i, 0)), jax.ShapeDtypeStruct((rows, n), dt))
    outs = [row_blk(W, BF16), row_blk(W, F32), row_blk(W, F32), row_blk(W, BF16), row_blk(W, F32),
            row_blk(W, BF16), row_blk(W, F32), row_blk(W, F32), row_blk(W, F32),
            (pl.BlockSpec((nkm, 1, W), lambda i: (i, 0, 0)), jax.ShapeDtypeStruct((nt * nkm, 1, W), F32))]
    return pl.pallas_call(
        _in_odd_kernel,
        grid=(nt,),
        in_specs=[pl.BlockSpec((tm, d), lambda i: (i, 0)),
                  pl.BlockSpec((1, rg, d), lambda i: (i // tiles_per_group, 0, 0)),
                  pl.BlockSpec((1, rg, d), lambda i: (i // tiles_per_group, 0, 0)),
                  pl.BlockSpec((1, d), lambda i: (0, 0)),
                  pl.BlockSpec((d, 6 * W), lambda i: (0, 0)),
                  pl.BlockSpec((tm, W), lambda i: (i % tiles_per_group, 0)),
                  pl.BlockSpec((tm, W), lambda i: (i % tiles_per_group, 0))],
        out_specs=[o[0] for o in outs],
        out_shape=[o[1] for o in outs],
        compiler_params=_params(("arbitrary",)),
        name="in_proj_odd",
    )(x, shift, scale, g.reshape(1, d), w_in.astype(BF16), cos, sin)


def _rope_tables(positions):
    half = HEAD_DIM // 2
    freqs = ROPE_THETA ** (-jnp.arange(half, dtype=F32) / half)
    ang = positions.astype(F32)[:, None] * freqs[None, :]
    cos = jnp.cos(ang)
    sin = jnp.sin(ang)
    heads = HALF_WIDTH // HEAD_DIM
    cos_t = jnp.tile(jnp.concatenate([cos, cos], axis=-1), (1, heads))
    sin_t = jnp.tile(jnp.concatenate([-sin, sin], axis=-1), (1, heads))
    return cos_t, sin_t


def _out_proj_kernel(ya_ref, yb_ref, x_ref, gate_ref, w_ref, fg_ref, o_ref, *, final_norm):
    W = HALF_WIDTH
    out = _dot(ya_ref[...].astype(BF16), w_ref[0:W, :]) + _dot(yb_ref[...].astype(BF16), w_ref[W:2 * W, :])
    y = x_ref[...] + gate_ref[0] * out
    if final_norm:
        ms = jnp.mean(y * y, axis=-1, keepdims=True)
        y = y * lax.rsqrt(ms + NORM_EPS) * fg_ref[...]
    o_ref[...] = y


def _out_proj(ya, yb, x, gate, w_out, final_g, tm, final_norm):
    rows, d = x.shape
    W = HALF_WIDTH
    nt = rows // tm
    groups = gate.shape[0]
    tiles_per_group = nt // groups
    rg = gate.shape[1]
    return pl.pallas_call(
        functools.partial(_out_proj_kernel, final_norm=final_norm),
        grid=(nt,),
        in_specs=[pl.BlockSpec((tm, W), lambda i: (i, 0)),
                  pl.BlockSpec((tm, W), lambda i: (i, 0)),
                  pl.BlockSpec((tm, d), lambda i: (i, 0)),
                  pl.BlockSpec((1, rg, d), lambda i: (i // tiles_per_group, 0, 0)),
                  pl.BlockSpec((2 * W, d), lambda i: (0, 0)),
                  pl.BlockSpec((1, d), lambda i: (0, 0))],
        out_specs=pl.BlockSpec((tm, d), lambda i: (i, 0)),
        out_shape=jax.ShapeDtypeStruct((rows, d), F32),
        compiler_params=_params(("arbitrary",)),
        name="out_proj",
    )(ya, yb, x, gate, w_out.astype(BF16), final_g.reshape(1, d))


def _sb_prompt_kernel(q_ref, k_ref, v_ref, zs_ref, o_ref, knorm_ref, *, tq):
    qi = pl.program_id(2)
    q = q_ref[...]
    lane = _iota((tq, LANES), 1)
    head0 = lane < HEAD_DIM
    qs = (jnp.where(head0, q, jnp.zeros_like(q)), jnp.where(head0, jnp.zeros_like(q), q))

    def head_sq_norms(x):
        sq = x.astype(F32) * x.astype(F32)
        return (jnp.sum(jnp.where(head0, sq, 0.0), axis=1, keepdims=True),
                jnp.sum(jnp.where(head0, 0.0, sq), axis=1, keepdims=True))

    @pl.when(qi == 0)
    def _():
        def scan(j, m):
            n0, n1 = head_sq_norms(k_ref[pl.ds(pl.multiple_of(j * tq, tq), tq), :])
            return (jnp.maximum(m[0], jnp.max(n0, axis=0, keepdims=True)),
                    jnp.maximum(m[1], jnp.max(n1, axis=0, keepdims=True)))
        zero = jnp.zeros((1, 1), F32)
        m0, m1 = lax.fori_loop(0, k_ref.shape[0] // tq, scan, (zero, zero))
        knorm_ref[...] = jnp.where(_iota((1, LANES), 1) == 0, m0, m1)

    qn0, qn1 = head_sq_norms(q)
    zb0 = jnp.sqrt(qn0 * knorm_ref[0:1, 0:1]) * SB_BOUND_SLACK
    zb1 = jnp.sqrt(qn1 * knorm_ref[0:1, 1:2]) * SB_BOUND_SLACK
    row = _iota((tq, tq), 0)
    col = _iota((tq, tq), 1)
    suffix = jnp.where(row >= col, 1.0, 0.0).astype(BF16)
    causal = col < row

    def blocks(items, carry):
        acc, c0, c1 = carry
        loaded, parts = [], []
        for j, diag, valid in items:
            start = pl.multiple_of(j * tq, tq)
            kb = k_ref[pl.ds(start, tq), :]
            loaded.append((v_ref[pl.ds(start, tq), :], diag, valid))
            per_head = []
            for qh in qs:
                z = _dot_nt(qh, kb)
                ls = -_softplus_scores(z)
                if diag:
                    ls = jnp.where(causal, ls, 0.0)
                hi, lo = _split_bf16(ls)
                per_head.append((z, _dot(hi, suffix) + _dot(lo, suffix)))
            parts.append(per_head)
        cs = [c0, c1]
        for (vb, diag, valid), per_head in zip(loaded, parts):
            pvs = []
            for h, (z, s) in enumerate(per_head):
                w = jnp.exp(z + s + cs[h])
                if diag:
                    w = jnp.where(causal, w, 0.0)
                total = s[:, 0:1]
                if valid is not None:
                    w = jnp.where(valid, w, 0.0)
                    total = jnp.where(valid, total, 0.0)
                pvs.append(_dot(w.astype(BF16), vb))
                cs[h] = cs[h] + total
            acc = acc + jnp.where(head0, pvs[0], pvs[1])
        return acc, cs[0], cs[1]

    def exhausted(c0, c1):
        worst = jnp.max(jnp.maximum(zb0 + c0, zb1 + c1))
        return (worst < SB_EXP_UNDERFLOW).astype(jnp.int32)

    def step(state):
        j, _, acc, c0, c1 = state
        acc, c0, c1 = blocks([(j, False, None)], (acc, c0, c1))
        return j - 1, exhausted(c0, c1), acc, c0, c1

    zero_c = jnp.zeros((tq, 1), F32)
    acc, c0, c1 = blocks([(qi, True, None), (jnp.maximum(qi - 1, 0), False, qi > 0)],
                         (jnp.zeros((tq, LANES), F32), zero_c, zero_c))
    state = lax.while_loop(lambda s: jnp.logical_and(s[0] >= 0, s[1] == 0), step,
                           (qi - 2, exhausted(c0, c1), acc, c0, c1))
    o_ref[...] = (state[2] * zs_ref[...]).astype(o_ref.dtype)


def _sb_prompt(q, k, v, zs, batch, seq, tq):
    rows, W = q.shape
    nq = seq // tq
    pairs = W // LANES
    return pl.pallas_call(
        functools.partial(_sb_prompt_kernel, tq=tq),
        grid=(batch, pairs, nq),
        in_specs=[pl.BlockSpec((tq, LANES), lambda b, p, i: (b * nq + i, p)),
                  pl.BlockSpec((seq, LANES), lambda b, p, i: (b, p)),
                  pl.BlockSpec((seq, LANES), lambda b, p, i: (b, p)),
                  pl.BlockSpec((tq, LANES), lambda b, p, i: (b * nq + i, p))],
        out_specs=pl.BlockSpec((tq, LANES), lambda b, p, i: (b * nq + i, p)),
        out_shape=jax.ShapeDtypeStruct((rows, W), BF16),
        scratch_shapes=[pltpu.VMEM((1, LANES), F32)],
        compiler_params=_params(("arbitrary", "arbitrary", "arbitrary")),
        name="sb_prompt",
    )(q, k, v, zs)


def _moba_prompt_kernel(q_ref, qf_ref, k_ref, v_ref, km_ref, zs_ref, o_ref, kpad_ref, *, tq, nb):
    qi = pl.program_id(2)

    @pl.when(qi == 0)
    def _():
        kpad_ref[...] = jnp.zeros_like(kpad_ref)
        kpad_ref[0:nb, :] = km_ref[0]

    q = q_ref[...]
    qf = qf_ref[...]
    lane = _iota((tq, LANES), 1)
    head0 = lane < HEAD_DIM
    row = _iota((tq, tq), 0)
    col = _iota((tq, tq), 1)
    kmean = kpad_ref[...]
    nbp = -(-nb // SUBLANES) * SUBLANES
    blk = _iota((nbp, tq), 0)
    blk_f = blk.astype(F32)

    qs, flags = [], []
    for h in range(2):
        hm = head0 if h == 0 else jnp.logical_not(head0)
        qs.append(jnp.where(hm, q, jnp.zeros_like(q)))
        g = _dot_nt(kmean, jnp.where(hm, qf, 0.0), HIGHEST)[0:nbp, :]
        g = jnp.where(blk < qi, g, NEG_INF)
        sel = jnp.zeros((nbp, tq), F32)
        for _ in range(MOBA_TOPK):
            mx = jnp.max(g, axis=0, keepdims=True)
            is_max = jnp.logical_and(g == mx, mx > NEG_INF)
            idx = jnp.min(jnp.where(is_max, blk_f, float(nbp)), axis=0, keepdims=True)
            pick = blk_f == idx
            sel = jnp.where(pick, 1.0, sel)
            g = jnp.where(pick, NEG_INF, g)
        not_sel = jnp.concatenate([jnp.where(sel > 0.0, 0.0, 1.0), jnp.ones((LANES - nbp, tq), F32)], axis=0)
        flags.append(jnp.transpose(not_sel))

    lhs = [jnp.where(head0, qs[0], pltpu.roll(flags[0], HEAD_DIM, 1).astype(BF16)),
           jnp.where(head0, flags[1].astype(BF16), qs[1])]
    ones = jnp.ones((tq, LANES), BF16)

    def values(vb):
        return jnp.where(head0, vb, ones), jnp.where(head0, ones, vb)

    start = pl.multiple_of(qi * tq, tq)
    kb = k_ref[pl.ds(start, tq), :]
    vmods = values(v_ref[pl.ds(start, tq), :])
    state = []
    for h in range(2):
        s = jnp.where(col <= row, _dot_nt(qs[h], kb), NEG_INF)
        m = jnp.max(s, axis=1, keepdims=True)
        p = jnp.exp(s - m)
        state += [m, _dot(p.astype(BF16), vmods[h])]

    def blocks(ns, st):
        ks, vs = [], []
        for n in ns:
            start_n = pl.multiple_of(n * tq, tq)
            kn = k_ref[pl.ds(start_n, tq), :]
            pen = [jnp.where(lane == n + off, MOBA_MASK_PENALTY, 0.0).astype(BF16) for off in (HEAD_DIM, 0)]
            ks.append((jnp.where(head0, kn, pen[0]), jnp.where(head0, pen[1], kn)))
            vs.append(values(v_ref[pl.ds(start_n, tq), :]))
        ss = [[_dot_nt(lhs[h], kn[h]) for kn in ks] for h in range(2)]
        new = []
        for h in range(2):
            m, acc = st[2 * h:2 * h + 2]
            m_new = m
            for s in ss[h]:
                m_new = jnp.maximum(m_new, jnp.max(s, axis=1, keepdims=True))
            ps = [jnp.exp(s - m_new).astype(BF16) for s in ss[h]]
            acc = jnp.exp(m - m_new) * acc
            for p, vn in zip(ps, vs):
                acc = acc + _dot(p, vn[h])
            new += [m_new, acc]
        return tuple(new)

    group = MOBA_GROUP if nb % MOBA_GROUP == 0 else 1
    st = lax.fori_loop(0, (qi + group - 1) // group,
                       lambda i, s: blocks([group * i + r for r in range(group)], s), tuple(state))
    o = jnp.where(head0, st[1] / pltpu.roll(st[1], HEAD_DIM, 1), st[3] / pltpu.roll(st[3], HEAD_DIM, 1))
    o_ref[...] = (o * zs_ref[...]).astype(o_ref.dtype)


def _moba_prompt(q, qf, k, v, kmean, zs, batch, seq):
    rows, W = q.shape
    tq = MOBA_BLOCK
    nq = seq // tq
    pairs = W // LANES
    assert nq <= HEAD_DIM
    return pl.pallas_call(
        functools.partial(_moba_prompt_kernel, tq=tq, nb=nq),
        grid=(batch, pairs, nq),
        in_specs=[pl.BlockSpec((tq, LANES), lambda b, p, i: (b * nq + i, p)),
                  pl.BlockSpec((tq, LANES), lambda b, p, i: (b * nq + i, p)),
                  pl.BlockSpec((seq, LANES), lambda b, p, i: (b, p)),
                  pl.BlockSpec((seq, LANES), lambda b, p, i: (b, p)),
                  pl.BlockSpec((1, nq, LANES), lambda b, p, i: (b, 0, p)),
                  pl.BlockSpec((tq, LANES), lambda b, p, i: (b * nq + i, p))],
        out_specs=pl.BlockSpec((tq, LANES), lambda b, p, i: (b * nq + i, p)),
        out_shape=jax.ShapeDtypeStruct((rows, W), BF16),
        scratch_shapes=[pltpu.VMEM((LANES, LANES), F32)],
        compiler_params=_params(("arbitrary", "arbitrary", "arbitrary")),
        name="moba_prompt",
    )(q, qf, k, v, kmean.reshape(batch, nq, W), zs)


def _mlstm_kernel(qk_ref, v_ref, gb_ref, gt_ref, conv0_ref, c0_ref, m0_ref,
                  cw_ref, cb_ref, brow_ref,
                  y_ref, conv_out_ref, c_out_ref, m_out_ref,
                  xpad_ref, caug_ref, m_ref, *, L):
    c = pl.program_id(1)
    nc = pl.num_programs(1)
    H, DK = ML_HEADS, ML_HEAD_DIM
    W = HALF_WIDTH
    PADR = SUBLANES

    @pl.when(c == 0)
    def _():
        xpad_ref[0:PADR, :] = conv0_ref[0]
        caug_ref[...] = c0_ref[0]
        m_ref[...] = m0_ref[0]

    xpad_ref[PADR:PADR + L, :] = qk_ref[...]
    y = cb_ref[...]
    for j in range(CONV_WIDTH):
        off = PADR - (CONV_WIDTH - 1) + j
        y = y + cw_ref[j:j + 1, :] * xpad_ref[off:off + L, :]
    tail = xpad_ref[L:L + PADR, :]
    xpad_ref[0:PADR, :] = tail
    conv_out_ref[0] = tail
    qk = _silu(y)

    lane = _iota((L, LANES), 1)
    is_f_col = jnp.logical_and(lane >= H, lane < 2 * H)
    gcol = gt_ref[0] + brow_ref[...]
    lf_col = jnp.where(is_f_col, _log_sigmoid(gcol), 0.0)
    grow = jnp.transpose(gcol)[0:2 * H, :]
    sub = _iota((2 * H, L), 0)
    lf_row = jnp.where(sub >= H, _log_sigmoid(grow), 0.0)
    row = _iota((L, L), 0)
    col = _iota((L, L), 1)
    causal = col <= row
    tri = jnp.where(causal, 1.0, 0.0)
    f_col = _dot(tri, lf_col, HIGHEST)
    f_row = _dot_nt(lf_row, tri, HIGHEST)
    ones_col = jnp.where(_iota((L, DK), 1) == 0, 1.0, 0.0).astype(BF16)
    m_all = m_ref[...]
    m_next = m_all
    lane1 = _iota((1, LANES), 1)

    for h in range(H):
        fc = f_col[:, H + h:H + h + 1]
        fr = f_row[H + h:H + h + 1, :]
        li_c = gcol[:, h:h + 1]
        li_r = grow[h:h + 1, :]
        m_prev = m_all[:, h:h + 1]
        inter = fc + m_prev
        intra = jnp.where(causal, fc - fr + li_r, NEG_INF)
        mt = jnp.maximum(inter, jnp.max(intra, axis=1, keepdims=True))
        w = jnp.exp(intra - mt)
        g = jnp.exp(inter - mt)
        qh = qk[:, h * DK:(h + 1) * DK].astype(BF16)
        kf = qk[:, W + h * DK:W + (h + 1) * DK] * (DK ** -0.5)
        kh = kf.astype(BF16)
        vaug = jnp.concatenate([v_ref[:, h * DK:(h + 1) * DK].astype(BF16), ones_col], axis=1)
        s = _dot_nt(qh, kh) * w
        nd = g * _dot(qh, caug_ref[h].astype(BF16)) + _dot(s.astype(BF16), vaug)
        den = nd[:, DK:DK + 1]
        hout = nd[:, 0:DK] / jnp.maximum(jnp.abs(den), jnp.exp(-mt))
        y_ref[:, h * DK:(h + 1) * DK] = (gb_ref[:, h * DK:(h + 1) * DK] * hout).astype(y_ref.dtype)
        m_new = mt[L - 1:L, :]
        f_last = fc[L - 1:L, :]
        decay = jnp.exp(f_last + m_prev - m_new)
        ws = jnp.exp(f_last - fc + li_c - m_new)
        caug_ref[h] = decay * caug_ref[h] + _dot_tn((kf * ws).astype(BF16), vaug)
        m_next = jnp.where(lane1 == h, m_new, m_next)

    m_ref[...] = m_next

    @pl.when(c == nc - 1)
    def _():
        c_out_ref[0] = caug_ref[...]
        m_out_ref[0] = m_next


def _mlstm(qk_pre, vb, gb, gates, conv0, c0aug, m0, conv_w, conv_b, b_ig, b_fg, batch, L):
    rows, W2 = qk_pre.shape
    W = HALF_WIDTH
    H = ML_HEADS
    nc = rows // batch // L
    brow = jnp.pad(jnp.concatenate([b_ig, b_fg]).reshape(1, 2 * H), ((0, 0), (0, LANES - 2 * H)))
    y_dtype = BF16 if L % 16 == 0 else F32
    out_shapes = [jax.ShapeDtypeStruct((rows, W), y_dtype),
                  jax.ShapeDtypeStruct((batch, SUBLANES, W2), F32),
                  jax.ShapeDtypeStruct((batch, H, ML_HEAD_DIM, 2 * ML_HEAD_DIM), F32),
                  jax.ShapeDtypeStruct((batch, 1, LANES), F32)]
    return pl.pallas_call(
        functools.partial(_mlstm_kernel, L=L),
        grid=(batch, nc),
        in_specs=[pl.BlockSpec((L, W2), lambda b, c: (b * nc + c, 0)),
                  pl.BlockSpec((L, W), lambda b, c: (b * nc + c, 0)),
                  pl.BlockSpec((L, W), lambda b, c: (b * nc + c, 0)),
                  pl.BlockSpec((1, L, LANES), lambda b, c: (b * nc + c, 0, 0)),
                  pl.BlockSpec((1, SUBLANES, W2), lambda b, c: (b, 0, 0)),
                  pl.BlockSpec((1, H, ML_HEAD_DIM, 2 * ML_HEAD_DIM), lambda b, c: (b, 0, 0, 0)),
                  pl.BlockSpec((1, 1, LANES), lambda b, c: (b, 0, 0)),
                  pl.BlockSpec((CONV_WIDTH, W2), lambda b, c: (0, 0)),
                  pl.BlockSpec((1, W2), lambda b, c: (0, 0)),
                  pl.BlockSpec((1, LANES), lambda b, c: (0, 0))],
        out_specs=[pl.BlockSpec((L, W), lambda b, c: (b * nc + c, 0)),
                   pl.BlockSpec((1, SUBLANES, W2), lambda b, c: (b, 0, 0)),
                   pl.BlockSpec((1, H, ML_HEAD_DIM, 2 * ML_HEAD_DIM), lambda b, c: (b, 0, 0, 0)),
                   pl.BlockSpec((1, 1, LANES), lambda b, c: (b, 0, 0))],
        out_shape=out_shapes,
        scratch_shapes=[pltpu.VMEM((L + SUBLANES, W2), F32),
                        pltpu.VMEM((H, ML_HEAD_DIM, 2 * ML_HEAD_DIM), F32),
                        pltpu.VMEM((1, LANES), F32)],
        compiler_params=_params(("arbitrary", "arbitrary")),
        name="mlstm",
    )(qk_pre, vb, gb, gates, conv0, c0aug, m0, conv_w, conv_b.reshape(1, W2), brow)


def _rglru_gates(xc, wr_ref, br_ref, wi_ref, bi_ref, lam_ref):
    xb = xc.astype(BF16)
    r = _sigmoid(_dot(xb, wr_ref[...]) + br_ref[...])
    i = _sigmoid(_dot(xb, wi_ref[...]) + bi_ref[...])
    log_a = RG_C * r * _log_sigmoid(lam_ref[...])
    a = jnp.exp(log_a)
    b = jnp.sqrt(-jnp.tanh(log_a) * (a * a + 1.0)) * (i * xc)
    return a, b


def _rglru_prompt_kernel(xd_ref, zs_ref, conv0_ref, h0_ref, cw_ref, cb_ref, wr_ref, br_ref, wi_ref, bi_ref,
                         lam_ref, y_ref, conv_out_ref, h_out_ref,
                         xpad_ref, a_ref, b_ref, hs_ref, hc_ref, *, L):
    c = pl.program_id(1)
    PADR = SUBLANES

    @pl.when(c == 0)
    def _():
        xpad_ref[0:PADR, :] = conv0_ref[0]
        hc_ref[...] = h0_ref[0]

    xpad_ref[PADR:PADR + L, :] = xd_ref[...]
    xc = cb_ref[...]
    for j in range(CONV_WIDTH):
        off = PADR - (CONV_WIDTH - 1) + j
        xc = xc + cw_ref[j:j + 1, :] * xpad_ref[off:off + L, :]
    tail = xpad_ref[L:L + PADR, :]
    xpad_ref[0:PADR, :] = tail
    conv_out_ref[0] = tail

    a, b = _rglru_gates(xc, wr_ref, br_ref, wi_ref, bi_ref, lam_ref)
    a_ref[...] = a
    b_ref[...] = b

    def step(t, h):
        h = a_ref[pl.ds(t, 1), :] * h + b_ref[pl.ds(t, 1), :]
        hs_ref[pl.ds(t, 1), :] = h
        return h

    h = lax.fori_loop(0, L, step, hc_ref[...], unroll=8)
    hc_ref[...] = h
    h_out_ref[0] = h
    y_ref[...] = (hs_ref[...] * zs_ref[...]).astype(y_ref.dtype)


def _block_diag(w):
    g, n, _ = w.shape
    eye = jnp.eye(g, dtype=w.dtype)
    return (eye[:, None, :, None] * w[:, :, None, :]).reshape(g * n, g * n)


def _rglru_prompt(xd, zs, conv0, h0, conv_w, conv_b, wr, br, wi, bi, lam, batch, L):
    rows, W = xd.shape
    nc = rows // batch // L
    vec = lambda a: a.reshape(1, W)
    const = lambda shape: pl.BlockSpec(shape, lambda b, c: tuple(0 for _ in shape))
    return pl.pallas_call(
        functools.partial(_rglru_prompt_kernel, L=L),
        grid=(batch, nc),
        in_specs=[pl.BlockSpec((L, W), lambda b, c: (b * nc + c, 0)),
                  pl.BlockSpec((L, W), lambda b, c: (b * nc + c, 0)),
                  pl.BlockSpec((1, SUBLANES, W), lambda b, c: (b, 0, 0)),
                  pl.BlockSpec((1, 1, W), lambda b, c: (b, 0, 0)),
                  const((CONV_WIDTH, W)), const((1, W)), const((W, W)), const((1, W)), const((W, W)),
                  const((1, W)), const((1, W))],
        out_specs=[pl.BlockSpec((L, W), lambda b, c: (b * nc + c, 0)),
                   pl.BlockSpec((1, SUBLANES, W), lambda b, c: (b, 0, 0)),
                   pl.BlockSpec((1, 1, W), lambda b, c: (b, 0, 0))],
        out_shape=[jax.ShapeDtypeStruct((rows, W), BF16),
                   jax.ShapeDtypeStruct((batch, SUBLANES, W), F32),
                   jax.ShapeDtypeStruct((batch, 1, W), F32)],
        scratch_shapes=[pltpu.VMEM((L + SUBLANES, W), F32), pltpu.VMEM((L, W), F32), pltpu.VMEM((L, W), F32),
                        pltpu.VMEM((L, W), F32), pltpu.VMEM((1, W), F32)],
        compiler_params=_params(("arbitrary", "arbitrary")),
        name="rglru_prompt",
    )(xd, zs, conv0, h0, conv_w, vec(conv_b), _block_diag(wr).astype(BF16), vec(br),
      _block_diag(wi).astype(BF16), vec(bi), vec(lam))


def _rglru_sample_kernel(xd_ref, zs_ref, conv0_ref, h0_ref, cw_ref, cb_ref, wr_ref, br_ref, wi_ref, bi_ref,
                         lam_ref, y_ref, h_out_ref, *, T):
    xs = [conv0_ref[j] for j in range(CONV_WIDTH - 1)] + [xd_ref[t] for t in range(T)]
    h = h0_ref[...]
    for t in range(T):
        xc = cb_ref[...]
        for j in range(CONV_WIDTH):
            xc = xc + cw_ref[j:j + 1, :] * xs[t + j]
        a, b = _rglru_gates(xc, wr_ref, br_ref, wi_ref, bi_ref, lam_ref)
        h = a * h + b
        y_ref[t] = h * zs_ref[t]
    h_out_ref[...] = h


def _rglru_sample(xd, zs, conv0, h0, conv_w, conv_b, wr, br, wi, bi, lam):
    T, B, W = xd.shape
    vec = lambda a: a.reshape(1, W)
    return pl.pallas_call(
        functools.partial(_rglru_sample_kernel, T=T),
        out_shape=[jax.ShapeDtypeStruct((T, B, W), F32), jax.ShapeDtypeStruct((B, W), F32)],
        compiler_params=pltpu.CompilerParams(vmem_limit_bytes=VMEM_LIMIT),
        name="rglru_sample",
    )(xd, zs, conv0, h0, conv_w, vec(conv_b), _block_diag(wr).astype(BF16), vec(br),
      _block_diag(wi).astype(BF16), vec(bi), vec(lam))


def _query_rows(q):
    b, t, w = q.shape
    heads = w // HEAD_DIM
    onehot = (jnp.arange(heads)[:, None] == (jnp.arange(w) // HEAD_DIM)[None, :]).astype(q.dtype)
    return (q[:, :, None, :] * onehot[None, None]).reshape(b, t * heads, w)


def _channel_major(a, lanes):
    return jnp.pad(jnp.transpose(a, (0, 2, 1)), ((0, 0), (0, 0), (0, lanes - a.shape[1])))


def _pool_channel_major(pool):
    n_pool, page_rows, heads, hd = pool.shape
    return jnp.transpose(pool, (0, 2, 3, 1)).reshape(n_pool, heads * hd, page_rows)


def _page_specs(pages_per_step, page_rows, width, page_of):
    def spec(i):
        return pl.BlockSpec((None, width, page_rows), lambda b, c, pt: (pt[b, page_of(c, i)], 0, 0))
    return [spec(i) for i in range(pages_per_step)]


def _head_diagonal(o, t_new, zs):
    heads = HALF_WIDTH // HEAD_DIM
    rows = t_new * heads
    keep = (_iota((rows, HALF_WIDTH), 1) // HEAD_DIM) == (_iota((rows, HALF_WIDTH), 0) % heads)
    o = jnp.where(keep, o, 0.0)
    return jnp.sum(o.reshape(t_new, heads, HALF_WIDTH), axis=1) * zs


def _key_absmax_kernel(pt_ref, *rest, P):
    kpages, o_ref = rest[:P], rest[P]
    c = pl.program_id(1)
    m = jnp.abs(kpages[0][...])
    for i in range(1, P):
        m = jnp.maximum(m, jnp.abs(kpages[i][...]))

    @pl.when(c == 0)
    def _():
        o_ref[0] = m

    @pl.when(c > 0)
    def _():
        o_ref[0] = jnp.maximum(o_ref[0], m)


def _key_absmax(page_table, k_pool, P):
    B, n_pages = page_table.shape
    W, page_rows = k_pool.shape[1], k_pool.shape[2]
    grid_spec = pltpu.PrefetchScalarGridSpec(
        num_scalar_prefetch=1,
        grid=(B, n_pages // P),
        in_specs=_page_specs(P, page_rows, W, lambda c, i: c * P + i),
        out_specs=pl.BlockSpec((1, W, page_rows), lambda b, c, pt: (b, 0, 0)),
    )
    return pl.pallas_call(
        functools.partial(_key_absmax_kernel, P=P),
        grid_spec=grid_spec,
        out_shape=jax.ShapeDtypeStruct((B, W, page_rows), F32),
        compiler_params=_params(("arbitrary", "arbitrary")),
        name="key_absmax",
    )(page_table, *([k_pool] * P))


def _sb_sample_kernel(pt_ref, q_ref, knew_ref, vnew_ref, zs_ref, kabs_ref, acc0_ref, car0_ref, *rest,
                      P, t_new, newest):
    kpages, vpages = rest[:P], rest[P:2 * P]
    o_ref, acc_out_ref, car_out_ref, done_ref, acc_ref, car_ref = rest[2 * P:]
    c = pl.program_id(1)
    nch = pl.num_programs(1)
    q = q_ref[0]
    heads = HALF_WIDTH // HEAD_DIM
    rows = t_new * heads

    def key_blocks(kts, vts, mask):
        n = kts[0].shape[1]
        suffix = jnp.where(_iota((n, n), 0) >= _iota((n, n), 1), 1.0, 0.0).astype(BF16)
        zs = [_dot(q, kt.astype(BF16)) for kt in kts]
        lss = [-_softplus_scores(z) for z in zs]
        if mask is not None:
            lss = [jnp.where(mask, ls, 0.0) for ls in lss]
        sums = []
        for ls in lss:
            hi, lo = _split_bf16(ls)
            sums.append(_dot(hi, suffix) + _dot(lo, suffix))
        carry = car_ref[...]
        acc = acc_ref[...]
        for z, s, vt in zip(zs, sums, vts):
            w = jnp.exp(z + s + carry)
            if mask is not None:
                w = jnp.where(mask, w, 0.0)
            acc = acc + _dot_nt(w.astype(BF16), vt.astype(BF16))
            carry = carry + s[:, 0:1]
        acc_ref[...] = acc
        car_ref[...] = carry

    @pl.when(c == 0)
    def _():
        if newest:
            acc_ref[...] = jnp.zeros_like(acc_ref)
            car_ref[...] = jnp.zeros_like(car_ref)
            n = knew_ref.shape[2]
            j = _iota((rows, n), 1)
            t = _iota((rows, n), 0) // heads
            key_blocks([knew_ref[0]], [vnew_ref[0]], jnp.logical_and(j < t, j < t_new))
        else:
            acc_ref[...] = acc0_ref[0]
            car_ref[...] = car0_ref[0][:, 0:1]

    key_blocks([kpages[i][...] for i in reversed(range(P))], [vpages[i][...] for i in reversed(range(P))], None)

    @pl.when(c == nch - 1)
    def _():
        o_ref[0] = _head_diagonal(acc_ref[...], t_new, zs_ref[0])
        acc_out_ref[0] = acc_ref[...]
        car_out_ref[0] = jnp.broadcast_to(car_ref[...], (rows, LANES))
        bound = jnp.max(_dot(jnp.abs(q.astype(F32)), kabs_ref[0], HIGHEST), axis=1, keepdims=True)
        worst = jnp.max(bound * SB_BOUND_SLACK + car_ref[...])
        done_ref[0] = jnp.where(worst < SB_EXP_UNDERFLOW, 1.0, 0.0) * jnp.ones((1, LANES), F32)


def _sb_sample_pages(page_table, q_rows, knew_t, vnew_t, zs, kabs, acc0, car0, k_pool, v_pool, P, first_page,
                     n_used, newest):
    B = page_table.shape[0]
    t_new = zs.shape[1]
    W = HALF_WIDTH
    rows = q_rows.shape[1]
    page_rows = k_pool.shape[2]
    nch = n_used // P
    page_of = lambda c, i: first_page + (nch - 1 - c) * P + i
    per_b = lambda shape: pl.BlockSpec((1,) + shape, lambda b, c, pt: (b,) + tuple(0 for _ in shape))
    grid_spec = pltpu.PrefetchScalarGridSpec(
        num_scalar_prefetch=1,
        grid=(B, nch),
        in_specs=[per_b((rows, W)), per_b(knew_t.shape[1:]), per_b(vnew_t.shape[1:]), per_b((t_new, W)),
                  per_b((W, page_rows)), per_b((rows, W)), per_b((rows, LANES))]
        + _page_specs(P, page_rows, W, page_of) + _page_specs(P, page_rows, W, page_of),
        out_specs=[per_b((t_new, W)), per_b((rows, W)), per_b((rows, LANES)), per_b((1, LANES))],
        scratch_shapes=[pltpu.VMEM((rows, W), F32), pltpu.VMEM((rows, 1), F32)],
    )
    return pl.pallas_call(
        functools.partial(_sb_sample_kernel, P=P, t_new=t_new, newest=newest),
        grid_spec=grid_spec,
        out_shape=[jax.ShapeDtypeStruct((B, t_new, W), F32), jax.ShapeDtypeStruct((B, rows, W), F32),
                   jax.ShapeDtypeStruct((B, rows, LANES), F32), jax.ShapeDtypeStruct((B, 1, LANES), F32)],
        compiler_params=_params(("arbitrary", "arbitrary")),
        name="sb_sample_newest" if newest else "sb_sample_older",
    )(page_table, q_rows, knew_t, vnew_t, zs, kabs, acc0, car0, *([k_pool] * P), *([v_pool] * P))


def _sb_sample(page_table, q_rows, knew_t, vnew_t, zs, k_pool, v_pool, p_scan, p_newest):
    B, n_pages = page_table.shape
    rows = q_rows.shape[1]
    p_newest = min(p_newest, n_pages)
    kabs = _key_absmax(page_table, k_pool, p_scan)
    y, acc, car, done = _sb_sample_pages(
        page_table, q_rows, knew_t, vnew_t, zs, kabs, jnp.zeros((B, rows, HALF_WIDTH), F32),
        jnp.zeros((B, rows, LANES), F32), k_pool, v_pool, p_newest, n_pages - p_newest, p_newest, True)
    n_older = n_pages - p_newest
    if n_older == 0:
        return y
    p_older = max(p for p in range(1, p_scan + 1) if n_older % p == 0)
    older = lambda: _sb_sample_pages(page_table, q_rows, knew_t, vnew_t, zs, kabs, acc, car, k_pool, v_pool,
                                     p_older, 0, n_older, False)[0]
    return lax.cond(jnp.min(done) > 0.5, lambda: y, older)


def _moba_sample_kernel(pt_ref, q_ref, qg_ref, knew_ref, vnew_ref, zs_ref, *rest, P, t_new, nb, pb):
    kpages, vpages = rest[:P], rest[P:2 * P]
    o_ref, m_ref, l_ref, acc_ref, km_ref = rest[2 * P:]
    c = pl.program_id(1)
    nch = pl.num_programs(1)
    q = q_ref[0]
    heads = HALF_WIDTH // HEAD_DIM
    rows = t_new * heads
    lane = _iota((rows, LANES), 1)

    @pl.when(c == 0)
    def _():
        km_ref[...] = jnp.zeros_like(km_ref)
        m_ref[...] = jnp.zeros_like(m_ref)
        l_ref[...] = jnp.zeros_like(l_ref)

    nblk = P // pb
    zs = [[_dot(q, kpages[pb * s + r][...].astype(BF16)) for r in range(pb)] for s in range(nblk)]
    km_all, m_all, l_all = km_ref[...], m_ref[...], l_ref[...]
    km_lane = _iota(km_ref.shape, 1)
    for s in range(nblk):
        n = c * nblk + s
        ksum = kpages[pb * s][...]
        for r in range(1, pb):
            ksum = ksum + kpages[pb * s + r][...]
        kmean = jnp.sum(ksum, axis=1, keepdims=True) * (1.0 / MOBA_BLOCK)
        km_all = jnp.where(km_lane == n, kmean, km_all)
        m = jnp.max(zs[s][0], axis=1, keepdims=True)
        for z in zs[s][1:]:
            m = jnp.maximum(m, jnp.max(z, axis=1, keepdims=True))
        l = jnp.zeros((rows, 1), F32)
        a = jnp.zeros((rows, HALF_WIDTH), F32)
        for r, z in enumerate(zs[s]):
            p = jnp.exp(z - m)
            l = l + jnp.sum(p, axis=1, keepdims=True)
            a = a + _dot_nt(p.astype(BF16), vpages[pb * s + r][...].astype(BF16))
        m_all = jnp.where(lane == n, m, m_all)
        l_all = jnp.where(lane == n, l, l_all)
        acc_ref[n] = a
    km_ref[...] = km_all
    m_ref[...] = m_all
    l_ref[...] = l_all

    @pl.when(c == nch - 1)
    def _():
        lane_f = lane.astype(F32)
        g = _dot(qg_ref[0], km_ref[...], HIGHEST)
        g = jnp.where(lane < nb, g, NEG_INF)
        sel = jnp.zeros((rows, LANES), F32)
        for _ in range(min(MOBA_TOPK, nb)):
            mx = jnp.max(g, axis=1, keepdims=True)
            is_max = jnp.logical_and(g == mx, mx > NEG_INF)
            idx = jnp.min(jnp.where(is_max, lane_f, float(LANES)), axis=1, keepdims=True)
            pick = lane_f == idx
            sel = jnp.where(pick, 1.0, sel)
            g = jnp.where(pick, NEG_INF, g)
        nn = knew_ref.shape[2]
        j = _iota((rows, nn), 1)
        t = _iota((rows, nn), 0) // heads
        zn = jnp.where(jnp.logical_and(j <= t, j < t_new), _dot(q, knew_ref[0].astype(BF16)), NEG_INF)
        m_all = m_ref[...]
        m_tot = jnp.maximum(jnp.max(jnp.where(sel > 0.0, m_all, NEG_INF), axis=1, keepdims=True),
                            jnp.max(zn, axis=1, keepdims=True))
        coef = jnp.where(sel > 0.0, jnp.exp(m_all - m_tot), 0.0)
        p_own = jnp.exp(zn - m_tot)
        denom = jnp.sum(coef * l_ref[...], axis=1, keepdims=True) + jnp.sum(p_own, axis=1, keepdims=True)
        o = _dot_nt(p_own.astype(BF16), vnew_ref[0].astype(BF16))
        for n in range(nb):
            o = o + coef[:, n:n + 1] * acc_ref[n]
        o_ref[0] = _head_diagonal(o / denom, t_new, zs_ref[0])


def _moba_sample(page_table, q_rows, qg_rows, knew_t, vnew_t, zs, k_pool, v_pool, P):
    B, n_pages = page_table.shape
    t_new = zs.shape[1]
    W = HALF_WIDTH
    rows = q_rows.shape[1]
    page_rows = k_pool.shape[2]
    pb = MOBA_BLOCK // page_rows
    nch = n_pages // P
    nb = n_pages // pb
    assert nb <= LANES and P % pb == 0
    page_of = lambda c, i: c * P + i
    per_b = lambda shape: pl.BlockSpec((1,) + shape, lambda b, c, pt: (b,) + tuple(0 for _ in shape))
    grid_spec = pltpu.PrefetchScalarGridSpec(
        num_scalar_prefetch=1,
        grid=(B, nch),
        in_specs=[per_b((rows, W)), per_b((rows, W)), per_b(knew_t.shape[1:]), per_b(vnew_t.shape[1:]),
                  per_b((t_new, W))]
        + _page_specs(P, page_rows, W, page_of) + _page_specs(P, page_rows, W, page_of),
        out_specs=per_b((t_new, W)),
        scratch_shapes=[pltpu.VMEM((rows, LANES), F32), pltpu.VMEM((rows, LANES), F32),
                        pltpu.VMEM((nb, rows, W), F32), pltpu.VMEM((W, LANES), F32)],
    )
    return pl.pallas_call(
        functools.partial(_moba_sample_kernel, P=P, t_new=t_new, nb=nb, pb=pb),
        grid_spec=grid_spec,
        out_shape=jax.ShapeDtypeStruct((B, t_new, W), F32),
        compiler_params=_params(("arbitrary", "arbitrary")),
        name="moba_sample",
    )(page_table, q_rows, qg_rows, knew_t, vnew_t, zs, *([k_pool] * P), *([v_pool] * P))


def _pad_rows(a, rows):
    return jnp.pad(a, ((0, 0), (0, rows - a.shape[1]), (0, 0)))


def kernel(x_prompt, x_sample, c_prompt, c_sample, page_table, cache_k_sb, cache_v_sb, state_conv_mlstm, state_c_mlstm, state_n_mlstm, state_m_mlstm, cache_k_moba, cache_v_moba, state_conv_rglru, state_h_rglru, norm_g_even, mod_w_even, mod_b_even, w_in_even, conv_w_even, conv_b_even, b_igate_even, b_fgate_even, w_out_even, norm_g_odd, mod_w_odd, mod_b_odd, w_in_odd, conv_w_odd, conv_b_odd, w_rgate_odd, b_rgate_odd, w_igate_odd, b_igate_odd, lru_lambda_odd, w_out_odd, final_g):
    B, T, D = x_prompt.shape
    DB, DT, _ = x_sample.shape
    W = HALF_WIDTH
    H = ML_HEADS
    heads = W // HEAD_DIM
    n_pool, page_rows = cache_k_sb.shape[1], cache_k_sb.shape[2]
    past = page_table.shape[1] * page_rows
    depth = norm_g_even.shape[0] + norm_g_odd.shape[0]
    TM = 256
    TP = 512 if T % 512 == 0 else TM
    n_pages = page_table.shape[1]
    PAGES_PER_STEP = max(p for p in range(2, 17, 2) if n_pages % p == 0)
    SB_NEWEST_PAGES = 4
    NEW_KEYS = LANES
    ML_PAD = SUBLANES

    xp = x_prompt.reshape(B * T, D)
    xs = x_sample.reshape(DB * DT, D)
    c_all = jnp.concatenate([c_prompt, c_sample], axis=0)
    c_rows = -(-c_all.shape[0] // SUBLANES) * SUBLANES
    c_all = jnp.pad(c_all, ((0, c_rows - c_all.shape[0]), (0, 0)))
    cos_p, sin_p = _rope_tables(jnp.arange(T))
    cos_s, sin_s = _rope_tables(past + jnp.arange(DT))
    cos_s, sin_s = jnp.tile(cos_s, (DB, 1)), jnp.tile(sin_s, (DB, 1))

    def mods(w, b):
        mod = _modulation(c_all, w, b)
        parts = []
        for part in jnp.split(mod, 3, axis=-1):
            p_part = part[:B].reshape(B, 1, D)
            s_part = jnp.repeat(part[B:B + DB], DT, axis=0).reshape(1, DB * DT, D)
            parts.append((p_part, s_part))
        return parts

    p_even, s_even, p_odd, s_odd = [], [], [], []
    y_prompt = y_sample = None
    for l in range(depth):
        j = l // 2
        last = l == depth - 1
        if l % 2 == 0:
            (sh_p, sh_s), (sc_p, sc_s), (ga_p, ga_s) = mods(mod_w_even[j], mod_b_even[j])
            qa, ka, kab, va, vab, za, qk, vb, gb, gt = _in_proj_even(
                xp, sh_p, sc_p, norm_g_even[j], w_in_even[j], TP)
            ya = _sb_prompt(qa, kab, vab, za, B, T, TM)
            yb, conv_p, caug_p, m_p = _mlstm(
                qk, vb, gb, gt.reshape(B * T // TM, TM, LANES),
                jnp.zeros((B, SUBLANES, 2 * W), F32), jnp.zeros((B, H, ML_HEAD_DIM, 2 * ML_HEAD_DIM), F32),
                jnp.zeros((B, 1, LANES), F32),
                conv_w_even[j], conv_b_even[j], b_igate_even[j], b_fgate_even[j], B, TM)
            xp = _out_proj(ya, yb, xp, ga_p, w_out_even[j], final_g, TP, last)
            p_even.append((ka.reshape(B, T, heads, HEAD_DIM), va.reshape(B, T, heads, HEAD_DIM),
                           conv_p[:, SUBLANES - (CONV_WIDTH - 1):], caug_p[..., :ML_HEAD_DIM],
                           caug_p[..., ML_HEAD_DIM], m_p[:, 0, :H]))
            qa, ka, kab, va, vab, za, qk, vb, gb, gt = _in_proj_even(
                xs, sh_s, sc_s, norm_g_even[j], w_in_even[j], DB * DT)
            ya = _sb_sample(page_table, _query_rows(qa.reshape(DB, DT, W)),
                            _channel_major(ka.reshape(DB, DT, W), NEW_KEYS),
                            _channel_major(va.reshape(DB, DT, W), NEW_KEYS), za.reshape(DB, DT, W),
                            _pool_channel_major(cache_k_sb[j]), _pool_channel_major(cache_v_sb[j]),
                            PAGES_PER_STEP, SB_NEWEST_PAGES).reshape(DB * DT, W)
            pad_t = lambda a: _pad_rows(a.reshape(DB, DT, a.shape[-1]), ML_PAD).reshape(DB * ML_PAD, a.shape[-1])
            g_rows = gt.reshape(DB, DT, LANES)[:, :, :2 * H]
            pad_gate = jnp.concatenate([jnp.full((H,), NEG_INF, F32), jnp.full((H,), jnp.inf, F32)])
            g_rows = jnp.concatenate([g_rows, jnp.broadcast_to(pad_gate, (DB, ML_PAD - DT, 2 * H))], axis=1)
            gt_s = jnp.pad(g_rows, ((0, 0), (0, 0), (0, LANES - 2 * H)))
            conv0 = jnp.pad(state_conv_mlstm[j], ((0, 0), (SUBLANES - (CONV_WIDTH - 1), 0), (0, 0)))
            c0aug = jnp.concatenate(
                [state_c_mlstm[j].astype(F32), state_n_mlstm[j].astype(F32)[..., None],
                 jnp.zeros((DB, H, ML_HEAD_DIM, ML_HEAD_DIM - 1), F32)], axis=-1)
            m0 = jnp.pad(state_m_mlstm[j].astype(F32), ((0, 0), (0, LANES - H))).reshape(DB, 1, LANES)
            yb, _, caug_s, m_s = _mlstm(
                pad_t(qk), pad_t(vb), pad_t(gb), gt_s, conv0, c0aug, m0,
                conv_w_even[j], conv_b_even[j], b_igate_even[j], b_fgate_even[j], DB, ML_PAD)
            yb = yb.reshape(DB, ML_PAD, W)[:, :DT].reshape(DB * DT, W)
            conv_s = jnp.concatenate([state_conv_mlstm[j].astype(F32), qk.reshape(DB, DT, 2 * W)],
                                     axis=1)[:, -(CONV_WIDTH - 1):]
            xs = _out_proj(ya, yb, xs, ga_s, w_out_even[j], final_g, DB * DT, last)
            s_even.append((ka.reshape(DB, DT, heads, HEAD_DIM), va.reshape(DB, DT, heads, HEAD_DIM),
                           conv_s, caug_s[..., :ML_HEAD_DIM], caug_s[..., ML_HEAD_DIM], m_s[:, 0, :H]))
        else:
            (sh_p, sh_s), (sc_p, sc_s), (ga_p, ga_s) = mods(mod_w_odd[j], mod_b_odd[j])
            lru = (conv_w_odd[j], conv_b_odd[j], w_rgate_odd[j], b_rgate_odd[j], w_igate_odd[j], b_igate_odd[j],
                   lru_lambda_odd[j])
            qb, qf, kc, kcb, vc, vcb, zc, xd, zd, km = _in_proj_odd(
                xp, sh_p, sc_p, norm_g_odd[j], w_in_odd[j], cos_p, sin_p, TP)
            yc = _moba_prompt(qb, qf, kcb, vcb, km, zc, B, T)
            yd, conv_p, h_p = _rglru_prompt(xd, zd, jnp.zeros((B, SUBLANES, W), F32), jnp.zeros((B, 1, W), F32),
                                            *lru, B, TM)
            xp = _out_proj(yc, yd, xp, ga_p, w_out_odd[j], final_g, TP, last)
            p_odd.append((kc.reshape(B, T, heads, HEAD_DIM), vc.reshape(B, T, heads, HEAD_DIM),
                          conv_p[:, SUBLANES - (CONV_WIDTH - 1):], h_p[:, 0]))
            qb, qf, kc, kcb, vc, vcb, zc, xd, zd, km = _in_proj_odd(
                xs, sh_s, sc_s, norm_g_odd[j], w_in_odd[j], cos_s, sin_s, DB * DT)
            yc = _moba_sample(page_table, _query_rows(qb.reshape(DB, DT, W)), _query_rows(qf.reshape(DB, DT, W)),
                              _channel_major(kc.reshape(DB, DT, W), NEW_KEYS),
                              _channel_major(vc.reshape(DB, DT, W), NEW_KEYS), zc.reshape(DB, DT, W),
                              _pool_channel_major(cache_k_moba[j]), _pool_channel_major(cache_v_moba[j]),
                              PAGES_PER_STEP).reshape(DB * DT, W)
            tmaj = lambda a: jnp.transpose(a.reshape(DB, DT, W), (1, 0, 2))
            yd, h_s = _rglru_sample(tmaj(xd), tmaj(zd), jnp.transpose(state_conv_rglru[j].astype(F32), (1, 0, 2)),
                                    state_h_rglru[j].astype(F32), *lru)
            yd = jnp.transpose(yd, (1, 0, 2)).reshape(DB * DT, W)
            conv_s = jnp.concatenate([state_conv_rglru[j].astype(F32), xd.reshape(DB, DT, W)],
                                     axis=1)[:, -(CONV_WIDTH - 1):]
            xs = _out_proj(yc, yd, xs, ga_s, w_out_odd[j], final_g, DB * DT, last)
            s_odd.append((kc.reshape(DB, DT, heads, HEAD_DIM), vc.reshape(DB, DT, heads, HEAD_DIM), conv_s, h_s))

    y_prompt = xp.reshape(B, T, D)
    y_sample = xs.reshape(DB, DT, D)
    stack = lambda group: [jnp.stack(a) for a in zip(*group)]
    pk_sb, pv_sb, pconv_m, pc_m, pn_m, pm_m = stack(p_even)
    sk_sb, sv_sb, sconv_m, sc_m, sn_m, sm_m = stack(s_even)
    pk_mb, pv_mb, pconv_d, ph_d = stack(p_odd)
    sk_mb, sv_mb, sconv_d, sh_d = stack(s_odd)
    return (y_prompt, y_sample,
            pk_sb, pv_sb, pconv_m, pc_m, pn_m, pm_m, pk_mb, pv_mb, pconv_d, ph_d,
            sk_sb, sv_sb, sconv_m, sc_m, sn_m, sm_m, sk_mb, sv_mb, sconv_d, sh_d)
```

```python
import functools

import jax
import jax.numpy as jnp
from jax import lax
from jax.experimental import pallas as pl
from jax.experimental.pallas import tpu as pltpu

F32 = jnp.float32
BF16 = jnp.bfloat16
HIGHEST = lax.Precision.HIGHEST

HEAD_DIM = 64
HALF_WIDTH = 512
ML_HEADS = 4
ML_HEAD_DIM = 128
CONV_WIDTH = 4
MOBA_BLOCK = 256
MOBA_TOPK = 3
RG_C = 8.0
ROPE_THETA = 10000.0
NORM_EPS = 1e-6
LANES = 128
SUBLANES = 8
VMEM_LIMIT = 56 * 1024 * 1024
NEG_INF = float("-inf")
SB_EXP_UNDERFLOW = -110.0
SB_BOUND_SLACK = 1.01
MOBA_MASK_PENALTY = -1e30
MOBA_GROUP = 4


def _params(sem):
    return pltpu.CompilerParams(dimension_semantics=sem, vmem_limit_bytes=VMEM_LIMIT)


def _sigmoid(x):
    return 1.0 / (1.0 + jnp.exp(-x))


def _silu(x):
    return x * _sigmoid(x)


def _softplus(x):
    return jnp.maximum(x, 0.0) + jnp.log1p(jnp.exp(-jnp.abs(x)))


def _softplus_scores(x):
    return jnp.maximum(x, 0.0) + jnp.log(1.0 + jnp.exp(-jnp.abs(x)))


def _log_sigmoid(x):
    return -_softplus(-x)


def _dot(a, b, precision=None):
    return jnp.dot(a, b, preferred_element_type=F32, precision=precision)


def _dot_nt(a, b, precision=None):
    return lax.dot_general(a, b, (((1,), (1,)), ((), ())), preferred_element_type=F32, precision=precision)


def _dot_tn(a, b, precision=None):
    return lax.dot_general(a, b, (((0,), (0,)), ((), ())), preferred_element_type=F32, precision=precision)


def _split_bf16(x):
    hi = x.astype(BF16)
    lo = (x - hi.astype(F32)).astype(BF16)
    return hi, lo


def _iota(shape, dim):
    return lax.broadcasted_iota(jnp.int32, shape, dim)


def _mod_kernel(c_ref, w_ref, b_ref, o_ref):
    o_ref[...] = _dot(_silu(c_ref[...]), w_ref[...], HIGHEST) + b_ref[...]


def _modulation(c, w, b):
    rows, d = c.shape
    n = w.shape[1]
    tn = 512
    return pl.pallas_call(
        _mod_kernel,
        grid=(n // tn,),
        in_specs=[pl.BlockSpec((rows, d), lambda j: (0, 0)),
                  pl.BlockSpec((d, tn), lambda j: (0, j)),
                  pl.BlockSpec((1, tn), lambda j: (0, j))],
        out_specs=pl.BlockSpec((rows, tn), lambda j: (0, j)),
        out_shape=jax.ShapeDtypeStruct((rows, n), F32),
        compiler_params=_params(("arbitrary",)),
        name="modulation",
    )(c, w, b.reshape(1, n))


def _normed(x_ref, shift_ref, scale_ref, g_ref):
    x = x_ref[...]
    ms = jnp.mean(x * x, axis=-1, keepdims=True)
    h = x * lax.rsqrt(ms + NORM_EPS) * g_ref[...]
    return h * (1.0 + scale_ref[0]) + shift_ref[0]


def _in_even_kernel(x_ref, shift_ref, scale_ref, g_ref, w_ref, wg_hi_ref, wg_lo_ref,
                    qa_ref, ka_ref, kab_ref, va_ref, vab_ref, za_ref, qk_ref, vb_ref, gb_ref, gt_ref):
    h = _normed(x_ref, shift_ref, scale_ref, g_ref)
    hb, h_lo = _split_bf16(h)
    W = HALF_WIDTH

    def proj(c):
        return _dot(hb, w_ref[:, c * W:(c + 1) * W])

    qa_ref[...] = (proj(0) * (HEAD_DIM ** -0.5)).astype(BF16)
    ka = proj(1)
    ka_ref[...] = ka
    kab_ref[...] = ka.astype(BF16)
    va = proj(2)
    va_ref[...] = va
    vab_ref[...] = va.astype(BF16)
    za_ref[...] = _silu(proj(3))
    qk_ref[:, 0:W] = proj(4)
    qk_ref[:, W:2 * W] = proj(5)
    vb_ref[...] = proj(6)
    gb_ref[...] = _sigmoid(proj(7)) * _silu(proj(8))
    gt_ref[0] = _dot(hb, wg_hi_ref[...]) + _dot(h_lo, wg_hi_ref[...]) + _dot(hb, wg_lo_ref[...])


def _in_proj_even(x, shift, scale, g, w_in, tm):
    rows, d = x.shape
    W = HALF_WIDTH
    nt = rows // tm
    groups = shift.shape[0]
    tiles_per_group = nt // groups
    w_main = w_in[:, :9 * W].astype(BF16)
    w_gate_hi, w_gate_lo = _split_bf16(jnp.pad(w_in[:, 9 * W:], ((0, 0), (0, LANES - 2 * ML_HEADS))))
    row_blk = lambda n, dt: (pl.BlockSpec((tm, n), lambda i: (i, 0)), jax.ShapeDtypeStruct((rows, n), dt))
    outs = [row_blk(W, BF16), row_blk(W, F32), row_blk(W, BF16), row_blk(W, F32), row_blk(W, BF16),
            row_blk(W, F32), row_blk(2 * W, F32), row_blk(W, F32), row_blk(W, F32),
            (pl.BlockSpec((1, tm, LANES), lambda i: (i, 0, 0)), jax.ShapeDtypeStruct((nt, tm, LANES), F32))]
    rg = shift.shape[1]
    return pl.pallas_call(
        _in_even_kernel,
        grid=(nt,),
        in_specs=[pl.BlockSpec((tm, d), lambda i: (i, 0)),
                  pl.BlockSpec((1, rg, d), lambda i: (i // tiles_per_group, 0, 0)),
                  pl.BlockSpec((1, rg, d), lambda i: (i // tiles_per_group, 0, 0)),
                  pl.BlockSpec((1, d), lambda i: (0, 0)),
                  pl.BlockSpec((d, 9 * W), lambda i: (0, 0)),
                  pl.BlockSpec((d, LANES), lambda i: (0, 0)),
                  pl.BlockSpec((d, LANES), lambda i: (0, 0))],
        out_specs=[o[0] for o in outs],
        out_shape=[o[1] for o in outs],
        compiler_params=_params(("arbitrary",)),
        name="in_proj_even",
    )(x, shift, scale, g.reshape(1, d), w_main, w_gate_hi, w_gate_lo)


def _in_odd_kernel(x_ref, shift_ref, scale_ref, g_ref, w_ref, cos_ref, sin_ref,
                   qb_ref, qf_ref, kc_ref, kcb_ref, vc_ref, vcb_ref, zc_ref, xd_ref, zd_ref, km_ref):
    h = _normed(x_ref, shift_ref, scale_ref, g_ref)
    hb = h.astype(BF16)
    W = HALF_WIDTH

    def proj(c):
        return _dot(hb, w_ref[:, c * W:(c + 1) * W])

    cos = cos_ref[...]
    sin = sin_ref[...]
    first_half = (_iota(cos.shape, 1) % HEAD_DIM) < (HEAD_DIM // 2)

    def rope(x):
        partner = jnp.where(first_half, pltpu.roll(x, W - HEAD_DIM // 2, 1), pltpu.roll(x, HEAD_DIM // 2, 1))
        return x * cos + partner * sin

    q = rope(proj(0))
    qf_ref[...] = q
    qb_ref[...] = (q * (HEAD_DIM ** -0.5)).astype(BF16)
    k = rope(proj(1))
    kc_ref[...] = k
    kcb_ref[...] = k.astype(BF16)
    blk = min(MOBA_BLOCK, k.shape[0])
    for r in range(k.shape[0] // blk):
        km_ref[r] = jnp.mean(k[r * blk:(r + 1) * blk], axis=0, keepdims=True)
    v = proj(2)
    vc_ref[...] = v
    vcb_ref[...] = v.astype(BF16)
    zc_ref[...] = _silu(proj(3))
    xd_ref[...] = proj(4)
    zd_ref[...] = _silu(proj(5))


def _in_proj_odd(x, shift, scale, g, w_in, cos, sin, tm):
    rows, d = x.shape
    W = HALF_WIDTH
    nt = rows // tm
    nkm = max(tm // MOBA_BLOCK, 1)
    groups = shift.shape[0]
    tiles_per_group = nt // groups
    rg = shift.shape[1]
    row_blk = lambda n, dt: (pl.BlockSpec((tm, n), lambda i: (i, 0)), jax.ShapeDtypeStruct((rows, n), dt))
    outs = [row_blk(W, BF16), row_blk(W, F32), row_blk(W, F32), row_blk(W, BF16), row_blk(W, F32),
            row_blk(W, BF16), row_blk(W, F32), row_blk(W, F32), row_blk(W, F32),
            (pl.BlockSpec((nkm, 1, W), lambda i: (i, 0, 0)), jax.ShapeDtypeStruct((nt * nkm, 1, W), F32))]
    return pl.pallas_call(
        _in_odd_kernel,
        grid=(nt,),
        in_specs=[pl.BlockSpec((tm, d), lambda i: (i, 0)),
                  pl.BlockSpec((1, rg, d), lambda i: (i // tiles_per_group, 0, 0)),
                  pl.BlockSpec((1, rg, d), lambda i: (i // tiles_per_group, 0, 0)),
                  pl.BlockSpec((1, d), lambda i: (0, 0)),
                  pl.BlockSpec((d, 6 * W), lambda i: (0, 0)),
                  pl.BlockSpec((tm, W), lambda i: (i % tiles_per_group, 0)),
                  pl.BlockSpec((tm, W), lambda i: (i % tiles_per_group, 0))],
        out_specs=[o[0] for o in outs],
        out_shape=[o[1] for o in outs],
        compiler_params=_params(("arbitrary",)),
        name="in_proj_odd",
    )(x, shift, scale, g.reshape(1, d), w_in.astype(BF16), cos, sin)


def _rope_tables(positions):
    half = HEAD_DIM // 2
    freqs = ROPE_THETA ** (-jnp.arange(half, dtype=F32) / half)
    ang = positions.astype(F32)[:, None] * freqs[None, :]
    cos = jnp.cos(ang)
    sin = jnp.sin(ang)
    heads = HALF_WIDTH // HEAD_DIM
    cos_t = jnp.tile(jnp.concatenate([cos, cos], axis=-1), (1, heads))
    sin_t = jnp.tile(jnp.concatenate([-sin, sin], axis=-1), (1, heads))
    return cos_t, sin_t


def _out_proj_kernel(ya_ref, yb_ref, x_ref, gate_ref, w_ref, fg_ref, o_ref, *, final_norm):
    W = HALF_WIDTH
    out = _dot(ya_ref[...].astype(BF16), w_ref[0:W, :]) + _dot(yb_ref[...].astype(BF16), w_ref[W:2 * W, :])
    y = x_ref[...] + gate_ref[0] * out
    if final_norm:
        ms = jnp.mean(y * y, axis=-1, keepdims=True)
        y = y * lax.rsqrt(ms + NORM_EPS) * fg_ref[...]
    o_ref[...] = y


def _out_proj(ya, yb, x, gate, w_out, final_g, tm, final_norm):
    rows, d = x.shape
    W = HALF_WIDTH
    nt = rows // tm
    groups = gate.shape[0]
    tiles_per_group = nt // groups
    rg = gate.shape[1]
    return pl.pallas_call(
        functools.partial(_out_proj_kernel, final_norm=final_norm),
        grid=(nt,),
        in_specs=[pl.BlockSpec((tm, W), lambda i: (i, 0)),
                  pl.BlockSpec((tm, W), lambda i: (i, 0)),
                  pl.BlockSpec((tm, d), lambda i: (i, 0)),
                  pl.BlockSpec((1, rg, d), lambda i: (i // tiles_per_group, 0, 0)),
                  pl.BlockSpec((2 * W, d), lambda i: (0, 0)),
                  pl.BlockSpec((1, d), lambda i: (0, 0))],
        out_specs=pl.BlockSpec((tm, d), lambda i: (i, 0)),
        out_shape=jax.ShapeDtypeStruct((rows, d), F32),
        compiler_params=_params(("arbitrary",)),
        name="out_proj",
    )(ya, yb, x, gate, w_out.astype(BF16), final_g.reshape(1, d))


def _sb_prompt_kernel(q_ref, k_ref, v_ref, zs_ref, o_ref, knorm_ref, *, tq):
    qi = pl.program_id(2)
    q = q_ref[...]
    lane = _iota((tq, LANES), 1)
    head0 = lane < HEAD_DIM
    qs = (jnp.where(head0, q, jnp.zeros_like(q)), jnp.where(head0, jnp.zeros_like(q), q))

    def head_sq_norms(x):
        sq = x.astype(F32) * x.astype(F32)
        return (jnp.sum(jnp.where(head0, sq, 0.0), axis=1, keepdims=True),
                jnp.sum(jnp.where(head0, 0.0, sq), axis=1, keepdims=True))

    @pl.when(qi == 0)
    def _():
        def scan(j, m):
            n0, n1 = head_sq_norms(k_ref[pl.ds(pl.multiple_of(j * tq, tq), tq), :])
            return (jnp.maximum(m[0], jnp.max(n0, axis=0, keepdims=True)),
                    jnp.maximum(m[1], jnp.max(n1, axis=0, keepdims=True)))
        zero = jnp.zeros((1, 1), F32)
        m0, m1 = lax.fori_loop(0, k_ref.shape[0] // tq, scan, (zero, zero))
        knorm_ref[...] = jnp.where(_iota((1, LANES), 1) == 0, m0, m1)

    qn0, qn1 = head_sq_norms(q)
    zb0 = jnp.sqrt(qn0 * knorm_ref[0:1, 0:1]) * SB_BOUND_SLACK
    zb1 = jnp.sqrt(qn1 * knorm_ref[0:1, 1:2]) * SB_BOUND_SLACK
    row = _iota((tq, tq), 0)
    col = _iota((tq, tq), 1)
    suffix = jnp.where(row >= col, 1.0, 0.0).astype(BF16)
    causal = col < row

    def blocks(items, carry):
        acc, c0, c1 = carry
        loaded, parts = [], []
        for j, diag, valid in items:
            start = pl.multiple_of(j * tq, tq)
            kb = k_ref[pl.ds(start, tq), :]
            loaded.append((v_ref[pl.ds(start, tq), :], diag, valid))
            per_head = []
            for qh in qs:
                z = _dot_nt(qh, kb)
                ls = -_softplus_scores(z)
                if diag:
                    ls = jnp.where(causal, ls, 0.0)
                per_head.append((z, _dot(ls.astype(BF16), suffix)))
            parts.append(per_head)
        cs = [c0, c1]
        for (vb, diag, valid), per_head in zip(loaded, parts):
            pvs = []
            for h, (z, s) in enumerate(per_head):
                w = jnp.exp(z + s + cs[h])
                if diag:
                    w = jnp.where(causal, w, 0.0)
                total = s[:, 0:1]
                if valid is not None:
                    w = jnp.where(valid, w, 0.0)
                    total = jnp.where(valid, total, 0.0)
                pvs.append(_dot(w.astype(BF16), vb))
                cs[h] = cs[h] + total
            acc = acc + jnp.where(head0, pvs[0], pvs[1])
        return acc, cs[0], cs[1]

    def exhausted(c0, c1):
        worst = jnp.max(jnp.maximum(zb0 + c0, zb1 + c1))
        return (worst < SB_EXP_UNDERFLOW).astype(jnp.int32)

    def step(state):
        j, _, acc, c0, c1 = state
        acc, c0, c1 = blocks([(j, False, None)], (acc, c0, c1))
        return j - 1, exhausted(c0, c1), acc, c0, c1

    zero_c = jnp.zeros((tq, 1), F32)
    acc, c0, c1 = blocks([(qi, True, None), (jnp.maximum(qi - 1, 0), False, qi > 0)],
                         (jnp.zeros((tq, LANES), F32), zero_c, zero_c))
    state = lax.while_loop(lambda s: jnp.logical_and(s[0] >= 0, s[1] == 0), step,
                           (qi - 2, exhausted(c0, c1), acc, c0, c1))
    o_ref[...] = (state[2] * zs_ref[...]).astype(o_ref.dtype)


def _sb_prompt(q, k, v, zs, batch, seq, tq):
    rows, W = q.shape
    nq = seq // tq
    pairs = W // LANES
    return pl.pallas_call(
        functools.partial(_sb_prompt_kernel, tq=tq),
        grid=(batch, pairs, nq),
        in_specs=[pl.BlockSpec((tq, LANES), lambda b, p, i: (b * nq + i, p)),
                  pl.BlockSpec((seq, LANES), lambda b, p, i: (b, p)),
                  pl.BlockSpec((seq, LANES), lambda b, p, i: (b, p)),
                  pl.BlockSpec((tq, LANES), lambda b, p, i: (b * nq + i, p))],
        out_specs=pl.BlockSpec((tq, LANES), lambda b, p, i: (b * nq + i, p)),
        out_shape=jax.ShapeDtypeStruct((rows, W), BF16),
        scratch_shapes=[pltpu.VMEM((1, LANES), F32)],
        compiler_params=_params(("arbitrary", "arbitrary", "arbitrary")),
        name="sb_prompt",
    )(q, k, v, zs)


def _moba_prompt_kernel(q_ref, qf_ref, k_ref, v_ref, km_ref, zs_ref, o_ref, kpad_ref, *, tq, nb):
    qi = pl.program_id(2)

    @pl.when(qi == 0)
    def _():
        kpad_ref[...] = jnp.zeros_like(kpad_ref)
        kpad_ref[0:nb, :] = km_ref[0]

    q = q_ref[...]
    qf = qf_ref[...]
    lane = _iota((tq, LANES), 1)
    head0 = lane < HEAD_DIM
    row = _iota((tq, tq), 0)
    col = _iota((tq, tq), 1)
    kmean = kpad_ref[...]
    nbp = -(-nb // SUBLANES) * SUBLANES
    blk = _iota((nbp, tq), 0)
    blk_f = blk.astype(F32)

    qs, flags = [], []
    for h in range(2):
        hm = head0 if h == 0 else jnp.logical_not(head0)
        qs.append(jnp.where(hm, q, jnp.zeros_like(q)))
        g = _dot_nt(kmean, jnp.where(hm, qf, 0.0), HIGHEST)[0:nbp, :]
        g = jnp.where(blk < qi, g, NEG_INF)
        sel = jnp.zeros((nbp, tq), F32)
        for _ in range(MOBA_TOPK):
            mx = jnp.max(g, axis=0, keepdims=True)
            is_max = jnp.logical_and(g == mx, mx > NEG_INF)
            idx = jnp.min(jnp.where(is_max, blk_f, float(nbp)), axis=0, keepdims=True)
            pick = blk_f == idx
            sel = jnp.where(pick, 1.0, sel)
            g = jnp.where(pick, NEG_INF, g)
        not_sel = jnp.concatenate([jnp.where(sel > 0.0, 0.0, 1.0), jnp.ones((LANES - nbp, tq), F32)], axis=0)
        flags.append(jnp.transpose(not_sel))

    lhs = [jnp.where(head0, qs[0], pltpu.roll(flags[0], HEAD_DIM, 1).astype(BF16)),
           jnp.where(head0, flags[1].astype(BF16), qs[1])]
    ones = jnp.ones((tq, LANES), BF16)

    def values(vb):
        return jnp.where(head0, vb, ones), jnp.where(head0, ones, vb)

    start = pl.multiple_of(qi * tq, tq)
    kb = k_ref[pl.ds(start, tq), :]
    vmods = values(v_ref[pl.ds(start, tq), :])
    state = []
    for h in range(2):
        s = jnp.where(col <= row, _dot_nt(qs[h], kb), NEG_INF)
        m = jnp.max(s, axis=1, keepdims=True)
        p = jnp.exp(s - m)
        state += [m, _dot(p.astype(BF16), vmods[h])]

    def blocks(ns, st):
        ks, vs = [], []
        for n in ns:
            start_n = pl.multiple_of(n * tq, tq)
            kn = k_ref[pl.ds(start_n, tq), :]
            pen = [jnp.where(lane == n + off, MOBA_MASK_PENALTY, 0.0).astype(BF16) for off in (HEAD_DIM, 0)]
            ks.append((jnp.where(head0, kn, pen[0]), jnp.where(head0, pen[1], kn)))
            vs.append(values(v_ref[pl.ds(start_n, tq), :]))
        ss = [[_dot_nt(lhs[h], kn[h]) for kn in ks] for h in range(2)]
        new = []
        for h in range(2):
            m, acc = st[2 * h:2 * h + 2]
            m_new = m
            for s in ss[h]:
                m_new = jnp.maximum(m_new, jnp.max(s, axis=1, keepdims=True))
            ps = [jnp.exp(s - m_new).astype(BF16) for s in ss[h]]
            acc = jnp.exp(m - m_new) * acc
            for p, vn in zip(ps, vs):
                acc = acc + _dot(p, vn[h])
            new += [m_new, acc]
        return tuple(new)

    group = MOBA_GROUP if nb % MOBA_GROUP == 0 else 1
    st = lax.fori_loop(0, (qi + group - 1) // group,
                       lambda i, s: blocks([group * i + r for r in range(group)], s), tuple(state))
    o = jnp.where(head0, st[1] / pltpu.roll(st[1], HEAD_DIM, 1), st[3] / pltpu.roll(st[3], HEAD_DIM, 1))
    o_ref[...] = (o * zs_ref[...]).astype(o_ref.dtype)


def _moba_prompt(q, qf, k, v, kmean, zs, batch, seq):
    rows, W = q.shape
    tq = MOBA_BLOCK
    nq = seq // tq
    pairs = W // LANES
    assert nq <= HEAD_DIM
    return pl.pallas_call(
        functools.partial(_moba_prompt_kernel, tq=tq, nb=nq),
        grid=(batch, pairs, nq),
        in_specs=[pl.BlockSpec((tq, LANES), lambda b, p, i: (b * nq + i, p)),
                  pl.BlockSpec((tq, LANES), lambda b, p, i: (b * nq + i, p)),
                  pl.BlockSpec((seq, LANES), lambda b, p, i: (b, p)),
                  pl.BlockSpec((seq, LANES), lambda b, p, i: (b, p)),
                  pl.BlockSpec((1, nq, LANES), lambda b, p, i: (b, 0, p)),
                  pl.BlockSpec((tq, LANES), lambda b, p, i: (b * nq + i, p))],
        out_specs=pl.BlockSpec((tq, LANES), lambda b, p, i: (b * nq + i, p)),
        out_shape=jax.ShapeDtypeStruct((rows, W), BF16),
        scratch_shapes=[pltpu.VMEM((LANES, LANES), F32)],
        compiler_params=_params(("arbitrary", "arbitrary", "arbitrary")),
        name="moba_prompt",
    )(q, qf, k, v, kmean.reshape(batch, nq, W), zs)


def _mlstm_kernel(qk_ref, v_ref, gb_ref, gt_ref, conv0_ref, c0_ref, m0_ref,
                  cw_ref, cb_ref, brow_ref,
                  y_ref, conv_out_ref, c_out_ref, m_out_ref,
                  xpad_ref, caug_ref, m_ref, *, L):
    c = pl.program_id(1)
    nc = pl.num_programs(1)
    H, DK = ML_HEADS, ML_HEAD_DIM
    W = HALF_WIDTH
    PADR = SUBLANES

    @pl.when(c == 0)
    def _():
        xpad_ref[0:PADR, :] = conv0_ref[0]
        caug_ref[...] = c0_ref[0]
        m_ref[...] = m0_ref[0]

    xpad_ref[PADR:PADR + L, :] = qk_ref[...]
    y = cb_ref[...]
    for j in range(CONV_WIDTH):
        off = PADR - (CONV_WIDTH - 1) + j
        y = y + cw_ref[j:j + 1, :] * xpad_ref[off:off + L, :]
    tail = xpad_ref[L:L + PADR, :]
    xpad_ref[0:PADR, :] = tail
    conv_out_ref[0] = tail
    qk = _silu(y)

    lane = _iota((L, LANES), 1)
    is_f_col = jnp.logical_and(lane >= H, lane < 2 * H)
    gcol = gt_ref[0] + brow_ref[...]
    lf_col = jnp.where(is_f_col, _log_sigmoid(gcol), 0.0)
    grow = jnp.transpose(gcol)[0:2 * H, :]
    sub = _iota((2 * H, L), 0)
    lf_row = jnp.where(sub >= H, _log_sigmoid(grow), 0.0)
    row = _iota((L, L), 0)
    col = _iota((L, L), 1)
    causal = col <= row
    tri = jnp.where(causal, 1.0, 0.0)
    f_col = _dot(tri, lf_col, HIGHEST)
    f_row = _dot_nt(lf_row, tri, HIGHEST)
    ones_col = jnp.where(_iota((L, DK), 1) == 0, 1.0, 0.0).astype(BF16)
    m_all = m_ref[...]
    m_next = m_all
    lane1 = _iota((1, LANES), 1)

    for h in range(H):
        fc = f_col[:, H + h:H + h + 1]
        fr = f_row[H + h:H + h + 1, :]
        li_c = gcol[:, h:h + 1]
        li_r = grow[h:h + 1, :]
        m_prev = m_all[:, h:h + 1]
        inter = fc + m_prev
        intra = jnp.where(causal, fc - fr + li_r, NEG_INF)
        mt = jnp.maximum(inter, jnp.max(intra, axis=1, keepdims=True))
        w = jnp.exp(intra - mt)
        g = jnp.exp(inter - mt)
        qh = qk[:, h * DK:(h + 1) * DK].astype(BF16)
        kf = qk[:, W + h * DK:W + (h + 1) * DK] * (DK ** -0.5)
        kh = kf.astype(BF16)
        vaug = jnp.concatenate([v_ref[:, h * DK:(h + 1) * DK].astype(BF16), ones_col], axis=1)
        s = _dot_nt(qh, kh) * w
        nd = g * _dot(qh, caug_ref[h].astype(BF16)) + _dot(s.astype(BF16), vaug)
        den = nd[:, DK:DK + 1]
        hout = nd[:, 0:DK] / jnp.maximum(jnp.abs(den), jnp.exp(-mt))
        y_ref[:, h * DK:(h + 1) * DK] = (gb_ref[:, h * DK:(h + 1) * DK] * hout).astype(y_ref.dtype)
        m_new = mt[L - 1:L, :]
        f_last = fc[L - 1:L, :]
        decay = jnp.exp(f_last + m_prev - m_new)
        ws = jnp.exp(f_last - fc + li_c - m_new)
        caug_ref[h] = decay * caug_ref[h] + _dot_tn((kf * ws).astype(BF16), vaug)
        m_next = jnp.where(lane1 == h, m_new, m_next)

    m_ref[...] = m_next

    @pl.when(c == nc - 1)
    def _():
        c_out_ref[0] = caug_ref[...]
        m_out_ref[0] = m_next


def _mlstm(qk_pre, vb, gb, gates, conv0, c0aug, m0, conv_w, conv_b, b_ig, b_fg, batch, L):
    rows, W2 = qk_pre.shape
    W = HALF_WIDTH
    H = ML_HEADS
    nc = rows // batch // L
    brow = jnp.pad(jnp.concatenate([b_ig, b_fg]).reshape(1, 2 * H), ((0, 0), (0, LANES - 2 * H)))
    y_dtype = BF16 if L % 16 == 0 else F32
    out_shapes = [jax.ShapeDtypeStruct((rows, W), y_dtype),
                  jax.ShapeDtypeStruct((batch, SUBLANES, W2), F32),
                  jax.ShapeDtypeStruct((batch, H, ML_HEAD_DIM, 2 * ML_HEAD_DIM), F32),
                  jax.ShapeDtypeStruct((batch, 1, LANES), F32)]
    return pl.pallas_call(
        functools.partial(_mlstm_kernel, L=L),
        grid=(batch, nc),
        in_specs=[pl.BlockSpec((L, W2), lambda b, c: (b * nc + c, 0)),
                  pl.BlockSpec((L, W), lambda b, c: (b * nc + c, 0)),
                  pl.BlockSpec((L, W), lambda b, c: (b * nc + c, 0)),
                  pl.BlockSpec((1, L, LANES), lambda b, c: (b * nc + c, 0, 0)),
                  pl.BlockSpec((1, SUBLANES, W2), lambda b, c: (b, 0, 0)),
                  pl.BlockSpec((1, H, ML_HEAD_DIM, 2 * ML_HEAD_DIM), lambda b, c: (b, 0, 0, 0)),
                  pl.BlockSpec((1, 1, LANES), lambda b, c: (b, 0, 0)),
                  pl.BlockSpec((CONV_WIDTH, W2), lambda b, c: (0, 0)),
                  pl.BlockSpec((1, W2), lambda b, c: (0, 0)),
                  pl.BlockSpec((1, LANES), lambda b, c: (0, 0))],
        out_specs=[pl.BlockSpec((L, W), lambda b, c: (b * nc + c, 0)),
                   pl.BlockSpec((1, SUBLANES, W2), lambda b, c: (b, 0, 0)),
                   pl.BlockSpec((1, H, ML_HEAD_DIM, 2 * ML_HEAD_DIM), lambda b, c: (b, 0, 0, 0)),
                   pl.BlockSpec((1, 1, LANES), lambda b, c: (b, 0, 0))],
        out_shape=out_shapes,
        scratch_shapes=[pltpu.VMEM((L + SUBLANES, W2), F32),
                        pltpu.VMEM((H, ML_HEAD_DIM, 2 * ML_HEAD_DIM), F32),
                        pltpu.VMEM((1, LANES), F32)],
        compiler_params=_params(("arbitrary", "arbitrary")),
        name="mlstm",
    )(qk_pre, vb, gb, gates, conv0, c0aug, m0, conv_w, conv_b.reshape(1, W2), brow)


def _rglru_gates(xc, wr_ref, br_ref, wi_ref, bi_ref, lam_ref):
    xb = xc.astype(BF16)
    r = _sigmoid(_dot(xb, wr_ref[...]) + br_ref[...])
    i = _sigmoid(_dot(xb, wi_ref[...]) + bi_ref[...])
    log_a = RG_C * r * _log_sigmoid(lam_ref[...])
    a = jnp.exp(log_a)
    b = jnp.sqrt(-jnp.tanh(log_a) * (a * a + 1.0)) * (i * xc)
    return a, b


def _rglru_prompt_kernel(xd_ref, zs_ref, conv0_ref, h0_ref, cw_ref, cb_ref, wr_ref, br_ref, wi_ref, bi_ref,
                         lam_ref, y_ref, conv_out_ref, h_out_ref,
                         xpad_ref, a_ref, b_ref, hs_ref, hc_ref, *, L):
    c = pl.program_id(1)
    PADR = SUBLANES

    @pl.when(c == 0)
    def _():
        xpad_ref[0:PADR, :] = conv0_ref[0]
        hc_ref[...] = h0_ref[0]

    xpad_ref[PADR:PADR + L, :] = xd_ref[...]
    xc = cb_ref[...]
    for j in range(CONV_WIDTH):
        off = PADR - (CONV_WIDTH - 1) + j
        xc = xc + cw_ref[j:j + 1, :] * xpad_ref[off:off + L, :]
    tail = xpad_ref[L:L + PADR, :]
    xpad_ref[0:PADR, :] = tail
    conv_out_ref[0] = tail

    a, b = _rglru_gates(xc, wr_ref, br_ref, wi_ref, bi_ref, lam_ref)
    a_ref[...] = a
    b_ref[...] = b

    def step(t, h):
        h = a_ref[pl.ds(t, 1), :] * h + b_ref[pl.ds(t, 1), :]
        hs_ref[pl.ds(t, 1), :] = h
        return h

    h = lax.fori_loop(0, L, step, hc_ref[...], unroll=8)
    hc_ref[...] = h
    h_out_ref[0] = h
    y_ref[...] = (hs_ref[...] * zs_ref[...]).astype(y_ref.dtype)


def _block_diag(w):
    g, n, _ = w.shape
    eye = jnp.eye(g, dtype=w.dtype)
    return (eye[:, None, :, None] * w[:, :, None, :]).reshape(g * n, g * n)


def _rglru_prompt(xd, zs, conv0, h0, conv_w, conv_b, wr, br, wi, bi, lam, batch, L):
    rows, W = xd.shape
    nc = rows // batch // L
    vec = lambda a: a.reshape(1, W)
    const = lambda shape: pl.BlockSpec(shape, lambda b, c: tuple(0 for _ in shape))
    return pl.pallas_call(
        functools.partial(_rglru_prompt_kernel, L=L),
        grid=(batch, nc),
        in_specs=[pl.BlockSpec((L, W), lambda b, c: (b * nc + c, 0)),
                  pl.BlockSpec((L, W), lambda b, c: (b * nc + c, 0)),
                  pl.BlockSpec((1, SUBLANES, W), lambda b, c: (b, 0, 0)),
                  pl.BlockSpec((1, 1, W), lambda b, c: (b, 0, 0)),
                  const((CONV_WIDTH, W)), const((1, W)), const((W, W)), const((1, W)), const((W, W)),
                  const((1, W)), const((1, W))],
        out_specs=[pl.BlockSpec((L, W), lambda b, c: (b * nc + c, 0)),
                   pl.BlockSpec((1, SUBLANES, W), lambda b, c: (b, 0, 0)),
                   pl.BlockSpec((1, 1, W), lambda b, c: (b, 0, 0))],
        out_shape=[jax.ShapeDtypeStruct((rows, W), BF16),
                   jax.ShapeDtypeStruct((batch, SUBLANES, W), F32),
                   jax.ShapeDtypeStruct((batch, 1, W), F32)],
        scratch_shapes=[pltpu.VMEM((L + SUBLANES, W), F32), pltpu.VMEM((L, W), F32), pltpu.VMEM((L, W), F32),
                        pltpu.VMEM((L, W), F32), pltpu.VMEM((1, W), F32)],
        compiler_params=_params(("arbitrary", "arbitrary")),
        name="rglru_prompt",
    )(xd, zs, conv0, h0, conv_w, vec(conv_b), _block_diag(wr).astype(BF16), vec(br),
      _block_diag(wi).astype(BF16), vec(bi), vec(lam))


def _rglru_sample_kernel(xd_ref, zs_ref, conv0_ref, h0_ref, cw_ref, cb_ref, wr_ref, br_ref, wi_ref, bi_ref,
                         lam_ref, y_ref, h_out_ref, *, T):
    xs = [conv0_ref[j] for j in range(CONV_WIDTH - 1)] + [xd_ref[t] for t in range(T)]
    h = h0_ref[...]
    for t in range(T):
        xc = cb_ref[...]
        for j in range(CONV_WIDTH):
            xc = xc + cw_ref[j:j + 1, :] * xs[t + j]
        a, b = _rglru_gates(xc, wr_ref, br_ref, wi_ref, bi_ref, lam_ref)
        h = a * h + b
        y_ref[t] = h * zs_ref[t]
    h_out_ref[...] = h


def _rglru_sample(xd, zs, conv0, h0, conv_w, conv_b, wr, br, wi, bi, lam):
    T, B, W = xd.shape
    vec = lambda a: a.reshape(1, W)
    return pl.pallas_call(
        functools.partial(_rglru_sample_kernel, T=T),
        out_shape=[jax.ShapeDtypeStruct((T, B, W), F32), jax.ShapeDtypeStruct((B, W), F32)],
        compiler_params=pltpu.CompilerParams(vmem_limit_bytes=VMEM_LIMIT),
        name="rglru_sample",
    )(xd, zs, conv0, h0, conv_w, vec(conv_b), _block_diag(wr).astype(BF16), vec(br),
      _block_diag(wi).astype(BF16), vec(bi), vec(lam))


def _query_rows(q):
    b, t, w = q.shape
    heads = w // HEAD_DIM
    onehot = (jnp.arange(heads)[:, None] == (jnp.arange(w) // HEAD_DIM)[None, :]).astype(q.dtype)
    return (q[:, :, None, :] * onehot[None, None]).reshape(b, t * heads, w)


def _channel_major(a, lanes):
    return jnp.pad(jnp.transpose(a, (0, 2, 1)), ((0, 0), (0, 0), (0, lanes - a.shape[1])))


def _pool_channel_major(pool):
    n_pool, page_rows, heads, hd = pool.shape
    return jnp.transpose(pool, (0, 2, 3, 1)).reshape(n_pool, heads * hd, page_rows)


def _page_specs(pages_per_step, page_rows, width, page_of):
    def spec(i):
        return pl.BlockSpec((None, width, page_rows), lambda b, c, pt: (pt[b, page_of(c, i)], 0, 0))
    return [spec(i) for i in range(pages_per_step)]


def _head_diagonal(o, t_new, zs):
    heads = HALF_WIDTH // HEAD_DIM
    rows = t_new * heads
    keep = (_iota((rows, HALF_WIDTH), 1) // HEAD_DIM) == (_iota((rows, HALF_WIDTH), 0) % heads)
    o = jnp.where(keep, o, 0.0)
    return jnp.sum(o.reshape(t_new, heads, HALF_WIDTH), axis=1) * zs


def _key_absmax_kernel(pt_ref, *rest, P):
    kpages, o_ref = rest[:P], rest[P]
    c = pl.program_id(1)
    m = jnp.abs(kpages[0][...])
    for i in range(1, P):
        m = jnp.maximum(m, jnp.abs(kpages[i][...]))

    @pl.when(c == 0)
    def _():
        o_ref[0] = m

    @pl.when(c > 0)
    def _():
        o_ref[0] = jnp.maximum(o_ref[0], m)


def _key_absmax(page_table, k_pool, P):
    B, n_pages = page_table.shape
    W, page_rows = k_pool.shape[1], k_pool.shape[2]
    grid_spec = pltpu.PrefetchScalarGridSpec(
        num_scalar_prefetch=1,
        grid=(B, n_pages // P),
        in_specs=_page_specs(P, page_rows, W, lambda c, i: c * P + i),
        out_specs=pl.BlockSpec((1, W, page_rows), lambda b, c, pt: (b, 0, 0)),
    )
    return pl.pallas_call(
        functools.partial(_key_absmax_kernel, P=P),
        grid_spec=grid_spec,
        out_shape=jax.ShapeDtypeStruct((B, W, page_rows), F32),
        compiler_params=_params(("arbitrary", "arbitrary")),
        name="key_absmax",
    )(page_table, *([k_pool] * P))


def _sb_sample_kernel(pt_ref, q_ref, knew_ref, vnew_ref, zs_ref, kabs_ref, acc0_ref, car0_ref, *rest,
                      P, t_new, newest):
    kpages, vpages = rest[:P], rest[P:2 * P]
    o_ref, acc_out_ref, car_out_ref, done_ref, acc_ref, car_ref = rest[2 * P:]
    c = pl.program_id(1)
    nch = pl.num_programs(1)
    q = q_ref[0]
    heads = HALF_WIDTH // HEAD_DIM
    rows = t_new * heads

    def key_blocks(kts, vts, mask):
        n = kts[0].shape[1]
        suffix = jnp.where(_iota((n, n), 0) >= _iota((n, n), 1), 1.0, 0.0).astype(BF16)
        zs = [_dot(q, kt.astype(BF16)) for kt in kts]
        lss = [-_softplus_scores(z) for z in zs]
        if mask is not None:
            lss = [jnp.where(mask, ls, 0.0) for ls in lss]
        sums = []
        for ls in lss:
            hi, lo = _split_bf16(ls)
            sums.append(_dot(hi, suffix) + _dot(lo, suffix))
        carry = car_ref[...]
        acc = acc_ref[...]
        for z, s, vt in zip(zs, sums, vts):
            w = jnp.exp(z + s + carry)
            if mask is not None:
                w = jnp.where(mask, w, 0.0)
            acc = acc + _dot_nt(w.astype(BF16), vt.astype(BF16))
            carry = carry + s[:, 0:1]
        acc_ref[...] = acc
        car_ref[...] = carry

    @pl.when(c == 0)
    def _():
        if newest:
            acc_ref[...] = jnp.zeros_like(acc_ref)
            car_ref[...] = jnp.zeros_like(car_ref)
            n = knew_ref.shape[2]
            j = _iota((rows, n), 1)
            t = _iota((rows, n), 0) // heads
            key_blocks([knew_ref[0]], [vnew_ref[0]], jnp.logical_and(j < t, j < t_new))
        else:
            acc_ref[...] = acc0_ref[0]
            car_ref[...] = car0_ref[0][:, 0:1]

    key_blocks([kpages[i][...] for i in reversed(range(P))], [vpages[i][...] for i in reversed(range(P))], None)

    @pl.when(c == nch - 1)
    def _():
        o_ref[0] = _head_diagonal(acc_ref[...], t_new, zs_ref[0])
        acc_out_ref[0] = acc_ref[...]
        car_out_ref[0] = jnp.broadcast_to(car_ref[...], (rows, LANES))
        bound = jnp.max(_dot(jnp.abs(q.astype(F32)), kabs_ref[0], HIGHEST), axis=1, keepdims=True)
        worst = jnp.max(bound * SB_BOUND_SLACK + car_ref[...])
        done_ref[0] = jnp.where(worst < SB_EXP_UNDERFLOW, 1.0, 0.0) * jnp.ones((1, LANES), F32)


def _sb_sample_pages(page_table, q_rows, knew_t, vnew_t, zs, kabs, acc0, car0, k_pool, v_pool, P, first_page,
                     n_used, newest):
    B = page_table.shape[0]
    t_new = zs.shape[1]
    W = HALF_WIDTH
    rows = q_rows.shape[1]
    page_rows = k_pool.shape[2]
    nch = n_used // P
    page_of = lambda c, i: first_page + (nch - 1 - c) * P + i
    per_b = lambda shape: pl.BlockSpec((1,) + shape, lambda b, c, pt: (b,) + tuple(0 for _ in shape))
    grid_spec = pltpu.PrefetchScalarGridSpec(
        num_scalar_prefetch=1,
        grid=(B, nch),
        in_specs=[per_b((rows, W)), per_b(knew_t.shape[1:]), per_b(vnew_t.shape[1:]), per_b((t_new, W)),
                  per_b((W, page_rows)), per_b((rows, W)), per_b((rows, LANES))]
        + _page_specs(P, page_rows, W, page_of) + _page_specs(P, page_rows, W, page_of),
        out_specs=[per_b((t_new, W)), per_b((rows, W)), per_b((rows, LANES)), per_b((1, LANES))],
        scratch_shapes=[pltpu.VMEM((rows, W), F32), pltpu.VMEM((rows, 1), F32)],
    )
    return pl.pallas_call(
        functools.partial(_sb_sample_kernel, P=P, t_new=t_new, newest=newest),
        grid_spec=grid_spec,
        out_shape=[jax.ShapeDtypeStruct((B, t_new, W), F32), jax.ShapeDtypeStruct((B, rows, W), F32),
                   jax.ShapeDtypeStruct((B, rows, LANES), F32), jax.ShapeDtypeStruct((B, 1, LANES), F32)],
        compiler_params=_params(("arbitrary", "arbitrary")),
        name="sb_sample_newest" if newest else "sb_sample_older",
    )(page_table, q_rows, knew_t, vnew_t, zs, kabs, acc0, car0, *([k_pool] * P), *([v_pool] * P))


def _sb_sample(page_table, q_rows, knew_t, vnew_t, zs, k_pool, v_pool, p_scan, p_newest):
    B, n_pages = page_table.shape
    rows = q_rows.shape[1]
    p_newest = min(p_newest, n_pages)
    kabs = _key_absmax(page_table, k_pool, p_scan)
    y, acc, car, done = _sb_sample_pages(
        page_table, q_rows, knew_t, vnew_t, zs, kabs, jnp.zeros((B, rows, HALF_WIDTH), F32),
        jnp.zeros((B, rows, LANES), F32), k_pool, v_pool, p_newest, n_pages - p_newest, p_newest, True)
    n_older = n_pages - p_newest
    if n_older == 0:
        return y
    p_older = max(p for p in range(1, p_scan + 1) if n_older % p == 0)
    older = lambda: _sb_sample_pages(page_table, q_rows, knew_t, vnew_t, zs, kabs, acc, car, k_pool, v_pool,
                                     p_older, 0, n_older, False)[0]
    return lax.cond(jnp.min(done) > 0.5, lambda: y, older)


def _moba_sample_kernel(pt_ref, q_ref, qg_ref, knew_ref, vnew_ref, zs_ref, *rest, P, t_new, nb, pb):
    kpages, vpages = rest[:P], rest[P:2 * P]
    o_ref, m_ref, l_ref, acc_ref, km_ref = rest[2 * P:]
    c = pl.program_id(1)
    nch = pl.num_programs(1)
    q = q_ref[0]
    heads = HALF_WIDTH // HEAD_DIM
    rows = t_new * heads
    lane = _iota((rows, LANES), 1)

    @pl.when(c == 0)
    def _():
        km_ref[...] = jnp.zeros_like(km_ref)
        m_ref[...] = jnp.zeros_like(m_ref)
        l_ref[...] = jnp.zeros_like(l_ref)

    nblk = P // pb
    zs = [[_dot(q, kpages[pb * s + r][...].astype(BF16)) for r in range(pb)] for s in range(nblk)]
    km_all, m_all, l_all = km_ref[...], m_ref[...], l_ref[...]
    km_lane = _iota(km_ref.shape, 1)
    for s in range(nblk):
        n = c * nblk + s
        ksum = kpages[pb * s][...]
        for r in range(1, pb):
            ksum = ksum + kpages[pb * s + r][...]
        kmean = jnp.sum(ksum, axis=1, keepdims=True) * (1.0 / MOBA_BLOCK)
        km_all = jnp.where(km_lane == n, kmean, km_all)
        m = jnp.max(zs[s][0], axis=1, keepdims=True)
        for z in zs[s][1:]:
            m = jnp.maximum(m, jnp.max(z, axis=1, keepdims=True))
        l = jnp.zeros((rows, 1), F32)
        a = jnp.zeros((rows, HALF_WIDTH), F32)
        for r, z in enumerate(zs[s]):
            p = jnp.exp(z - m)
            l = l + jnp.sum(p, axis=1, keepdims=True)
            a = a + _dot_nt(p.astype(BF16), vpages[pb * s + r][...].astype(BF16))
        m_all = jnp.where(lane == n, m, m_all)
        l_all = jnp.where(lane == n, l, l_all)
        acc_ref[n] = a
    km_ref[...] = km_all
    m_ref[...] = m_all
    l_ref[...] = l_all

    @pl.when(c == nch - 1)
    def _():
        lane_f = lane.astype(F32)
        g = _dot(qg_ref[0], km_ref[...], HIGHEST)
        g = jnp.where(lane < nb, g, NEG_INF)
        sel = jnp.zeros((rows, LANES), F32)
        for _ in range(min(MOBA_TOPK, nb)):
            mx = jnp.max(g, axis=1, keepdims=True)
            is_max = jnp.logical_and(g == mx, mx > NEG_INF)
            idx = jnp.min(jnp.where(is_max, lane_f, float(LANES)), axis=1, keepdims=True)
            pick = lane_f == idx
            sel = jnp.where(pick, 1.0, sel)
            g = jnp.where(pick, NEG_INF, g)
        nn = knew_ref.shape[2]
        j = _iota((rows, nn), 1)
        t = _iota((rows, nn), 0) // heads
        zn = jnp.where(jnp.logical_and(j <= t, j < t_new), _dot(q, knew_ref[0].astype(BF16)), NEG_INF)
        m_all = m_ref[...]
        m_tot = jnp.maximum(jnp.max(jnp.where(sel > 0.0, m_all, NEG_INF), axis=1, keepdims=True),
                            jnp.max(zn, axis=1, keepdims=True))
        coef = jnp.where(sel > 0.0, jnp.exp(m_all - m_tot), 0.0)
        p_own = jnp.exp(zn - m_tot)
        denom = jnp.sum(coef * l_ref[...], axis=1, keepdims=True) + jnp.sum(p_own, axis=1, keepdims=True)
        o = _dot_nt(p_own.astype(BF16), vnew_ref[0].astype(BF16))
        for n in range(nb):
            o = o + coef[:, n:n + 1] * acc_ref[n]
        o_ref[0] = _head_diagonal(o / denom, t_new, zs_ref[0])


def _moba_sample(page_table, q_rows, qg_rows, knew_t, vnew_t, zs, k_pool, v_pool, P):
    B, n_pages = page_table.shape
    t_new = zs.shape[1]
    W = HALF_WIDTH
    rows = q_rows.shape[1]
    page_rows = k_pool.shape[2]
    pb = MOBA_BLOCK // page_rows
    nch = n_pages // P
    nb = n_pages // pb
    assert nb <= LANES and P % pb == 0
    page_of = lambda c, i: c * P + i
    per_b = lambda shape: pl.BlockSpec((1,) + shape, lambda b, c, pt: (b,) + tuple(0 for _ in shape))
    grid_spec = pltpu.PrefetchScalarGridSpec(
        num_scalar_prefetch=1,
        grid=(B, nch),
        in_specs=[per_b((rows, W)), per_b((rows, W)), per_b(knew_t.shape[1:]), per_b(vnew_t.shape[1:]),
                  per_b((t_new, W))]
        + _page_specs(P, page_rows, W, page_of) + _page_specs(P, page_rows, W, page_of),
        out_specs=per_b((t_new, W)),
        scratch_shapes=[pltpu.VMEM((rows, LANES), F32), pltpu.VMEM((rows, LANES), F32),
                        pltpu.VMEM((nb, rows, W), F32), pltpu.VMEM((W, LANES), F32)],
    )
    return pl.pallas_call(
        functools.partial(_moba_sample_kernel, P=P, t_new=t_new, nb=nb, pb=pb),
        grid_spec=grid_spec,
        out_shape=jax.ShapeDtypeStruct((B, t_new, W), F32),
        compiler_params=_params(("arbitrary", "arbitrary")),
        name="moba_sample",
    )(page_table, q_rows, qg_rows, knew_t, vnew_t, zs, *([k_pool] * P), *([v_pool] * P))


def _pad_rows(a, rows):
    return jnp.pad(a, ((0, 0), (0, rows - a.shape[1]), (0, 0)))


def kernel(x_prompt, x_sample, c_prompt, c_sample, page_table, cache_k_sb, cache_v_sb, state_conv_mlstm, state_c_mlstm, state_n_mlstm, state_m_mlstm, cache_k_moba, cache_v_moba, state_conv_rglru, state_h_rglru, norm_g_even, mod_w_even, mod_b_even, w_in_even, conv_w_even, conv_b_even, b_igate_even, b_fgate_even, w_out_even, norm_g_odd, mod_w_odd, mod_b_odd, w_in_odd, conv_w_odd, conv_b_odd, w_rgate_odd, b_rgate_odd, w_igate_odd, b_igate_odd, lru_lambda_odd, w_out_odd, final_g):
    B, T, D = x_prompt.shape
    DB, DT, _ = x_sample.shape
    W = HALF_WIDTH
    H = ML_HEADS
    heads = W // HEAD_DIM
    n_pool, page_rows = cache_k_sb.shape[1], cache_k_sb.shape[2]
    past = page_table.shape[1] * page_rows
    depth = norm_g_even.shape[0] + norm_g_odd.shape[0]
    TM = 256
    TP = 512 if T % 512 == 0 else TM
    n_pages = page_table.shape[1]
    PAGES_PER_STEP = max(p for p in range(2, 17, 2) if n_pages % p == 0)
    SB_NEWEST_PAGES = 4
    NEW_KEYS = LANES
    ML_PAD = SUBLANES

    xp = x_prompt.reshape(B * T, D)
    xs = x_sample.reshape(DB * DT, D)
    c_all = jnp.concatenate([c_prompt, c_sample], axis=0)
    c_rows = -(-c_all.shape[0] // SUBLANES) * SUBLANES
    c_all = jnp.pad(c_all, ((0, c_rows - c_all.shape[0]), (0, 0)))
    cos_p, sin_p = _rope_tables(jnp.arange(T))
    cos_s, sin_s = _rope_tables(past + jnp.arange(DT))
    cos_s, sin_s = jnp.tile(cos_s, (DB, 1)), jnp.tile(sin_s, (DB, 1))

    def mods(w, b):
        mod = _modulation(c_all, w, b)
        parts = []
        for part in jnp.split(mod, 3, axis=-1):
            p_part = part[:B].reshape(B, 1, D)
            s_part = jnp.repeat(part[B:B + DB], DT, axis=0).reshape(1, DB * DT, D)
            parts.append((p_part, s_part))
        return parts

    p_even, s_even, p_odd, s_odd = [], [], [], []
    y_prompt = y_sample = None
    for l in range(depth):
        j = l // 2
        last = l == depth - 1
        if l % 2 == 0:
            (sh_p, sh_s), (sc_p, sc_s), (ga_p, ga_s) = mods(mod_w_even[j], mod_b_even[j])
            qa, ka, kab, va, vab, za, qk, vb, gb, gt = _in_proj_even(
                xp, sh_p, sc_p, norm_g_even[j], w_in_even[j], TP)
            ya = _sb_prompt(qa, kab, vab, za, B, T, TM)
            yb, conv_p, caug_p, m_p = _mlstm(
                qk, vb, gb, gt.reshape(B * T // TM, TM, LANES),
                jnp.zeros((B, SUBLANES, 2 * W), F32), jnp.zeros((B, H, ML_HEAD_DIM, 2 * ML_HEAD_DIM), F32),
                jnp.zeros((B, 1, LANES), F32),
                conv_w_even[j], conv_b_even[j], b_igate_even[j], b_fgate_even[j], B, TM)
            xp = _out_proj(ya, yb, xp, ga_p, w_out_even[j], final_g, TP, last)
            p_even.append((ka.reshape(B, T, heads, HEAD_DIM), va.reshape(B, T, heads, HEAD_DIM),
                           conv_p[:, SUBLANES - (CONV_WIDTH - 1):], caug_p[..., :ML_HEAD_DIM],
                           caug_p[..., ML_HEAD_DIM], m_p[:, 0, :H]))
            qa, ka, kab, va, vab, za, qk, vb, gb, gt = _in_proj_even(
                xs, sh_s, sc_s, norm_g_even[j], w_in_even[j], DB * DT)
            ya = _sb_sample(page_table, _query_rows(qa.reshape(DB, DT, W)),
                            _channel_major(ka.reshape(DB, DT, W), NEW_KEYS),
                            _channel_major(va.reshape(DB, DT, W), NEW_KEYS), za.reshape(DB, DT, W),
                            _pool_channel_major(cache_k_sb[j]), _pool_channel_major(cache_v_sb[j]),
                            PAGES_PER_STEP, SB_NEWEST_PAGES).reshape(DB * DT, W)
            pad_t = lambda a: _pad_rows(a.reshape(DB, DT, a.shape[-1]), ML_PAD).reshape(DB * ML_PAD, a.shape[-1])
            g_rows = gt.reshape(DB, DT, LANES)[:, :, :2 * H]
            pad_gate = jnp.concatenate([jnp.full((H,), NEG_INF, F32), jnp.full((H,), jnp.inf, F32)])
            g_rows = jnp.concatenate([g_rows, jnp.broadcast_to(pad_gate, (DB, ML_PAD - DT, 2 * H))], axis=1)
            gt_s = jnp.pad(g_rows, ((0, 0), (0, 0), (0, LANES - 2 * H)))
            conv0 = jnp.pad(state_conv_mlstm[j], ((0, 0), (SUBLANES - (CONV_WIDTH - 1), 0), (0, 0)))
            c0aug = jnp.concatenate(
                [state_c_mlstm[j].astype(F32), state_n_mlstm[j].astype(F32)[..., None],
                 jnp.zeros((DB, H, ML_HEAD_DIM, ML_HEAD_DIM - 1), F32)], axis=-1)
            m0 = jnp.pad(state_m_mlstm[j].astype(F32), ((0, 0), (0, LANES - H))).reshape(DB, 1, LANES)
            yb, _, caug_s, m_s = _mlstm(
                pad_t(qk), pad_t(vb), pad_t(gb), gt_s, conv0, c0aug, m0,
                conv_w_even[j], conv_b_even[j], b_igate_even[j], b_fgate_even[j], DB, ML_PAD)
            yb = yb.reshape(DB, ML_PAD, W)[:, :DT].reshape(DB * DT, W)
            conv_s = jnp.concatenate([state_conv_mlstm[j].astype(F32), qk.reshape(DB, DT, 2 * W)],
                                     axis=1)[:, -(CONV_WIDTH - 1):]
            xs = _out_proj(ya, yb, xs, ga_s, w_out_even[j], final_g, DB * DT, last)
            s_even.append((ka.reshape(DB, DT, heads, HEAD_DIM), va.reshape(DB, DT, heads, HEAD_DIM),
                           conv_s, caug_s[..., :ML_HEAD_DIM], caug_s[..., ML_HEAD_DIM], m_s[:, 0, :H]))
        else:
            (sh_p, sh_s), (sc_p, sc_s), (ga_p, ga_s) = mods(mod_w_odd[j], mod_b_odd[j])
            lru = (conv_w_odd[j], conv_b_odd[j], w_rgate_odd[j], b_rgate_odd[j], w_igate_odd[j], b_igate_odd[j],
                   lru_lambda_odd[j])
            qb, qf, kc, kcb, vc, vcb, zc, xd, zd, km = _in_proj_odd(
                xp, sh_p, sc_p, norm_g_odd[j], w_in_odd[j], cos_p, sin_p, TP)
            yc = _moba_prompt(qb, qf, kcb, vcb, km, zc, B, T)
            yd, conv_p, h_p = _rglru_prompt(xd, zd, jnp.zeros((B, SUBLANES, W), F32), jnp.zeros((B, 1, W), F32),
                                            *lru, B, TM)
            xp = _out_proj(yc, yd, xp, ga_p, w_out_odd[j], final_g, TP, last)
            p_odd.append((kc.reshape(B, T, heads, HEAD_DIM), vc.reshape(B, T, heads, HEAD_DIM),
                          conv_p[:, SUBLANES - (CONV_WIDTH - 1):], h_p[:, 0]))
            qb, qf, kc, kcb, vc, vcb, zc, xd, zd, km = _in_proj_odd(
                xs, sh_s, sc_s, norm_g_odd[j], w_in_odd[j], cos_s, sin_s, DB * DT)
            yc = _moba_sample(page_table, _query_rows(qb.reshape(DB, DT, W)), _query_rows(qf.reshape(DB, DT, W)),
                              _channel_major(kc.reshape(DB, DT, W), NEW_KEYS),
                              _channel_major(vc.reshape(DB, DT, W), NEW_KEYS), zc.reshape(DB, DT, W),
                              _pool_channel_major(cache_k_moba[j]), _pool_channel_major(cache_v_moba[j]),
                              PAGES_PER_STEP).reshape(DB * DT, W)
            tmaj = lambda a: jnp.transpose(a.reshape(DB, DT, W), (1, 0, 2))
            yd, h_s = _rglru_sample(tmaj(xd), tmaj(zd), jnp.transpose(state_conv_rglru[j].astype(F32), (1, 0, 2)),
                                    state_h_rglru[j].astype(F32), *lru)
            yd = jnp.transpose(yd, (1, 0, 2)).reshape(DB * DT, W)
            conv_s = jnp.concatenate([state_conv_rglru[j].astype(F32), xd.reshape(DB, DT, W)],
                                     axis=1)[:, -(CONV_WIDTH - 1):]
            xs = _out_proj(yc, yd, xs, ga_s, w_out_odd[j], final_g, DB * DT, last)
            s_odd.append((kc.reshape(DB, DT, heads, HEAD_DIM), vc.reshape(DB, DT, heads, HEAD_DIM), conv_s, h_s))

    y_prompt = xp.reshape(B, T, D)
    y_sample = xs.reshape(DB, DT, D)
    stack = lambda group: [jnp.stack(a) for a in zip(*group)]
    pk_sb, pv_sb, pconv_m, pc_m, pn_m, pm_m = stack(p_even)
    sk_sb, sv_sb, sconv_m, sc_m, sn_m, sm_m = stack(s_even)
    pk_mb, pv_mb, pconv_d, ph_d = stack(p_odd)
    sk_mb, sv_mb, sconv_d, sh_d = stack(s_odd)
    return (y_prompt, y_sample,
            pk_sb, pv_sb, pconv_m, pc_m, pn_m, pm_m, pk_mb, pv_mb, pconv_d, ph_d,
            sk_sb, sv_sb, sconv_m, sc_m, sn_m, sm_m, sk_mb, sv_mb, sconv_d, sh_d)
```

```python
import functools

import jax
import jax.numpy as jnp
from jax import lax
from jax.experimental import pallas as pl
from jax.experimental.pallas import tpu as pltpu

F32 = jnp.float32
BF16 = jnp.bfloat16
HIGHEST = lax.Precision.HIGHEST

HEAD_DIM = 64
HALF_WIDTH = 512
ML_HEADS = 4
ML_HEAD_DIM = 128
CONV_WIDTH = 4
MOBA_BLOCK = 256
MOBA_TOPK = 3
RG_C = 8.0
ROPE_THETA = 10000.0
NORM_EPS = 1e-6
LANES = 128
SUBLANES = 8
VMEM_LIMIT = 56 * 1024 * 1024
NEG_INF = float("-inf")
SB_EXP_UNDERFLOW = -110.0
SB_BOUND_SLACK = 1.01
MOBA_MASK_PENALTY = -1e30
MOBA_GROUP = 4


def _params(sem):
    return pltpu.CompilerParams(dimension_semantics=sem, vmem_limit_bytes=VMEM_LIMIT)


def _sigmoid(x):
    return 1.0 / (1.0 + jnp.exp(-x))


def _silu(x):
    return x * _sigmoid(x)


def _softplus(x):
    return jnp.maximum(x, 0.0) + jnp.log1p(jnp.exp(-jnp.abs(x)))


def _softplus_scores(x):
    return jnp.maximum(x, 0.0) + jnp.log(1.0 + jnp.exp(-jnp.abs(x)))


def _log_sigmoid(x):
    return -_softplus(-x)


def _dot(a, b, precision=None):
    return jnp.dot(a, b, preferred_element_type=F32, precision=precision)


def _dot_nt(a, b, precision=None):
    return lax.dot_general(a, b, (((1,), (1,)), ((), ())), preferred_element_type=F32, precision=precision)


def _dot_tn(a, b, precision=None):
    return lax.dot_general(a, b, (((0,), (0,)), ((), ())), preferred_element_type=F32, precision=precision)


def _split_bf16(x):
    hi = x.astype(BF16)
    lo = (x - hi.astype(F32)).astype(BF16)
    return hi, lo


def _iota(shape, dim):
    return lax.broadcasted_iota(jnp.int32, shape, dim)


def _mod_kernel(c_ref, w_ref, b_ref, o_ref):
    o_ref[...] = _dot(_silu(c_ref[...]), w_ref[...], HIGHEST) + b_ref[...]


def _modulation(c, w, b):
    rows, d = c.shape
    n = w.shape[1]
    tn = 512
    return pl.pallas_call(
        _mod_kernel,
        grid=(n // tn,),
        in_specs=[pl.BlockSpec((rows, d), lambda j: (0, 0)),
                  pl.BlockSpec((d, tn), lambda j: (0, j)),
                  pl.BlockSpec((1, tn), lambda j: (0, j))],
        out_specs=pl.BlockSpec((rows, tn), lambda j: (0, j)),
        out_shape=jax.ShapeDtypeStruct((rows, n), F32),
        compiler_params=_params(("arbitrary",)),
        name="modulation",
    )(c, w, b.reshape(1, n))


def _normed(x_ref, shift_ref, scale_ref, g_ref):
    x = x_ref[...]
    ms = jnp.mean(x * x, axis=-1, keepdims=True)
    h = x * lax.rsqrt(ms + NORM_EPS) * g_ref[...]
    return h * (1.0 + scale_ref[0]) + shift_ref[0]


def _in_even_kernel(x_ref, shift_ref, scale_ref, g_ref, w_ref, wg_hi_ref, wg_lo_ref,
                    qa_ref, ka_ref, kab_ref, va_ref, vab_ref, za_ref, qk_ref, vb_ref, gb_ref, gt_ref):
    h = _normed(x_ref, shift_ref, scale_ref, g_ref)
    hb, h_lo = _split_bf16(h)
    W = HALF_WIDTH

    def proj(c):
        return _dot(hb, w_ref[:, c * W:(c + 1) * W])

    qa_ref[...] = (proj(0) * (HEAD_DIM ** -0.5)).astype(BF16)
    ka = proj(1)
    ka_ref[...] = ka
    kab_ref[...] = ka.astype(BF16)
    va = proj(2)
    va_ref[...] = va
    vab_ref[...] = va.astype(BF16)
    za_ref[...] = _silu(proj(3))
    qk_ref[:, 0:W] = proj(4)
    qk_ref[:, W:2 * W] = proj(5)
    vb_ref[...] = proj(6)
    gb_ref[...] = _sigmoid(proj(7)) * _silu(proj(8))
    gt_ref[0] = _dot(hb, wg_hi_ref[...]) + _dot(h_lo, wg_hi_ref[...]) + _dot(hb, wg_lo_ref[...])


def _in_proj_even(x, shift, scale, g, w_in, tm):
    rows, d = x.shape
    W = HALF_WIDTH
    nt = rows // tm
    groups = shift.shape[0]
    tiles_per_group = nt // groups
    w_main = w_in[:, :9 * W].astype(BF16)
    w_gate_hi, w_gate_lo = _split_bf16(jnp.pad(w_in[:, 9 * W:], ((0, 0), (0, LANES - 2 * ML_HEADS))))
    row_blk = lambda n, dt: (pl.BlockSpec((tm, n), lambda i: (i, 0)), jax.ShapeDtypeStruct((rows, n), dt))
    outs = [row_blk(W, BF16), row_blk(W, F32), row_blk(W, BF16), row_blk(W, F32), row_blk(W, BF16),
            row_blk(W, F32), row_blk(2 * W, F32), row_blk(W, F32), row_blk(W, F32),
            (pl.BlockSpec((1, tm, LANES), lambda i: (i, 0, 0)), jax.ShapeDtypeStruct((nt, tm, LANES), F32))]
    rg = shift.shape[1]
    return pl.pallas_call(
        _in_even_kernel,
        grid=(nt,),
        in_specs=[pl.BlockSpec((tm, d), lambda i: (i, 0)),
                  pl.BlockSpec((1, rg, d), lambda i: (i // tiles_per_group, 0, 0)),
                  pl.BlockSpec((1, rg, d), lambda i: (i // tiles_per_group, 0, 0)),
                  pl.BlockSpec((1, d), lambda i: (0, 0)),
                  pl.BlockSpec((d, 9 * W), lambda i: (0, 0)),
                  pl.BlockSpec((d, LANES), lambda i: (0, 0)),
                  pl.BlockSpec((d, LANES), lambda i: (0, 0))],
        out_specs=[o[0] for o in outs],
        out_shape=[o[1] for o in outs],
        compiler_params=_params(("arbitrary",)),
        name="in_proj_even",
    )(x, shift, scale, g.reshape(1, d), w_main, w_gate_hi, w_gate_lo)


def _in_odd_kernel(x_ref, shift_ref, scale_ref, g_ref, w_ref, cos_ref, sin_ref,
                   qb_ref, qf_ref, kc_ref, kcb_ref, vc_ref, vcb_ref, zc_ref, xd_ref, zd_ref, km_ref):
    h = _normed(x_ref, shift_ref, scale_ref, g_ref)
    hb = h.astype(BF16)
    W = HALF_WIDTH

    def proj(c):
        return _dot(hb, w_ref[:, c * W:(c + 1) * W])

    cos = cos_ref[...]
    sin = sin_ref[...]
    first_half = (_iota(cos.shape, 1) % HEAD_DIM) < (HEAD_DIM // 2)

    def rope(x):
        partner = jnp.where(first_half, pltpu.roll(x, W - HEAD_DIM // 2, 1), pltpu.roll(x, HEAD_DIM // 2, 1))
        return x * cos + partner * sin

    q = rope(proj(0))
    qf_ref[...] = q
    qb_ref[...] = (q * (HEAD_DIM ** -0.5)).astype(BF16)
    k = rope(proj(1))
    kc_ref[...] = k
    kcb_ref[...] = k.astype(BF16)
    blk = min(MOBA_BLOCK, k.shape[0])
    for r in range(k.shape[0] // blk):
        km_ref[r] = jnp.mean(k[r * blk:(r + 1) * blk], axis=0, keepdims=True)
    v = proj(2)
    vc_ref[...] = v
    vcb_ref[...] = v.astype(BF16)
    zc_ref[...] = _silu(proj(3))
    xd_ref[...] = proj(4)
    zd_ref[...] = _silu(proj(5))


def _in_proj_odd(x, shift, scale, g, w_in, cos, sin, tm):
    rows, d = x.shape
    W = HALF_WIDTH
    nt = rows // tm
    nkm = max(tm // MOBA_BLOCK, 1)
    groups = shift.shape[0]
    tiles_per_group = nt // groups
    rg = shift.shape[1]
    row_blk = lambda n, dt: (pl.BlockSpec((tm, n), lambda i: (i, 0)), jax.ShapeDtypeStruct((rows, n), dt))
    outs = [row_blk(W, BF16), row_blk(W, F32), row_blk(W, F32), row_blk(W, BF16), row_blk(W, F32),
            row_blk(W, BF16), row_blk(W, F32), row_blk(W, F32), row_blk(W, F32),
            (pl.BlockSpec((nkm, 1, W), lambda i: (i, 0, 0)), jax.ShapeDtypeStruct((nt * nkm, 1, W), F32))]
    return pl.pallas_call(
        _in_odd_kernel,
        grid=(nt,),
        in_specs=[pl.BlockSpec((tm, d), lambda i: (i, 0)),
                  pl.BlockSpec((1, rg, d), lambda i: (i // tiles_per_group, 0, 0)),
                  pl.BlockSpec((1, rg, d), lambda i: (i // tiles_per_group, 0, 0)),
                  pl.BlockSpec((1, d), lambda i: (0, 0)),
                  pl.BlockSpec((d, 6 * W), lambda i: (0, 0)),
                  pl.BlockSpec((tm, W), lambda i: (i % tiles_per_group, 0)),
                  pl.BlockSpec((tm, W), lambda i: (i % tiles_per_group, 0))],
        out_specs=[o[0] for o in outs],
        out_shape=[o[1] for o in outs],
        compiler_params=_params(("arbitrary",)),
        name="in_proj_odd",
    )(x, shift, scale, g.reshape(1, d), w_in.astype(BF16), cos, sin)


def _rope_tables(positions):
    half = HEAD_DIM // 2
    freqs = ROPE_THETA ** (-jnp.arange(half, dtype=F32) / half)
    ang = positions.astype(F32)[:, None] * freqs[None, :]
    cos = jnp.cos(ang)
    sin = jnp.sin(ang)
    heads = HALF_WIDTH // HEAD_DIM
    cos_t = jnp.tile(jnp.concatenate([cos, cos], axis=-1), (1, heads))
    sin_t = jnp.tile(jnp.concatenate([-sin, sin], axis=-1), (1, heads))
    return cos_t, sin_t


def _out_proj_kernel(ya_ref, yb_ref, x_ref, gate_ref, w_ref, fg_ref, o_ref, *, final_norm):
    W = HALF_WIDTH
    out = _dot(ya_ref[...].astype(BF16), w_ref[0:W, :]) + _dot(yb_ref[...].astype(BF16), w_ref[W:2 * W, :])
    y = x_ref[...] + gate_ref[0] * out
    if final_norm:
        ms = jnp.mean(y * y, axis=-1, keepdims=True)
        y = y * lax.rsqrt(ms + NORM_EPS) * fg_ref[...]
    o_ref[...] = y


def _out_proj(ya, yb, x, gate, w_out, final_g, tm, final_norm):
    rows, d = x.shape
    W = HALF_WIDTH
    nt = rows // tm
    groups = gate.shape[0]
    tiles_per_group = nt // groups
    rg = gate.shape[1]
    return pl.pallas_call(
        functools.partial(_out_proj_kernel, final_norm=final_norm),
        grid=(nt,),
        in_specs=[pl.BlockSpec((tm, W), lambda i: (i, 0)),
                  pl.BlockSpec((tm, W), lambda i: (i, 0)),
                  pl.BlockSpec((tm, d), lambda i: (i, 0)),
                  pl.BlockSpec((1, rg, d), lambda i: (i // tiles_per_group, 0, 0)),
                  pl.BlockSpec((2 * W, d), lambda i: (0, 0)),
                  pl.BlockSpec((1, d), lambda i: (0, 0))],
        out_specs=pl.BlockSpec((tm, d), lambda i: (i, 0)),
        out_shape=jax.ShapeDtypeStruct((rows, d), F32),
        compiler_params=_params(("arbitrary",)),
        name="out_proj",
    )(ya, yb, x, gate, w_out.astype(BF16), final_g.reshape(1, d))


def _sb_prompt_kernel(q_ref, k_ref, v_ref, zs_ref, o_ref, knorm_ref, *, tq):
    qi = pl.program_id(2)
    q = q_ref[...]
    lane = _iota((tq, LANES), 1)
    head0 = lane < HEAD_DIM
    qs = (jnp.where(head0, q, jnp.zeros_like(q)), jnp.where(head0, jnp.zeros_like(q), q))

    def head_sq_norms(x):
        sq = x.astype(F32) * x.astype(F32)
        return (jnp.sum(jnp.where(head0, sq, 0.0), axis=1, keepdims=True),
                jnp.sum(jnp.where(head0, 0.0, sq), axis=1, keepdims=True))

    @pl.when(qi == 0)
    def _():
        def scan(j, m):
            n0, n1 = head_sq_norms(k_ref[pl.ds(pl.multiple_of(j * tq, tq), tq), :])
            return (jnp.maximum(m[0], jnp.max(n0, axis=0, keepdims=True)),
                    jnp.maximum(m[1], jnp.max(n1, axis=0, keepdims=True)))
        zero = jnp.zeros((1, 1), F32)
        m0, m1 = lax.fori_loop(0, k_ref.shape[0] // tq, scan, (zero, zero))
        knorm_ref[...] = jnp.where(_iota((1, LANES), 1) == 0, m0, m1)

    qn0, qn1 = head_sq_norms(q)
    zb0 = jnp.sqrt(qn0 * knorm_ref[0:1, 0:1]) * SB_BOUND_SLACK
    zb1 = jnp.sqrt(qn1 * knorm_ref[0:1, 1:2]) * SB_BOUND_SLACK
    row = _iota((tq, tq), 0)
    col = _iota((tq, tq), 1)
    suffix = jnp.where(row >= col, 1.0, 0.0).astype(BF16)
    causal = col < row

    def blocks(items, carry):
        acc, c0, c1 = carry
        loaded, parts = [], []
        for j, diag, valid in items:
            start = pl.multiple_of(j * tq, tq)
            kb = k_ref[pl.ds(start, tq), :]
            loaded.append((v_ref[pl.ds(start, tq), :], diag, valid))
            per_head = []
            for qh in qs:
                z = _dot_nt(qh, kb)
                ls = -_softplus_scores(z)
                if diag:
                    ls = jnp.where(causal, ls, 0.0)
                per_head.append((z, _dot(ls.astype(BF16), suffix)))
            parts.append(per_head)
        cs = [c0, c1]
        for (vb, diag, valid), per_head in zip(loaded, parts):
            pvs = []
            for h, (z, s) in enumerate(per_head):
                w = jnp.exp(z + s + cs[h])
                if diag:
                    w = jnp.where(causal, w, 0.0)
                total = s[:, 0:1]
                if valid is not None:
                    w = jnp.where(valid, w, 0.0)
                    total = jnp.where(valid, total, 0.0)
                pvs.append(_dot(w.astype(BF16), vb))
                cs[h] = cs[h] + total
            acc = acc + jnp.where(head0, pvs[0], pvs[1])
        return acc, cs[0], cs[1]

    def exhausted(c0, c1):
        worst = jnp.max(jnp.maximum(zb0 + c0, zb1 + c1))
        return (worst < SB_EXP_UNDERFLOW).astype(jnp.int32)

    def step(state):
        j, _, acc, c0, c1 = state
        acc, c0, c1 = blocks([(j, False, None)], (acc, c0, c1))
        return j - 1, exhausted(c0, c1), acc, c0, c1

    zero_c = jnp.zeros((tq, 1), F32)
    acc, c0, c1 = blocks([(qi, True, None), (jnp.maximum(qi - 1, 0), False, qi > 0)],
                         (jnp.zeros((tq, LANES), F32), zero_c, zero_c))
    state = lax.while_loop(lambda s: jnp.logical_and(s[0] >= 0, s[1] == 0), step,
                           (qi - 2, exhausted(c0, c1), acc, c0, c1))
    o_ref[...] = (state[2] * zs_ref[...]).astype(o_ref.dtype)


def _sb_prompt(q, k, v, zs, batch, seq, tq):
    rows, W = q.shape
    nq = seq // tq
    pairs = W // LANES
    return pl.pallas_call(
        functools.partial(_sb_prompt_kernel, tq=tq),
        grid=(batch, pairs, nq),
        in_specs=[pl.BlockSpec((tq, LANES), lambda b, p, i: (b * nq + i, p)),
                  pl.BlockSpec((seq, LANES), lambda b, p, i: (b, p)),
                  pl.BlockSpec((seq, LANES), lambda b, p, i: (b, p)),
                  pl.BlockSpec((tq, LANES), lambda b, p, i: (b * nq + i, p))],
        out_specs=pl.BlockSpec((tq, LANES), lambda b, p, i: (b * nq + i, p)),
        out_shape=jax.ShapeDtypeStruct((rows, W), BF16),
        scratch_shapes=[pltpu.VMEM((1, LANES), F32)],
        compiler_params=_params(("arbitrary", "arbitrary", "arbitrary")),
        name="sb_prompt",
    )(q, k, v, zs)


def _moba_prompt_kernel(q_ref, qf_ref, k_ref, v_ref, km_ref, zs_ref, o_ref, kpad_ref, *, tq, nb):
    qi = pl.program_id(2)

    @pl.when(qi == 0)
    def _():
        kpad_ref[...] = jnp.zeros_like(kpad_ref)
        kpad_ref[0:nb, :] = km_ref[0]

    q = q_ref[...]
    qf = qf_ref[...]
    lane = _iota((tq, LANES), 1)
    head0 = lane < HEAD_DIM
    row = _iota((tq, tq), 0)
    col = _iota((tq, tq), 1)
    kmean = kpad_ref[...]
    nbp = -(-nb // SUBLANES) * SUBLANES
    blk = _iota((nbp, tq), 0)
    blk_f = blk.astype(F32)

    qs, flags = [], []
    for h in range(2):
        hm = head0 if h == 0 else jnp.logical_not(head0)
        qs.append(jnp.where(hm, q, jnp.zeros_like(q)))
        g = _dot_nt(kmean, jnp.where(hm, qf, 0.0), HIGHEST)[0:nbp, :]
        g = jnp.where(blk < qi, g, NEG_INF)
        sel = jnp.zeros((nbp, tq), F32)
        for _ in range(MOBA_TOPK):
            mx = jnp.max(g, axis=0, keepdims=True)
            is_max = jnp.logical_and(g == mx, mx > NEG_INF)
            idx = jnp.min(jnp.where(is_max, blk_f, float(nbp)), axis=0, keepdims=True)
            pick = blk_f == idx
            sel = jnp.where(pick, 1.0, sel)
            g = jnp.where(pick, NEG_INF, g)
        not_sel = jnp.concatenate([jnp.where(sel > 0.0, 0.0, 1.0), jnp.ones((LANES - nbp, tq), F32)], axis=0)
        flags.append(jnp.transpose(not_sel))

    lhs = [jnp.where(head0, qs[0], pltpu.roll(flags[0], HEAD_DIM, 1).astype(BF16)),
           jnp.where(head0, flags[1].astype(BF16), qs[1])]
    ones = jnp.ones((tq, LANES), BF16)

    def values(vb):
        return jnp.where(head0, vb, ones), jnp.where(head0, ones, vb)

    start = pl.multiple_of(qi * tq, tq)
    kb = k_ref[pl.ds(start, tq), :]
    vmods = values(v_ref[pl.ds(start, tq), :])
    state = []
    for h in range(2):
        s = jnp.where(col <= row, _dot_nt(qs[h], kb), NEG_INF)
        m = jnp.max(s, axis=1, keepdims=True)
        p = jnp.exp(s - m)
        state += [m, _dot(p.astype(BF16), vmods[h])]

    def blocks(ns, st):
        ks, vs = [], []
        for n in ns:
            start_n = pl.multiple_of(n * tq, tq)
            kn = k_ref[pl.ds(start_n, tq), :]
            pen = [jnp.where(lane == n + off, MOBA_MASK_PENALTY, 0.0).astype(BF16) for off in (HEAD_DIM, 0)]
            ks.append((jnp.where(head0, kn, pen[0]), jnp.where(head0, pen[1], kn)))
            vs.append(values(v_ref[pl.ds(start_n, tq), :]))
        ss = [[_dot_nt(lhs[h], kn[h]) for kn in ks] for h in range(2)]
        new = []
        for h in range(2):
            m, acc = st[2 * h:2 * h + 2]
            m_new = m
            for s in ss[h]:
                m_new = jnp.maximum(m_new, jnp.max(s, axis=1, keepdims=True))
            ps = [jnp.exp((s - m_new).astype(BF16)) for s in ss[h]]
            acc = jnp.exp(m - m_new) * acc
            for p, vn in zip(ps, vs):
                acc = acc + _dot(p, vn[h])
            new += [m_new, acc]
        return tuple(new)

    group = MOBA_GROUP if nb % MOBA_GROUP == 0 else 1
    st = lax.fori_loop(0, (qi + group - 1) // group,
                       lambda i, s: blocks([group * i + r for r in range(group)], s), tuple(state))
    o = jnp.where(head0, st[1] / pltpu.roll(st[1], HEAD_DIM, 1), st[3] / pltpu.roll(st[3], HEAD_DIM, 1))
    o_ref[...] = (o * zs_ref[...]).astype(o_ref.dtype)


def _moba_prompt(q, qf, k, v, kmean, zs, batch, seq):
    rows, W = q.shape
    tq = MOBA_BLOCK
    nq = seq // tq
    pairs = W // LANES
    assert nq <= HEAD_DIM
    return pl.pallas_call(
        functools.partial(_moba_prompt_kernel, tq=tq, nb=nq),
        grid=(batch, pairs, nq),
        in_specs=[pl.BlockSpec((tq, LANES), lambda b, p, i: (b * nq + i, p)),
                  pl.BlockSpec((tq, LANES), lambda b, p, i: (b * nq + i, p)),
                  pl.BlockSpec((seq, LANES), lambda b, p, i: (b, p)),
                  pl.BlockSpec((seq, LANES), lambda b, p, i: (b, p)),
                  pl.BlockSpec((1, nq, LANES), lambda b, p, i: (b, 0, p)),
                  pl.BlockSpec((tq, LANES), lambda b, p, i: (b * nq + i, p))],
        out_specs=pl.BlockSpec((tq, LANES), lambda b, p, i: (b * nq + i, p)),
        out_shape=jax.ShapeDtypeStruct((rows, W), BF16),
        scratch_shapes=[pltpu.VMEM((LANES, LANES), F32)],
        compiler_params=_params(("arbitrary", "arbitrary", "arbitrary")),
        name="moba_prompt",
    )(q, qf, k, v, kmean.reshape(batch, nq, W), zs)


def _mlstm_kernel(qk_ref, v_ref, gb_ref, gt_ref, conv0_ref, c0_ref, m0_ref,
                  cw_ref, cb_ref, brow_ref,
                  y_ref, conv_out_ref, c_out_ref, m_out_ref,
                  xpad_ref, caug_ref, m_ref, *, L):
    c = pl.program_id(1)
    nc = pl.num_programs(1)
    H, DK = ML_HEADS, ML_HEAD_DIM
    W = HALF_WIDTH
    PADR = SUBLANES

    @pl.when(c == 0)
    def _():
        xpad_ref[0:PADR, :] = conv0_ref[0]
        caug_ref[...] = c0_ref[0]
        m_ref[...] = m0_ref[0]

    xpad_ref[PADR:PADR + L, :] = qk_ref[...]
    y = cb_ref[...]
    for j in range(CONV_WIDTH):
        off = PADR - (CONV_WIDTH - 1) + j
        y = y + cw_ref[j:j + 1, :] * xpad_ref[off:off + L, :]
    tail = xpad_ref[L:L + PADR, :]
    xpad_ref[0:PADR, :] = tail
    conv_out_ref[0] = tail
    qk = _silu(y)

    lane = _iota((L, LANES), 1)
    is_f_col = jnp.logical_and(lane >= H, lane < 2 * H)
    gcol = gt_ref[0] + brow_ref[...]
    lf_col = jnp.where(is_f_col, _log_sigmoid(gcol), 0.0)
    grow = jnp.transpose(gcol)[0:2 * H, :]
    sub = _iota((2 * H, L), 0)
    lf_row = jnp.where(sub >= H, _log_sigmoid(grow), 0.0)
    row = _iota((L, L), 0)
    col = _iota((L, L), 1)
    causal = col <= row
    tri = jnp.where(causal, 1.0, 0.0)
    f_col = _dot(tri, lf_col, HIGHEST)
    f_row = _dot_nt(lf_row, tri, HIGHEST)
    ones_col = jnp.where(_iota((L, DK), 1) == 0, 1.0, 0.0).astype(BF16)
    m_all = m_ref[...]
    m_next = m_all
    lane1 = _iota((1, LANES), 1)

    for h in range(H):
        fc = f_col[:, H + h:H + h + 1]
        fr = f_row[H + h:H + h + 1, :]
        li_c = gcol[:, h:h + 1]
        li_r = grow[h:h + 1, :]
        m_prev = m_all[:, h:h + 1]
        inter = fc + m_prev
        intra = jnp.where(causal, fc - fr + li_r, NEG_INF)
        mt = jnp.maximum(inter, jnp.max(intra, axis=1, keepdims=True))
        w = jnp.exp(intra - mt)
        g = jnp.exp(inter - mt)
        qh = qk[:, h * DK:(h + 1) * DK].astype(BF16)
        kf = qk[:, W + h * DK:W + (h + 1) * DK] * (DK ** -0.5)
        kh = kf.astype(BF16)
        vaug = jnp.concatenate([v_ref[:, h * DK:(h + 1) * DK].astype(BF16), ones_col], axis=1)
        s = _dot_nt(qh, kh) * w
        nd = g * _dot(qh, caug_ref[h].astype(BF16)) + _dot(s.astype(BF16), vaug)
        den = nd[:, DK:DK + 1]
        hout = nd[:, 0:DK] / jnp.maximum(jnp.abs(den), jnp.exp(-mt))
        y_ref[:, h * DK:(h + 1) * DK] = (gb_ref[:, h * DK:(h + 1) * DK] * hout).astype(y_ref.dtype)
        m_new = mt[L - 1:L, :]
        f_last = fc[L - 1:L, :]
        decay = jnp.exp(f_last + m_prev - m_new)
        ws = jnp.exp(f_last - fc + li_c - m_new)
        caug_ref[h] = decay * caug_ref[h] + _dot_tn((kf * ws).astype(BF16), vaug)
        m_next = jnp.where(lane1 == h, m_new, m_next)

    m_ref[...] = m_next

    @pl.when(c == nc - 1)
    def _():
        c_out_ref[0] = caug_ref[...]
        m_out_ref[0] = m_next


def _mlstm(qk_pre, vb, gb, gates, conv0, c0aug, m0, conv_w, conv_b, b_ig, b_fg, batch, L):
    rows, W2 = qk_pre.shape
    W = HALF_WIDTH
    H = ML_HEADS
    nc = rows // batch // L
    brow = jnp.pad(jnp.concatenate([b_ig, b_fg]).reshape(1, 2 * H), ((0, 0), (0, LANES - 2 * H)))
    y_dtype = BF16 if L % 16 == 0 else F32
    out_shapes = [jax.ShapeDtypeStruct((rows, W), y_dtype),
                  jax.ShapeDtypeStruct((batch, SUBLANES, W2), F32),
                  jax.ShapeDtypeStruct((batch, H, ML_HEAD_DIM, 2 * ML_HEAD_DIM), F32),
                  jax.ShapeDtypeStruct((batch, 1, LANES), F32)]
    return pl.pallas_call(
        functools.partial(_mlstm_kernel, L=L),
        grid=(batch, nc),
        in_specs=[pl.BlockSpec((L, W2), lambda b, c: (b * nc + c, 0)),
                  pl.BlockSpec((L, W), lambda b, c: (b * nc + c, 0)),
                  pl.BlockSpec((L, W), lambda b, c: (b * nc + c, 0)),
                  pl.BlockSpec((1, L, LANES), lambda b, c: (b * nc + c, 0, 0)),
                  pl.BlockSpec((1, SUBLANES, W2), lambda b, c: (b, 0, 0)),
                  pl.BlockSpec((1, H, ML_HEAD_DIM, 2 * ML_HEAD_DIM), lambda b, c: (b, 0, 0, 0)),
                  pl.BlockSpec((1, 1, LANES), lambda b, c: (b, 0, 0)),
                  pl.BlockSpec((CONV_WIDTH, W2), lambda b, c: (0, 0)),
                  pl.BlockSpec((1, W2), lambda b, c: (0, 0)),
                  pl.BlockSpec((1, LANES), lambda b, c: (0, 0))],
        out_specs=[pl.BlockSpec((L, W), lambda b, c: (b * nc + c, 0)),
                   pl.BlockSpec((1, SUBLANES, W2), lambda b, c: (b, 0, 0)),
                   pl.BlockSpec((1, H, ML_HEAD_DIM, 2 * ML_HEAD_DIM), lambda b, c: (b, 0, 0, 0)),
                   pl.BlockSpec((1, 1, LANES), lambda b, c: (b, 0, 0))],
        out_shape=out_shapes,
        scratch_shapes=[pltpu.VMEM((L + SUBLANES, W2), F32),
                        pltpu.VMEM((H, ML_HEAD_DIM, 2 * ML_HEAD_DIM), F32),
                        pltpu.VMEM((1, LANES), F32)],
        compiler_params=_params(("arbitrary", "arbitrary")),
        name="mlstm",
    )(qk_pre, vb, gb, gates, conv0, c0aug, m0, conv_w, conv_b.reshape(1, W2), brow)


def _rglru_gates(xc, wr_ref, br_ref, wi_ref, bi_ref, lam_ref):
    xb = xc.astype(BF16)
    r = _sigmoid(_dot(xb, wr_ref[...]) + br_ref[...])
    i = _sigmoid(_dot(xb, wi_ref[...]) + bi_ref[...])
    log_a = RG_C * r * _log_sigmoid(lam_ref[...])
    a = jnp.exp(log_a)
    b = jnp.sqrt(-jnp.tanh(log_a) * (a * a + 1.0)) * (i * xc)
    return a, b


def _rglru_prompt_kernel(xd_ref, zs_ref, conv0_ref, h0_ref, cw_ref, cb_ref, wr_ref, br_ref, wi_ref, bi_ref,
                         lam_ref, y_ref, conv_out_ref, h_out_ref,
                         xpad_ref, a_ref, b_ref, hs_ref, hc_ref, *, L):
    c = pl.program_id(1)
    PADR = SUBLANES

    @pl.when(c == 0)
    def _():
        xpad_ref[0:PADR, :] = conv0_ref[0]
        hc_ref[...] = h0_ref[0]

    xpad_ref[PADR:PADR + L, :] = xd_ref[...]
    xc = cb_ref[...]
    for j in range(CONV_WIDTH):
        off = PADR - (CONV_WIDTH - 1) + j
        xc = xc + cw_ref[j:j + 1, :] * xpad_ref[off:off + L, :]
    tail = xpad_ref[L:L + PADR, :]
    xpad_ref[0:PADR, :] = tail
    conv_out_ref[0] = tail

    a, b = _rglru_gates(xc, wr_ref, br_ref, wi_ref, bi_ref, lam_ref)
    a_ref[...] = a
    b_ref[...] = b

    def step(t, h):
        h = a_ref[pl.ds(t, 1), :] * h + b_ref[pl.ds(t, 1), :]
        hs_ref[pl.ds(t, 1), :] = h
        return h

    h = lax.fori_loop(0, L, step, hc_ref[...], unroll=8)
    hc_ref[...] = h
    h_out_ref[0] = h
    y_ref[...] = (hs_ref[...] * zs_ref[...]).astype(y_ref.dtype)


def _block_diag(w):
    g, n, _ = w.shape
    eye = jnp.eye(g, dtype=w.dtype)
    return (eye[:, None, :, None] * w[:, :, None, :]).reshape(g * n, g * n)


def _rglru_prompt(xd, zs, conv0, h0, conv_w, conv_b, wr, br, wi, bi, lam, batch, L):
    rows, W = xd.shape
    nc = rows // batch // L
    vec = lambda a: a.reshape(1, W)
    const = lambda shape: pl.BlockSpec(shape, lambda b, c: tuple(0 for _ in shape))
    return pl.pallas_call(
        functools.partial(_rglru_prompt_kernel, L=L),
        grid=(batch, nc),
        in_specs=[pl.BlockSpec((L, W), lambda b, c: (b * nc + c, 0)),
                  pl.BlockSpec((L, W), lambda b, c: (b * nc + c, 0)),
                  pl.BlockSpec((1, SUBLANES, W), lambda b, c: (b, 0, 0)),
                  pl.BlockSpec((1, 1, W), lambda b, c: (b, 0, 0)),
                  const((CONV_WIDTH, W)), const((1, W)), const((W, W)), const((1, W)), const((W, W)),
                  const((1, W)), const((1, W))],
        out_specs=[pl.BlockSpec((L, W), lambda b, c: (b * nc + c, 0)),
                   pl.BlockSpec((1, SUBLANES, W), lambda b, c: (b, 0, 0)),
                   pl.BlockSpec((1, 1, W), lambda b, c: (b, 0, 0))],
        out_shape=[jax.ShapeDtypeStruct((rows, W), BF16),
                   jax.ShapeDtypeStruct((batch, SUBLANES, W), F32),
                   jax.ShapeDtypeStruct((batch, 1, W), F32)],
        scratch_shapes=[pltpu.VMEM((L + SUBLANES, W), F32), pltpu.VMEM((L, W), F32), pltpu.VMEM((L, W), F32),
                        pltpu.VMEM((L, W), F32), pltpu.VMEM((1, W), F32)],
        compiler_params=_params(("arbitrary", "arbitrary")),
        name="rglru_prompt",
    )(xd, zs, conv0, h0, conv_w, vec(conv_b), _block_diag(wr).astype(BF16), vec(br),
      _block_diag(wi).astype(BF16), vec(bi), vec(lam))


def _rglru_sample_kernel(xd_ref, zs_ref, conv0_ref, h0_ref, cw_ref, cb_ref, wr_ref, br_ref, wi_ref, bi_ref,
                         lam_ref, y_ref, h_out_ref, *, T):
    xs = [conv0_ref[j] for j in range(CONV_WIDTH - 1)] + [xd_ref[t] for t in range(T)]
    h = h0_ref[...]
    for t in range(T):
        xc = cb_ref[...]
        for j in range(CONV_WIDTH):
            xc = xc + cw_ref[j:j + 1, :] * xs[t + j]
        a, b = _rglru_gates(xc, wr_ref, br_ref, wi_ref, bi_ref, lam_ref)
        h = a * h + b
        y_ref[t] = h * zs_ref[t]
    h_out_ref[...] = h


def _rglru_sample(xd, zs, conv0, h0, conv_w, conv_b, wr, br, wi, bi, lam):
    T, B, W = xd.shape
    vec = lambda a: a.reshape(1, W)
    return pl.pallas_call(
        functools.partial(_rglru_sample_kernel, T=T),
        out_shape=[jax.ShapeDtypeStruct((T, B, W), F32), jax.ShapeDtypeStruct((B, W), F32)],
        compiler_params=pltpu.CompilerParams(vmem_limit_bytes=VMEM_LIMIT),
        name="rglru_sample",
    )(xd, zs, conv0, h0, conv_w, vec(conv_b), _block_diag(wr).astype(BF16), vec(br),
      _block_diag(wi).astype(BF16), vec(bi), vec(lam))


def _query_rows(q):
    b, t, w = q.shape
    heads = w // HEAD_DIM
    onehot = (jnp.arange(heads)[:, None] == (jnp.arange(w) // HEAD_DIM)[None, :]).astype(q.dtype)
    return (q[:, :, None, :] * onehot[None, None]).reshape(b, t * heads, w)


def _channel_major(a, lanes):
    return jnp.pad(jnp.transpose(a, (0, 2, 1)), ((0, 0), (0, 0), (0, lanes - a.shape[1])))


def _pool_channel_major(pool):
    n_pool, page_rows, heads, hd = pool.shape
    return jnp.transpose(pool, (0, 2, 3, 1)).reshape(n_pool, heads * hd, page_rows)


def _page_specs(pages_per_step, page_rows, width, page_of):
    def spec(i):
        return pl.BlockSpec((None, width, page_rows), lambda b, c, pt: (pt[b, page_of(c, i)], 0, 0))
    return [spec(i) for i in range(pages_per_step)]


def _head_diagonal(o, t_new, zs):
    heads = HALF_WIDTH // HEAD_DIM
    rows = t_new * heads
    keep = (_iota((rows, HALF_WIDTH), 1) // HEAD_DIM) == (_iota((rows, HALF_WIDTH), 0) % heads)
    o = jnp.where(keep, o, 0.0)
    return jnp.sum(o.reshape(t_new, heads, HALF_WIDTH), axis=1) * zs


def _key_absmax_kernel(pt_ref, *rest, P):
    kpages, o_ref = rest[:P], rest[P]
    c = pl.program_id(1)
    m = jnp.abs(kpages[0][...])
    for i in range(1, P):
        m = jnp.maximum(m, jnp.abs(kpages[i][...]))

    @pl.when(c == 0)
    def _():
        o_ref[0] = m

    @pl.when(c > 0)
    def _():
        o_ref[0] = jnp.maximum(o_ref[0], m)


def _key_absmax(page_table, k_pool, P):
    B, n_pages = page_table.shape
    W, page_rows = k_pool.shape[1], k_pool.shape[2]
    grid_spec = pltpu.PrefetchScalarGridSpec(
        num_scalar_prefetch=1,
        grid=(B, n_pages // P),
        in_specs=_page_specs(P, page_rows, W, lambda c, i: c * P + i),
        out_specs=pl.BlockSpec((1, W, page_rows), lambda b, c, pt: (b, 0, 0)),
    )
    return pl.pallas_call(
        functools.partial(_key_absmax_kernel, P=P),
        grid_spec=grid_spec,
        out_shape=jax.ShapeDtypeStruct((B, W, page_rows), F32),
        compiler_params=_params(("arbitrary", "arbitrary")),
        name="key_absmax",
    )(page_table, *([k_pool] * P))


def _sb_sample_kernel(pt_ref, q_ref, knew_ref, vnew_ref, zs_ref, kabs_ref, acc0_ref, car0_ref, *rest,
                      P, t_new, newest):
    kpages, vpages = rest[:P], rest[P:2 * P]
    o_ref, acc_out_ref, car_out_ref, done_ref, acc_ref, car_ref = rest[2 * P:]
    c = pl.program_id(1)
    nch = pl.num_programs(1)
    q = q_ref[0]
    heads = HALF_WIDTH // HEAD_DIM
    rows = t_new * heads

    def key_blocks(kts, vts, mask):
        n = kts[0].shape[1]
        suffix = jnp.where(_iota((n, n), 0) >= _iota((n, n), 1), 1.0, 0.0).astype(BF16)
        zs = [_dot(q, kt.astype(BF16)) for kt in kts]
        lss = [-_softplus_scores(z) for z in zs]
        if mask is not None:
            lss = [jnp.where(mask, ls, 0.0) for ls in lss]
        sums = []
        for ls in lss:
            hi, lo = _split_bf16(ls)
            sums.append(_dot(hi, suffix) + _dot(lo, suffix))
        carry = car_ref[...]
        acc = acc_ref[...]
        for z, s, vt in zip(zs, sums, vts):
            w = jnp.exp(z + s + carry)
            if mask is not None:
                w = jnp.where(mask, w, 0.0)
            acc = acc + _dot_nt(w.astype(BF16), vt.astype(BF16))
            carry = carry + s[:, 0:1]
        acc_ref[...] = acc
        car_ref[...] = carry

    @pl.when(c == 0)
    def _():
        if newest:
            acc_ref[...] = jnp.zeros_like(acc_ref)
            car_ref[...] = jnp.zeros_like(car_ref)
            n = knew_ref.shape[2]
            j = _iota((rows, n), 1)
            t = _iota((rows, n), 0) // heads
            key_blocks([knew_ref[0]], [vnew_ref[0]], jnp.logical_and(j < t, j < t_new))
        else:
            acc_ref[...] = acc0_ref[0]
            car_ref[...] = car0_ref[0][:, 0:1]

    key_blocks([kpages[i][...] for i in reversed(range(P))], [vpages[i][...] for i in reversed(range(P))], None)

    @pl.when(c == nch - 1)
    def _():
        o_ref[0] = _head_diagonal(acc_ref[...], t_new, zs_ref[0])
        acc_out_ref[0] = acc_ref[...]
        car_out_ref[0] = jnp.broadcast_to(car_ref[...], (rows, LANES))
        bound = jnp.max(_dot(jnp.abs(q.astype(F32)), kabs_ref[0], HIGHEST), axis=1, keepdims=True)
        worst = jnp.max(bound * SB_BOUND_SLACK + car_ref[...])
        done_ref[0] = jnp.where(worst < SB_EXP_UNDERFLOW, 1.0, 0.0) * jnp.ones((1, LANES), F32)


def _sb_sample_pages(page_table, q_rows, knew_t, vnew_t, zs, kabs, acc0, car0, k_pool, v_pool, P, first_page,
                     n_used, newest):
    B = page_table.shape[0]
    t_new = zs.shape[1]
    W = HALF_WIDTH
    rows = q_rows.shape[1]
    page_rows = k_pool.shape[2]
    nch = n_used // P
    page_of = lambda c, i: first_page + (nch - 1 - c) * P + i
    per_b = lambda shape: pl.BlockSpec((1,) + shape, lambda b, c, pt: (b,) + tuple(0 for _ in shape))
    grid_spec = pltpu.PrefetchScalarGridSpec(
        num_scalar_prefetch=1,
        grid=(B, nch),
        in_specs=[per_b((rows, W)), per_b(knew_t.shape[1:]), per_b(vnew_t.shape[1:]), per_b((t_new, W)),
                  per_b((W, page_rows)), per_b((rows, W)), per_b((rows, LANES))]
        + _page_specs(P, page_rows, W, page_of) + _page_specs(P, page_rows, W, page_of),
        out_specs=[per_b((t_new, W)), per_b((rows, W)), per_b((rows, LANES)), per_b((1, LANES))],
        scratch_shapes=[pltpu.VMEM((rows, W), F32), pltpu.VMEM((rows, 1), F32)],
    )
    return pl.pallas_call(
        functools.partial(_sb_sample_kernel, P=P, t_new=t_new, newest=newest),
        grid_spec=grid_spec,
        out_shape=[jax.ShapeDtypeStruct((B, t_new, W), F32), jax.ShapeDtypeStruct((B, rows, W), F32),
                   jax.ShapeDtypeStruct((B, rows, LANES), F32), jax.ShapeDtypeStruct((B, 1, LANES), F32)],
        compiler_params=_params(("arbitrary", "arbitrary")),
        name="sb_sample_newest" if newest else "sb_sample_older",
    )(page_table, q_rows, knew_t, vnew_t, zs, kabs, acc0, car0, *([k_pool] * P), *([v_pool] * P))


def _sb_sample(page_table, q_rows, knew_t, vnew_t, zs, k_pool, v_pool, p_scan, p_newest):
    B, n_pages = page_table.shape
    rows = q_rows.shape[1]
    p_newest = min(p_newest, n_pages)
    kabs = _key_absmax(page_table, k_pool, p_scan)
    y, acc, car, done = _sb_sample_pages(
        page_table, q_rows, knew_t, vnew_t, zs, kabs, jnp.zeros((B, rows, HALF_WIDTH), F32),
        jnp.zeros((B, rows, LANES), F32), k_pool, v_pool, p_newest, n_pages - p_newest, p_newest, True)
    n_older = n_pages - p_newest
    if n_older == 0:
        return y
    p_older = max(p for p in range(1, p_scan + 1) if n_older % p == 0)
    older = lambda: _sb_sample_pages(page_table, q_rows, knew_t, vnew_t, zs, kabs, acc, car, k_pool, v_pool,
                                     p_older, 0, n_older, False)[0]
    return lax.cond(jnp.min(done) > 0.5, lambda: y, older)


def _moba_sample_kernel(pt_ref, q_ref, qg_ref, knew_ref, vnew_ref, zs_ref, *rest, P, t_new, nb, pb):
    kpages, vpages = rest[:P], rest[P:2 * P]
    o_ref, m_ref, l_ref, acc_ref, km_ref = rest[2 * P:]
    c = pl.program_id(1)
    nch = pl.num_programs(1)
    q = q_ref[0]
    heads = HALF_WIDTH // HEAD_DIM
    rows = t_new * heads
    lane = _iota((rows, LANES), 1)

    @pl.when(c == 0)
    def _():
        km_ref[...] = jnp.zeros_like(km_ref)
        m_ref[...] = jnp.zeros_like(m_ref)
        l_ref[...] = jnp.zeros_like(l_ref)

    nblk = P // pb
    zs = [[_dot(q, kpages[pb * s + r][...].astype(BF16)) for r in range(pb)] for s in range(nblk)]
    km_all, m_all, l_all = km_ref[...], m_ref[...], l_ref[...]
    km_lane = _iota(km_ref.shape, 1)
    for s in range(nblk):
        n = c * nblk + s
        ksum = kpages[pb * s][...]
        for r in range(1, pb):
            ksum = ksum + kpages[pb * s + r][...]
        kmean = jnp.sum(ksum, axis=1, keepdims=True) * (1.0 / MOBA_BLOCK)
        km_all = jnp.where(km_lane == n, kmean, km_all)
        m = jnp.max(zs[s][0], axis=1, keepdims=True)
        for z in zs[s][1:]:
            m = jnp.maximum(m, jnp.max(z, axis=1, keepdims=True))
        l = jnp.zeros((rows, 1), F32)
        a = jnp.zeros((rows, HALF_WIDTH), F32)
        for r, z in enumerate(zs[s]):
            p = jnp.exp(z - m)
            l = l + jnp.sum(p, axis=1, keepdims=True)
            a = a + _dot_nt(p.astype(BF16), vpages[pb * s + r][...].astype(BF16))
        m_all = jnp.where(lane == n, m, m_all)
        l_all = jnp.where(lane == n, l, l_all)
        acc_ref[n] = a
    km_ref[...] = km_all
    m_ref[...] = m_all
    l_ref[...] = l_all

    @pl.when(c == nch - 1)
    def _():
        lane_f = lane.astype(F32)
        g = _dot(qg_ref[0], km_ref[...], HIGHEST)
        g = jnp.where(lane < nb, g, NEG_INF)
        sel = jnp.zeros((rows, LANES), F32)
        for _ in range(min(MOBA_TOPK, nb)):
            mx = jnp.max(g, axis=1, keepdims=True)
            is_max = jnp.logical_and(g == mx, mx > NEG_INF)
            idx = jnp.min(jnp.where(is_max, lane_f, float(LANES)), axis=1, keepdims=True)
            pick = lane_f == idx
            sel = jnp.where(pick, 1.0, sel)
            g = jnp.where(pick, NEG_INF, g)
        nn = knew_ref.shape[2]
        j = _iota((rows, nn), 1)
        t = _iota((rows, nn), 0) // heads
        zn = jnp.where(jnp.logical_and(j <= t, j < t_new), _dot(q, knew_ref[0].astype(BF16)), NEG_INF)
        m_all = m_ref[...]
        m_tot = jnp.maximum(jnp.max(jnp.where(sel > 0.0, m_all, NEG_INF), axis=1, keepdims=True),
                            jnp.max(zn, axis=1, keepdims=True))
        coef = jnp.where(sel > 0.0, jnp.exp(m_all - m_tot), 0.0)
        p_own = jnp.exp(zn - m_tot)
        denom = jnp.sum(coef * l_ref[...], axis=1, keepdims=True) + jnp.sum(p_own, axis=1, keepdims=True)
        o = _dot_nt(p_own.astype(BF16), vnew_ref[0].astype(BF16))
        for n in range(nb):
            o = o + coef[:, n:n + 1] * acc_ref[n]
        o_ref[0] = _head_diagonal(o / denom, t_new, zs_ref[0])


def _moba_sample(page_table, q_rows, qg_rows, knew_t, vnew_t, zs, k_pool, v_pool, P):
    B, n_pages = page_table.shape
    t_new = zs.shape[1]
    W = HALF_WIDTH
    rows = q_rows.shape[1]
    page_rows = k_pool.shape[2]
    pb = MOBA_BLOCK // page_rows
    nch = n_pages // P
    nb = n_pages // pb
    assert nb <= LANES and P % pb == 0
    page_of = lambda c, i: c * P + i
    per_b = lambda shape: pl.BlockSpec((1,) + shape, lambda b, c, pt: (b,) + tuple(0 for _ in shape))
    grid_spec = pltpu.PrefetchScalarGridSpec(
        num_scalar_prefetch=1,
        grid=(B, nch),
        in_specs=[per_b((rows, W)), per_b((rows, W)), per_b(knew_t.shape[1:]), per_b(vnew_t.shape[1:]),
                  per_b((t_new, W))]
        + _page_specs(P, page_rows, W, page_of) + _page_specs(P, page_rows, W, page_of),
        out_specs=per_b((t_new, W)),
        scratch_shapes=[pltpu.VMEM((rows, LANES), F32), pltpu.VMEM((rows, LANES), F32),
                        pltpu.VMEM((nb, rows, W), F32), pltpu.VMEM((W, LANES), F32)],
    )
    return pl.pallas_call(
        functools.partial(_moba_sample_kernel, P=P, t_new=t_new, nb=nb, pb=pb),
        grid_spec=grid_spec,
        out_shape=jax.ShapeDtypeStruct((B, t_new, W), F32),
        compiler_params=_params(("arbitrary", "arbitrary")),
        name="moba_sample",
    )(page_table, q_rows, qg_rows, knew_t, vnew_t, zs, *([k_pool] * P), *([v_pool] * P))


def _pad_rows(a, rows):
    return jnp.pad(a, ((0, 0), (0, rows - a.shape[1]), (0, 0)))


def kernel(x_prompt, x_sample, c_prompt, c_sample, page_table, cache_k_sb, cache_v_sb, state_conv_mlstm, state_c_mlstm, state_n_mlstm, state_m_mlstm, cache_k_moba, cache_v_moba, state_conv_rglru, state_h_rglru, norm_g_even, mod_w_even, mod_b_even, w_in_even, conv_w_even, conv_b_even, b_igate_even, b_fgate_even, w_out_even, norm_g_odd, mod_w_odd, mod_b_odd, w_in_odd, conv_w_odd, conv_b_odd, w_rgate_odd, b_rgate_odd, w_igate_odd, b_igate_odd, lru_lambda_odd, w_out_odd, final_g):
    B, T, D = x_prompt.shape
    DB, DT, _ = x_sample.shape
    W = HALF_WIDTH
    H = ML_HEADS
    heads = W // HEAD_DIM
    n_pool, page_rows = cache_k_sb.shape[1], cache_k_sb.shape[2]
    past = page_table.shape[1] * page_rows
    depth = norm_g_even.shape[0] + norm_g_odd.shape[0]
    TM = 256
    TP = 512 if T % 512 == 0 else TM
    n_pages = page_table.shape[1]
    PAGES_PER_STEP = max(p for p in range(2, 17, 2) if n_pages % p == 0)
    SB_NEWEST_PAGES = 4
    NEW_KEYS = LANES
    ML_PAD = SUBLANES

    xp = x_prompt.reshape(B * T, D)
    xs = x_sample.reshape(DB * DT, D)
    c_all = jnp.concatenate([c_prompt, c_sample], axis=0)
    c_rows = -(-c_all.shape[0] // SUBLANES) * SUBLANES
    c_all = jnp.pad(c_all, ((0, c_rows - c_all.shape[0]), (0, 0)))
    cos_p, sin_p = _rope_tables(jnp.arange(T))
    cos_s, sin_s = _rope_tables(past + jnp.arange(DT))
    cos_s, sin_s = jnp.tile(cos_s, (DB, 1)), jnp.tile(sin_s, (DB, 1))

    def mods(w, b):
        mod = _modulation(c_all, w, b)
        parts = []
        for part in jnp.split(mod, 3, axis=-1):
            p_part = part[:B].reshape(B, 1, D)
            s_part = jnp.repeat(part[B:B + DB], DT, axis=0).reshape(1, DB * DT, D)
            parts.append((p_part, s_part))
        return parts

    p_even, s_even, p_odd, s_odd = [], [], [], []
    y_prompt = y_sample = None
    for l in range(depth):
        j = l // 2
        last = l == depth - 1
        if l % 2 == 0:
            (sh_p, sh_s), (sc_p, sc_s), (ga_p, ga_s) = mods(mod_w_even[j], mod_b_even[j])
            qa, ka, kab, va, vab, za, qk, vb, gb, gt = _in_proj_even(
                xp, sh_p, sc_p, norm_g_even[j], w_in_even[j], TP)
            ya = _sb_prompt(qa, kab, vab, za, B, T, TM)
            yb, conv_p, caug_p, m_p = _mlstm(
                qk, vb, gb, gt.reshape(B * T // TM, TM, LANES),
                jnp.zeros((B, SUBLANES, 2 * W), F32), jnp.zeros((B, H, ML_HEAD_DIM, 2 * ML_HEAD_DIM), F32),
                jnp.zeros((B, 1, LANES), F32),
                conv_w_even[j], conv_b_even[j], b_igate_even[j], b_fgate_even[j], B, TM)
            xp = _out_proj(ya, yb, xp, ga_p, w_out_even[j], final_g, TP, last)
            p_even.append((ka.reshape(B, T, heads, HEAD_DIM), va.reshape(B, T, heads, HEAD_DIM),
                           conv_p[:, SUBLANES - (CONV_WIDTH - 1):], caug_p[..., :ML_HEAD_DIM],
                           caug_p[..., ML_HEAD_DIM], m_p[:, 0, :H]))
            qa, ka, kab, va, vab, za, qk, vb, gb, gt = _in_proj_even(
                xs, sh_s, sc_s, norm_g_even[j], w_in_even[j], DB * DT)
            ya = _sb_sample(page_table, _query_rows(qa.reshape(DB, DT, W)),
                            _channel_major(ka.reshape(DB, DT, W), NEW_KEYS),
                            _channel_major(va.reshape(DB, DT, W), NEW_KEYS), za.reshape(DB, DT, W),
                            _pool_channel_major(cache_k_sb[j]), _pool_channel_major(cache_v_sb[j]),
                            PAGES_PER_STEP, SB_NEWEST_PAGES).reshape(DB * DT, W)
            pad_t = lambda a: _pad_rows(a.reshape(DB, DT, a.shape[-1]), ML_PAD).reshape(DB * ML_PAD, a.shape[-1])
            g_rows = gt.reshape(DB, DT, LANES)[:, :, :2 * H]
            pad_gate = jnp.concatenate([jnp.full((H,), NEG_INF, F32), jnp.full((H,), jnp.inf, F32)])
            g_rows = jnp.concatenate([g_rows, jnp.broadcast_to(pad_gate, (DB, ML_PAD - DT, 2 * H))], axis=1)
            gt_s = jnp.pad(g_rows, ((0, 0), (0, 0), (0, LANES - 2 * H)))
            conv0 = jnp.pad(state_conv_mlstm[j], ((0, 0), (SUBLANES - (CONV_WIDTH - 1), 0), (0, 0)))
            c0aug = jnp.concatenate(
                [state_c_mlstm[j].astype(F32), state_n_mlstm[j].astype(F32)[..., None],
                 jnp.zeros((DB, H, ML_HEAD_DIM, ML_HEAD_DIM - 1), F32)], axis=-1)
            m0 = jnp.pad(state_m_mlstm[j].astype(F32), ((0, 0), (0, LANES - H))).reshape(DB, 1, LANES)
            yb, _, caug_s, m_s = _mlstm(
                pad_t(qk), pad_t(vb), pad_t(gb), gt_s, conv0, c0aug, m0,
                conv_w_even[j], conv_b_even[j], b_igate_even[j], b_fgate_even[j], DB, ML_PAD)
            yb = yb.reshape(DB, ML_PAD, W)[:, :DT].reshape(DB * DT, W)
            conv_s = jnp.concatenate([state_conv_mlstm[j].astype(F32), qk.reshape(DB, DT, 2 * W)],
                                     axis=1)[:, -(CONV_WIDTH - 1):]
            xs = _out_proj(ya, yb, xs, ga_s, w_out_even[j], final_g, DB * DT, last)
            s_even.append((ka.reshape(DB, DT, heads, HEAD_DIM), va.reshape(DB, DT, heads, HEAD_DIM),
                           conv_s, caug_s[..., :ML_HEAD_DIM], caug_s[..., ML_HEAD_DIM], m_s[:, 0, :H]))
        else:
            (sh_p, sh_s), (sc_p, sc_s), (ga_p, ga_s) = mods(mod_w_odd[j], mod_b_odd[j])
            lru = (conv_w_odd[j], conv_b_odd[j], w_rgate_odd[j], b_rgate_odd[j], w_igate_odd[j], b_igate_odd[j],
                   lru_lambda_odd[j])
            qb, qf, kc, kcb, vc, vcb, zc, xd, zd, km = _in_proj_odd(
                xp, sh_p, sc_p, norm_g_odd[j], w_in_odd[j], cos_p, sin_p, TP)
            yc = _moba_prompt(qb, qf, kcb, vcb, km, zc, B, T)
            yd, conv_p, h_p = _rglru_prompt(xd, zd, jnp.zeros((B, SUBLANES, W), F32), jnp.zeros((B, 1, W), F32),
                                            *lru, B, TM)
            xp = _out_proj(yc, yd, xp, ga_p, w_out_odd[j], final_g, TP, last)
            p_odd.append((kc.reshape(B, T, heads, HEAD_DIM), vc.reshape(B, T, heads, HEAD_DIM),
                          conv_p[:, SUBLANES - (CONV_WIDTH - 1):], h_p[:, 0]))
            qb, qf, kc, kcb, vc, vcb, zc, xd, zd, km = _in_proj_odd(
                xs, sh_s, sc_s, norm_g_odd[j], w_in_odd[j], cos_s, sin_s, DB * DT)
            yc = _moba_sample(page_table, _query_rows(qb.reshape(DB, DT, W)), _query_rows(qf.reshape(DB, DT, W)),
                              _channel_major(kc.reshape(DB, DT, W), NEW_KEYS),
                              _channel_major(vc.reshape(DB, DT, W), NEW_KEYS), zc.reshape(DB, DT, W),
                              _pool_channel_major(cache_k_moba[j]), _pool_channel_major(cache_v_moba[j]),
                              PAGES_PER_STEP).reshape(DB * DT, W)
            tmaj = lambda a: jnp.transpose(a.reshape(DB, DT, W), (1, 0, 2))
            yd, h_s = _rglru_sample(tmaj(xd), tmaj(zd), jnp.transpose(state_conv_rglru[j].astype(F32), (1, 0, 2)),
                                    state_h_rglru[j].astype(F32), *lru)
            yd = jnp.transpose(yd, (1, 0, 2)).reshape(DB * DT, W)
            conv_s = jnp.concatenate([state_conv_rglru[j].astype(F32), xd.reshape(DB, DT, W)],
                                     axis=1)[:, -(CONV_WIDTH - 1):]
            xs = _out_proj(yc, yd, xs, ga_s, w_out_odd[j], final_g, DB * DT, last)
            s_odd.append((kc.reshape(DB, DT, heads, HEAD_DIM), vc.reshape(DB, DT, heads, HEAD_DIM), conv_s, h_s))

    y_prompt = xp.reshape(B, T, D)
    y_sample = xs.reshape(DB, DT, D)
    stack = lambda group: [jnp.stack(a) for a in zip(*group)]
    pk_sb, pv_sb, pconv_m, pc_m, pn_m, pm_m = stack(p_even)
    sk_sb, sv_sb, sconv_m, sc_m, sn_m, sm_m = stack(s_even)
    pk_mb, pv_mb, pconv_d, ph_d = stack(p_odd)
    sk_mb, sv_mb, sconv_d, sh_d = stack(s_odd)
    return (y_prompt, y_sample,
            pk_sb, pv_sb, pconv_m, pc_m, pn_m, pm_m, pk_mb, pv_mb, pconv_d, ph_d,
            sk_sb, sv_sb, sconv_m, sc_m, sn_m, sm_m, sk_mb, sv_mb, sconv_d, sh_d)
```
